```python
import math
import jax, jax.numpy as jnp
from jax import lax
import numpy as np

D_MODEL = 1024
BATCH = 2
SEQ = 8192
DEPTH = 2

HEAD_DIM = 64
D_PLE = 256
A_WIDTH = D_MODEL // 2
A_HEADS = A_WIDTH // HEAD_DIM
MOBA_BLOCK = 256
MOBA_TOPK = 3
Q_BLOCK = 128
REL_BUCKETS = 32
REL_MAX_DIST = 128
B_WIDTH = D_MODEL // 4
POOL_WINDOWS = (2, 4, 8, 16)
B_GROUPS = len(POOL_WINDOWS)
B_GROUP = B_WIDTH // B_GROUPS
C_WIDTH = D_MODEL // 4
C_HEADS = C_WIDTH // HEAD_DIM
SGU_CHUNK = 128

D_MIX = A_WIDTH + B_WIDTH + C_WIDTH
IN_WIDTHS = (A_WIDTH,) * 4 + (B_WIDTH,) * 2 + (C_WIDTH,) * 3
D_IN = sum(IN_WIDTHS)
EPS = 1e-6
NEG = -1e30

kernel_name = "hybrid_moba_pool_sgu_trunk"


def rms_norm(x, g):
    x32 = x.astype(jnp.float32)
    y = x32 * lax.rsqrt(jnp.mean(x32 * x32, axis=-1, keepdims=True) + EPS)
    return (y * g.astype(jnp.float32)).astype(x.dtype)


def t5_bucket(n):
    max_exact = REL_BUCKETS // 2
    nf = jnp.maximum(n, 1).astype(jnp.float32)
    large = max_exact + (jnp.log(nf / max_exact) / math.log(REL_MAX_DIST / max_exact)
                         * (REL_BUCKETS - max_exact)).astype(jnp.int32)
    large = jnp.minimum(large, REL_BUCKETS - 1)
    return jnp.where(n < max_exact, n, large)


def moba_attention(q, k, v, rel_bias):
    B, S, H, D = q.shape
    nb = -(-S // MOBA_BLOCK)
    pad = nb * MOBA_BLOCK - S
    topk = min(MOBA_TOPK, nb)
    scale = D ** -0.5
    qh = jnp.transpose(q, (0, 2, 1, 3))
    kh = jnp.pad(jnp.transpose(k, (0, 2, 1, 3)), ((0, 0), (0, 0), (0, pad), (0, 0)))
    vh = jnp.pad(jnp.transpose(v, (0, 2, 1, 3)), ((0, 0), (0, 0), (0, pad), (0, 0)))
    k_blocks = kh.reshape(B, H, nb, MOBA_BLOCK, D)
    v_blocks = vh.reshape(B, H, nb, MOBA_BLOCK, D)
    k_mean = jnp.mean(k_blocks.astype(jnp.float32), axis=3)
    bias_tab = jnp.transpose(rel_bias).astype(jnp.float32)
    b_ix = jnp.arange(B)[:, None, None, None]
    h_ix = jnp.arange(H)[None, :, None, None]
    h_ix5 = jnp.arange(H)[None, :, None, None, None]
    blk_ids = jnp.arange(nb)
    in_blk = jnp.arange(MOBA_BLOCK)

    def q_block(qi):
        start = qi * Q_BLOCK
        qb = lax.dynamic_slice_in_dim(qh, start, Q_BLOCK, axis=2).astype(jnp.float32)
        t = start + jnp.arange(Q_BLOCK)
        own = start // MOBA_BLOCK
        gate = jnp.einsum('bhqd,bhnd->bhqn', qb, k_mean)
        gate = jnp.where(blk_ids < own, gate, NEG)
        top_val, top_idx = lax.top_k(gate, topk)
        sel_valid = top_val > NEG * 0.5
        k_sel = k_blocks[b_ix, h_ix, top_idx].astype(jnp.float32)
        v_sel = v_blocks[b_ix, h_ix, top_idx].astype(jnp.float32)
        sel_pos = top_idx[..., None] * MOBA_BLOCK + in_blk
        sel_bias = bias_tab[h_ix5, t5_bucket(t[:, None, None] - sel_pos)]
        sel_logits = jnp.einsum('bhqd,bhqnkd->bhqnk', qb, k_sel) * scale + sel_bias
        sel_logits = jnp.where(sel_valid[..., None], sel_logits, NEG)
        k_own = lax.dynamic_slice_in_dim(kh, own * MOBA_BLOCK, MOBA_BLOCK, axis=2).astype(jnp.float32)
        v_own = lax.dynamic_slice_in_dim(vh, own * MOBA_BLOCK, MOBA_BLOCK, axis=2).astype(jnp.float32)
        rel = t[:, None] - (own * MOBA_BLOCK + in_blk)[None, :]
        own_bias = bias_tab[:, t5_bucket(jnp.maximum(rel, 0))][None]
        own_logits = jnp.einsum('bhqd,bhkd->bhqk', qb, k_own) * scale + own_bias
        own_logits = jnp.where(rel >= 0, own_logits, NEG)
        logits = jnp.concatenate(
            [sel_logits.reshape(B, H, Q_BLOCK, topk * MOBA_BLOCK), own_logits], axis=-1)
        probs = jax.nn.softmax(logits, axis=-1)
        p_sel = probs[..., :topk * MOBA_BLOCK].reshape(B, H, Q_BLOCK, topk, MOBA_BLOCK)
        p_own = probs[..., topk * MOBA_BLOCK:]
        out = (jnp.einsum('bhqnk,bhqnkd->bhqd', p_sel, v_sel)
               + jnp.einsum('bhqk,bhkd->bhqd', p_own, v_own))
        return out.astype(q.dtype)

    outs = lax.map(q_block, jnp.arange(S // Q_BLOCK))
    return jnp.transpose(outs, (1, 0, 3, 2, 4)).reshape(B, S, H * D)


def multiscale_pool(xb, w_pool, pool_scale):
    B, S, C = xb.shape
    x32 = xb.astype(jnp.float32)
    cs = jnp.cumsum(x32, axis=1)
    count = jnp.arange(1, S + 1, dtype=jnp.float32)[None, :, None]
    groups = []
    for gi, w in enumerate(POOL_WINDOWS):
        c = cs[..., gi * B_GROUP:(gi + 1) * B_GROUP]
        lag = jnp.pad(c, ((0, 0), (w, 0), (0, 0)))[:, :S]
        mean = (c - lag) / jnp.minimum(count, float(w))
        groups.append(mean - x32[..., gi * B_GROUP:(gi + 1) * B_GROUP])
    pooled = jnp.stack(groups, axis=2)
    mixed = jnp.einsum('bsgc,gcd->bsgd', pooled, w_pool.astype(jnp.float32)).reshape(B, S, C)
    return (mixed * pool_scale.astype(jnp.float32)).astype(xb.dtype)


def spatial_gating(u, v, w_s, b_s):
    B, S, C = v.shape
    v32 = v.astype(jnp.float32)
    mu = jnp.mean(v32, axis=-1, keepdims=True)
    var = jnp.mean(jnp.square(v32 - mu), axis=-1, keepdims=True)
    vn = ((v32 - mu) * lax.rsqrt(var + EPS)).reshape(B, S // SGU_CHUNK, SGU_CHUNK, C_HEADS, C // C_HEADS)
    w_c = jnp.tril(w_s.astype(jnp.float32))
    mixed = (jnp.einsum('hts,bnshc->bnthc', w_c, vn)
             + jnp.transpose(b_s.astype(jnp.float32))[None, None, :, :, None])
    return (u.astype(jnp.float32) * mixed.reshape(B, S, C)).astype(u.dtype)


def setup_inputs(seed: int = 0) -> dict:
    key = jax.random.key(seed)
    ks = jax.random.split(key, 14)
    f32 = jnp.float32
    T = SGU_CHUNK
    return {
        'x': jax.random.normal(ks[0], (BATCH, SEQ, D_MODEL), f32),
        'p': jax.random.normal(ks[1], (DEPTH, BATCH, SEQ, D_PLE), f32),
        'norm_g': 1.0 + 0.1 * jax.random.normal(ks[2], (DEPTH, D_MODEL), f32),
        'w_in': jax.random.normal(ks[3], (DEPTH, D_MODEL, D_IN), f32) * D_MODEL ** -0.5,
        'w_out': jax.random.normal(ks[4], (DEPTH, D_MIX, D_MODEL), f32) * D_MIX ** -0.5,
        'rel_bias': 0.5 * jax.random.normal(ks[5], (REL_BUCKETS, A_HEADS), f32),
        'pool_w': jax.random.normal(ks[6], (DEPTH, B_GROUPS, B_GROUP, B_GROUP), f32) * B_GROUP ** -0.5,
        'pool_scale': 1.0 + 0.1 * jax.random.normal(ks[7], (DEPTH, B_WIDTH), f32),
        'sgu_w': jax.random.normal(ks[8], (DEPTH, C_HEADS, T, T), f32) * T ** -0.5,
        'sgu_b': 1.0 + 0.1 * jax.random.normal(ks[9], (DEPTH, C_HEADS, T), f32),
        'ple_w': jax.random.normal(ks[10], (DEPTH, D_PLE, D_MODEL), f32) * D_PLE ** -0.5,
        'ple_gate_w': jax.random.normal(ks[11], (DEPTH, D_MODEL, D_MODEL), f32) * D_MODEL ** -0.5,
        'final_g': 1.0 + 0.1 * jax.random.normal(ks[12], (D_MODEL,), f32),
    }


def reference(x, p, norm_g, w_in, w_out, rel_bias, pool_w, pool_scale, sgu_w, sgu_b,
              ple_w, ple_gate_w, final_g):
    B, S, _ = x.shape
    split_at = [int(s) for s in np.cumsum(IN_WIDTHS)[:-1]]
    h = x
    for i in range(DEPTH):
        hn = rms_norm(h, norm_g[i])
        z = hn @ w_in[i]
        qa, ka, va, ga, xb, gb, uc, vc, gc = jnp.split(z, split_at, axis=-1)
        ya = moba_attention(qa.reshape(B, S, A_HEADS, HEAD_DIM),
                            ka.reshape(B, S, A_HEADS, HEAD_DIM),
                            va.reshape(B, S, A_HEADS, HEAD_DIM), rel_bias) * jax.nn.silu(ga)
        yb = multiscale_pool(xb, pool_w[i], pool_scale[i]) * jax.nn.silu(gb)
        yc = spatial_gating(uc, vc, sgu_w[i], sgu_b[i]) * jax.nn.silu(gc)
        h = h + jnp.concatenate([ya, yb, yc], axis=-1) @ w_out[i]
        h = h + (p[i] @ ple_w[i]) * jax.nn.sigmoid(h @ ple_gate_w[i])
    return rms_norm(h, final_g)
```

```python
import functools
import math

import numpy as np
import jax
import jax.numpy as jnp
from jax import lax
from jax.experimental import pallas as pl
from jax.experimental.pallas import tpu as pltpu

D_MODEL = 1024
HEAD_DIM = 64
A_WIDTH = 512
A_HEADS = 8
MOBA_BLOCK = 256
MOBA_TOPK = 3
REL_BUCKETS = 32
REL_MAX_DIST = 128
B_WIDTH = 256
POOL_WINDOWS = (2, 4, 8, 16)
B_GROUP = 64
C_WIDTH = 256
C_HEADS = 4
SGU_CHUNK = 128
D_PLE = 256
D_IN = 3328
EPS = 1e-6
NEG = -1e30

MAX_WINDOW = max(POOL_WINDOWS)
ROW_TILE = MOBA_BLOCK
OUT_TILE = 512
HEAD_PAIR = 2 * HEAD_DIM
VMEM_LIMIT = 48 * 1024 * 1024

_OFF = np.cumsum((0,) + (A_WIDTH,) * 4 + (B_WIDTH,) * 2 + (C_WIDTH,) * 3)
Q0, K0, V0, GA0, XB0, GB0, UC0, VC0, GC0, _ = (int(o) for o in _OFF)

BF16 = jnp.bfloat16
F32 = jnp.float32


def _bucket_thresholds():
    max_exact = REL_BUCKETS // 2
    n = np.arange(0, 4 * MOBA_BLOCK)
    nf = np.maximum(n, 1).astype(np.float64)
    large = max_exact + (np.log(nf / max_exact) / math.log(REL_MAX_DIST / max_exact)
                         * (REL_BUCKETS - max_exact)).astype(np.int64)
    large = np.minimum(large, REL_BUCKETS - 1)
    bucket = np.where(n < max_exact, n, large)
    return [int(np.argmax(bucket >= b)) for b in range(1, REL_BUCKETS)]


_THRESHOLDS = _bucket_thresholds()


def _silu(x):
    return x * (1.0 / (1.0 + jnp.exp(-x)))


def _split_bf16(x):
    hi = x.astype(BF16)
    lo = (x - hi.astype(F32)).astype(BF16)
    return hi, lo


def _dot(a, b):
    return jnp.dot(a, b, preferred_element_type=F32)


def _dot_nt(a, b):
    return lax.dot_general(a, b, (((1,), (1,)), ((), ())), preferred_element_type=F32)


def _bias_kernel(rb_ref, out_ref):
    h = pl.program_id(0)
    key = lax.broadcasted_iota(jnp.int32, (MOBA_BLOCK, MOBA_BLOCK), 0)
    qry = lax.broadcasted_iota(jnp.int32, (MOBA_BLOCK, MOBA_BLOCK), 1)
    for kind, shift in ((0, 0), (1, MOBA_BLOCK)):
        dist = qry - key + shift
        val = jnp.full((MOBA_BLOCK, MOBA_BLOCK), rb_ref[0, h], F32)
        for b in range(1, REL_BUCKETS):
            val = jnp.where(dist >= _THRESHOLDS[b - 1], rb_ref[b, h], val)
        if kind == 0:
            val = jnp.where(dist >= 0, val, NEG)
        out_ref[0, kind] = val


def _bias_tiles(rel_bias):
    return pl.pallas_call(
        _bias_kernel,
        grid=(A_HEADS,),
        in_specs=[pl.BlockSpec(memory_space=pltpu.SMEM)],
        out_specs=pl.BlockSpec((1, 2, MOBA_BLOCK, MOBA_BLOCK), lambda h: (h, 0, 0, 0)),
        out_shape=jax.ShapeDtypeStruct((A_HEADS, 2, MOBA_BLOCK, MOBA_BLOCK), F32),
        name="bias_tiles",
    )(rel_bias)


def _proj_kernel(h_ref, ng_ref, win_ref, poolw_ref, pscale_ref, sguw_ref, sgub_ref,
                 q_ref, k_ref, vt_ref, gsa_ref, ybc_ref, mask_ref,
                 kmt_scr, ext_scr, *, n_blocks):
    T = ROW_TILE
    s = pl.program_id(1)

    @pl.when(s == 0)
    def _():
        kmt_scr[...] = jnp.zeros_like(kmt_scr)
        ext_scr[0:MAX_WINDOW, :] = jnp.zeros((MAX_WINDOW, B_WIDTH), F32)

    h = h_ref[0]
    hn = h * lax.rsqrt(jnp.mean(h * h, axis=-1, keepdims=True) + EPS) * ng_ref[...]
    hb = hn.astype(BF16)

    def proj(c0, width):
        return _dot(hb, win_ref[:, c0:c0 + width])

    zq = proj(Q0, A_WIDTH)
    q_ref[0] = (zq * HEAD_DIM ** -0.5).astype(BF16)
    zk = proj(K0, A_WIDTH)
    k_ref[0, 0] = zk.astype(BF16)
    zv = proj(V0, A_WIDTH)
    vt_ref[0, :, 0] = zv.T.astype(BF16).reshape(A_HEADS // 2, HEAD_PAIR, T)
    gsa_ref[0] = _silu(proj(GA0, A_WIDTH))

    km = kmt_scr[...]
    km_hi, km_lo = _split_bf16(km)
    q_hi, q_lo = _split_bf16(zq)
    gate_t = _dot_nt(km_hi, q_hi) + _dot_nt(km_hi, q_lo) + _dot_nt(km_lo, q_hi)
    jrow = lax.broadcasted_iota(jnp.int32, (n_blocks, T), 0)
    for hd in range(A_HEADS):
        g = jnp.where(jrow < s, gate_t[hd * n_blocks:(hd + 1) * n_blocks], NEG)
        sel = jnp.zeros((n_blocks, T), jnp.bool_)
        for _ in range(MOBA_TOPK):
            m = jnp.max(g, axis=0, keepdims=True)
            idx = jnp.min(jnp.where(g == m, jrow, n_blocks), axis=0, keepdims=True)
            pick = jrow == idx
            sel = sel | (pick & (m > NEG * 0.5))
            g = jnp.where(pick, -jnp.inf, g)
        mask_ref[0, hd] = jnp.where(sel, 0.0, NEG)

    k_mean = jnp.mean(zk, axis=0, keepdims=True)
    lane = lax.broadcasted_iota(jnp.int32, (1, A_WIDTH), 1)
    for hd in range(A_HEADS):
        in_head = (lane >= hd * HEAD_DIM) & (lane < (hd + 1) * HEAD_DIM)
        kmt_scr[pl.ds(hd * n_blocks + s, 1), :] = jnp.where(in_head, k_mean, 0.0)

    xb = proj(XB0, B_WIDTH)
    ext_scr[MAX_WINDOW:MAX_WINDOW + T, :] = xb
    lane_b = lax.broadcasted_iota(jnp.int32, (1, B_WIDTH), 1)
    win = jnp.zeros((1, B_WIDTH), F32)
    for gi, w in enumerate(POOL_WINDOWS):
        win = jnp.where(lane_b // B_GROUP == gi, float(w), win)
    run = jnp.zeros((T, B_WIDTH), F32)
    wsum = jnp.zeros((T, B_WIDTH), F32)
    for lag in range(MAX_WINDOW):
        run = run + ext_scr[MAX_WINDOW - lag:MAX_WINDOW - lag + T, :]
        if lag + 1 in POOL_WINDOWS:
            wsum = jnp.where(win == float(lag + 1), run, wsum)
    pos = (s * T + lax.broadcasted_iota(jnp.int32, (T, 1), 0) + 1).astype(F32)
    pooled = wsum / jnp.minimum(pos, win) - xb
    ext_scr[0:MAX_WINDOW, :] = xb[T - MAX_WINDOW:T, :]
    mixed_b = _dot(pooled.astype(BF16), poolw_ref[...]) * pscale_ref[...]
    ybc_ref[0, :, 0:B_WIDTH] = mixed_b * _silu(proj(GB0, B_WIDTH))

    vc = proj(VC0, C_WIDTH)
    mu = jnp.mean(vc, axis=-1, keepdims=True)
    cen = vc - mu
    var = jnp.mean(cen * cen, axis=-1, keepdims=True)
    vn = cen * lax.rsqrt(var + EPS)
    rows = lax.broadcasted_iota(jnp.int32, (C_HEADS * SGU_CHUNK, C_WIDTH), 0)
    cols = lax.broadcasted_iota(jnp.int32, (C_HEADS * SGU_CHUNK, C_WIDTH), 1)
    head_sel = (rows // SGU_CHUNK) == (cols // HEAD_DIM)
    ug = proj(UC0, C_WIDTH) * _silu(proj(GC0, C_WIDTH))
    for c in range(T // SGU_CHUNK):
        vn_c = vn[c * SGU_CHUNK:(c + 1) * SGU_CHUNK]
        stack = jnp.where(head_sel, jnp.concatenate([vn_c] * C_HEADS, axis=0), 0.0)
        mixed_c = _dot(sguw_ref[...], stack.astype(BF16)) + sgub_ref[...]
        ybc_ref[0, c * SGU_CHUNK:(c + 1) * SGU_CHUNK, B_WIDTH:B_WIDTH + C_WIDTH] = (
            ug[c * SGU_CHUNK:(c + 1) * SGU_CHUNK] * mixed_c)


def _proj_call(h, norm_g, w_in, pool_bd, pool_scale, sgu_wcat, sgu_btile):
    B, S, _ = h.shape
    T = ROW_TILE
    nb = S // MOBA_BLOCK
    full = lambda shape: pl.BlockSpec(shape, lambda b, s: (0,) * len(shape))
    return pl.pallas_call(
        functools.partial(_proj_kernel, n_blocks=nb),
        grid=(B, S // T),
        in_specs=[
            pl.BlockSpec((1, T, D_MODEL), lambda b, s: (b, s, 0)),
            full((1, D_MODEL)),
            full((D_MODEL, D_IN)),
            full((B_WIDTH, B_WIDTH)),
            full((1, B_WIDTH)),
            full((SGU_CHUNK, C_HEADS * SGU_CHUNK)),
            full((SGU_CHUNK, C_WIDTH)),
        ],
        out_specs=[
            pl.BlockSpec((1, T, A_WIDTH), lambda b, s: (b, s, 0)),
            pl.BlockSpec((1, 1, MOBA_BLOCK, A_WIDTH), lambda b, s: (b, s, 0, 0)),
            pl.BlockSpec((1, A_HEADS // 2, 1, HEAD_PAIR, MOBA_BLOCK), lambda b, s: (b, 0, s, 0, 0)),
            pl.BlockSpec((1, T, A_WIDTH), lambda b, s: (b, s, 0)),
            pl.BlockSpec((1, T, B_WIDTH + C_WIDTH), lambda b, s: (b, s, 0)),
            pl.BlockSpec((1, A_HEADS, nb, T), lambda b, s: (b, 0, 0, s)),
        ],
        out_shape=[
            jax.ShapeDtypeStruct((B, S, A_WIDTH), BF16),
            jax.ShapeDtypeStruct((B, nb, MOBA_BLOCK, A_WIDTH), BF16),
            jax.ShapeDtypeStruct((B, A_HEADS // 2, nb, HEAD_PAIR, MOBA_BLOCK), BF16),
            jax.ShapeDtypeStruct((B, S, A_WIDTH), F32),
            jax.ShapeDtypeStruct((B, S, B_WIDTH + C_WIDTH), F32),
            jax.ShapeDtypeStruct((B, A_HEADS, nb, S), F32),
        ],
        scratch_shapes=[
            pltpu.VMEM((A_HEADS * nb, A_WIDTH), F32),
            pltpu.VMEM((MAX_WINDOW + T, B_WIDTH), F32),
        ],
        compiler_params=pltpu.CompilerParams(
            dimension_semantics=("arbitrary", "arbitrary"), vmem_limit_bytes=VMEM_LIMIT),
        name="proj_mix",
    )(h, norm_g, w_in, pool_bd, pool_scale, sgu_wcat, sgu_btile)


def _attn_kernel(rb_ref, q_ref, k_ref, vt_ref, mask_ref, bias_ref, o_ref):
    hp = pl.program_id(1)
    qi = pl.program_id(2)
    q = q_ref[0]
    lane = lax.broadcasted_iota(jnp.int32, (1, HEAD_PAIR), 1)
    prev = jnp.maximum(qi - 1, 0)
    outs = []
    for hh in range(2):
        in_head = (lane >= hh * HEAD_DIM) & (lane < (hh + 1) * HEAD_DIM)
        qm = jnp.where(in_head, q, jnp.zeros_like(q))
        far_bias = rb_ref[REL_BUCKETS - 1, 2 * hp + hh]
        rows = slice(hh * HEAD_DIM, (hh + 1) * HEAD_DIM)

        def scores(j):
            return _dot_nt(k_ref[0, j], qm)

        def pv(j, p):
            return _dot(vt_ref[0, 0, j, rows, :], p.astype(BF16))

        st = scores(qi) + bias_ref[hh, 0]
        m = jnp.max(st, axis=0, keepdims=True)
        p = jnp.exp(st - m)
        l = jnp.sum(p, axis=0, keepdims=True)
        acc = pv(qi, p)

        def update(carry, j, st):
            m, l, acc = carry
            m_new = jnp.maximum(m, jnp.max(st, axis=0, keepdims=True))
            alpha = jnp.exp(m - m_new)
            p = jnp.exp(st - m_new)
            return (m_new, alpha * l + jnp.sum(p, axis=0, keepdims=True),
                    alpha * acc + pv(j, p))

        prev_row = jnp.where(qi >= 1, mask_ref[0, hh, pl.ds(prev, 1), :], NEG)
        carry = update((m, l, acc), prev, scores(prev) + bias_ref[hh, 1] + prev_row)

        def far_step(j, carry):
            row = mask_ref[0, hh, pl.ds(j, 1), :] + far_bias
            return update(carry, j, scores(j) + row)

        m, l, acc = lax.fori_loop(0, prev, far_step, carry)
        outs.append(acc / l)
    o_ref[0] = jnp.concatenate(outs, axis=0).T


def _attn_call(rel_bias, q, k_blk, vt_blk, mask, bias_tiles):
    B, S, _ = q.shape
    nb = S // MOBA_BLOCK
    Tq = MOBA_BLOCK
    return pl.pallas_call(
        _attn_kernel,
        grid=(B, A_HEADS // 2, S // Tq),
        in_specs=[
            pl.BlockSpec(memory_space=pltpu.SMEM),
            pl.BlockSpec((1, Tq, HEAD_PAIR), lambda b, hp, qi: (b, qi, hp)),
            pl.BlockSpec((1, nb, MOBA_BLOCK, HEAD_PAIR), lambda b, hp, qi: (b, 0, 0, hp)),
            pl.BlockSpec((1, 1, nb, HEAD_PAIR, MOBA_BLOCK), lambda b, hp, qi: (b, hp, 0, 0, 0)),
            pl.BlockSpec((1, 2, nb, Tq), lambda b, hp, qi: (b, hp, 0, qi)),
            pl.BlockSpec((2, 2, MOBA_BLOCK, MOBA_BLOCK), lambda b, hp, qi: (hp, 0, 0, 0)),
        ],
        out_specs=pl.BlockSpec((1, Tq, HEAD_PAIR), lambda b, hp, qi: (b, qi, hp)),
        out_shape=jax.ShapeDtypeStruct((B, S, A_WIDTH), F32),
        compiler_params=pltpu.CompilerParams(
            dimension_semantics=("arbitrary", "arbitrary", "arbitrary"),
            vmem_limit_bytes=VMEM_LIMIT),
        name="moba_attn",
    )(rel_bias, q, k_blk, vt_blk, mask, bias_tiles)


def _out_kernel(h_ref, ya_ref, gsa_ref, ybc_ref, p_ref, wout_ref, plew_ref, gatew_ref,
                fg_ref, o_ref, *, final):
    ya = (ya_ref[...] * gsa_ref[...]).astype(BF16)
    y = _dot(ya, wout_ref[0:A_WIDTH, :]) + _dot(ybc_ref[...].astype(BF16), wout_ref[A_WIDTH:, :])
    h1 = h_ref[...] + y
    emb = _dot(p_ref[...].astype(BF16), plew_ref[...])
    gate = 1.0 / (1.0 + jnp.exp(-_dot(h1.astype(BF16), gatew_ref[...])))
    h2 = h1 + emb * gate
    if final:
        h2 = h2 * lax.rsqrt(jnp.mean(h2 * h2, axis=-1, keepdims=True) + EPS) * fg_ref[...]
    o_ref[...] = h2


def _out_call(h, ya, gsa, ybc, p, w_out, ple_w, gate_w, final_g, final):
    N = h.shape[0]
    T = OUT_TILE
    row = lambda width: pl.BlockSpec((T, width), lambda i: (i, 0))
    full = lambda shape: pl.BlockSpec(shape, lambda i: (0,) * len(shape))
    return pl.pallas_call(
        functools.partial(_out_kernel, final=final),
        grid=(N // T,),
        in_specs=[
            row(D_MODEL), row(A_WIDTH), row(A_WIDTH), row(B_WIDTH + C_WIDTH), row(D_PLE),
            full((D_MODEL, D_MODEL)), full((D_PLE, D_MODEL)), full((D_MODEL, D_MODEL)),
            full((1, D_MODEL)),
        ],
        out_specs=row(D_MODEL),
        out_shape=jax.ShapeDtypeStruct((N, D_MODEL), F32),
        compiler_params=pltpu.CompilerParams(
            dimension_semantics=("arbitrary",), vmem_limit_bytes=VMEM_LIMIT),
        name="out_ple",
    )(h, ya, gsa, ybc, p, w_out, ple_w, gate_w, final_g)


def _block_diag(w):
    G, c, d = w.shape
    eye = jnp.eye(G, dtype=w.dtype)
    return (eye[:, None, :, None] * w[:, :, None, :]).reshape(G * c, G * d)


def kernel(x, p, norm_g, w_in, w_out, rel_bias, pool_w, pool_scale, sgu_w, sgu_b,
           ple_w, ple_gate_w, final_g):
    B, S, D = x.shape
    depth = w_in.shape[0]
    N = B * S
    bias_tiles = _bias_tiles(rel_bias)
    h = x
    for i in range(depth):
        sgu_wcat = jnp.transpose(jnp.tril(sgu_w[i]), (1, 0, 2)).reshape(
            SGU_CHUNK, C_HEADS * SGU_CHUNK).astype(BF16)
        sgu_btile = jnp.repeat(jnp.transpose(sgu_b[i]), HEAD_DIM, axis=1)
        q, k_blk, vt_blk, gsa, ybc, mask = _proj_call(
            h, norm_g[i][None, :], w_in[i].astype(BF16), _block_diag(pool_w[i]).astype(BF16),
            pool_scale[i][None, :], sgu_wcat, sgu_btile)
        ya = _attn_call(rel_bias, q, k_blk, vt_blk, mask, bias_tiles)
        h = _out_call(
            h.reshape(N, D), ya.reshape(N, A_WIDTH), gsa.reshape(N, A_WIDTH),
            ybc.reshape(N, B_WIDTH + C_WIDTH), p[i].reshape(N, D_PLE),
            w_out[i].astype(BF16), ple_w[i].astype(BF16), ple_gate_w[i].astype(BF16),
            final_g[None, :], final=(i == depth - 1)).reshape(B, S, D)
    return h
```

```python
import functools
import math

import numpy as np
import jax
import jax.numpy as jnp
from jax import lax
from jax.experimental import pallas as pl
from jax.experimental.pallas import tpu as pltpu

D_MODEL = 1024
HEAD_DIM = 64
A_WIDTH = 512
A_HEADS = 8
MOBA_BLOCK = 256
MOBA_TOPK = 3
REL_BUCKETS = 32
REL_MAX_DIST = 128
B_WIDTH = 256
POOL_WINDOWS = (2, 4, 8, 16)
B_GROUP = 64
C_WIDTH = 256
C_HEADS = 4
SGU_CHUNK = 128
D_PLE = 256
D_IN = 3328
EPS = 1e-6
NEG = -1e30
LOG2E = math.log2(math.e)

MAX_WINDOW = max(POOL_WINDOWS)
ROW_TILE = MOBA_BLOCK
OUT_TILE = 512
LANES = 128
MASK_HI = HEAD_DIM
MASK_LO = HEAD_DIM + 32
V_ROWS = HEAD_DIM + 16
ATTN_HEADS = 2
FAR_GROUP = 2
VMEM_LIMIT = 48 * 1024 * 1024

_OFF = np.cumsum((0,) + (A_WIDTH,) * 4 + (B_WIDTH,) * 2 + (C_WIDTH,) * 3)
Q0, K0, V0, GA0, XB0, GB0, UC0, VC0, GC0, _ = (int(o) for o in _OFF)

BF16 = jnp.bfloat16
F32 = jnp.float32


def _bucket_thresholds():
    max_exact = REL_BUCKETS // 2
    n = np.arange(0, 4 * MOBA_BLOCK)
    nf = np.maximum(n, 1).astype(np.float64)
    large = max_exact + (np.log(nf / max_exact) / math.log(REL_MAX_DIST / max_exact)
                         * (REL_BUCKETS - max_exact)).astype(np.int64)
    large = np.minimum(large, REL_BUCKETS - 1)
    bucket = np.where(n < max_exact, n, large)
    return [int(np.argmax(bucket >= b)) for b in range(1, REL_BUCKETS)]


_THRESHOLDS = _bucket_thresholds()


def _silu(x):
    return x * (1.0 / (1.0 + jnp.exp(-x)))


def _split_bf16(x):
    hi = x.astype(BF16)
    lo = (x - hi.astype(F32)).astype(BF16)
    return hi, lo


def _dot(a, b):
    return jnp.dot(a, b, preferred_element_type=F32)


def _dot_nt(a, b):
    return lax.dot_general(a, b, (((1,), (1,)), ((), ())), preferred_element_type=F32)


def _bias_kernel(rb_ref, out_ref):
    h = pl.program_id(0)
    key = lax.broadcasted_iota(jnp.int32, (MOBA_BLOCK, MOBA_BLOCK), 0)
    qry = lax.broadcasted_iota(jnp.int32, (MOBA_BLOCK, MOBA_BLOCK), 1)
    for kind, shift in ((0, 0), (1, MOBA_BLOCK)):
        dist = qry - key + shift
        val = jnp.full((MOBA_BLOCK, MOBA_BLOCK), rb_ref[0, h], F32)
        for b in range(1, REL_BUCKETS):
            val = jnp.where(dist >= _THRESHOLDS[b - 1], rb_ref[b, h], val)
        val = val * LOG2E
        if kind == 0:
            val = jnp.where(dist >= 0, val, NEG)
        out_ref[0, kind] = val


def _bias_tiles(rel_bias):
    return pl.pallas_call(
        _bias_kernel,
        grid=(A_HEADS,),
        in_specs=[pl.BlockSpec(memory_space=pltpu.SMEM)],
        out_specs=pl.BlockSpec((1, 2, MOBA_BLOCK, MOBA_BLOCK), lambda h: (h, 0, 0, 0)),
        out_shape=jax.ShapeDtypeStruct((A_HEADS, 2, MOBA_BLOCK, MOBA_BLOCK), F32),
        name="bias_tiles",
    )(rel_bias)


def _proj_kernel(rb_ref, h_ref, ng_ref, wqt_ref, wvt_ref, win_ref, poolw_ref, pscale_ref,
                 sguw_ref, sgub_ref,
                 q_ref, k_ref, vt_ref, gsa_ref, ybc_ref,
                 kmt_scr, ext_scr, *, n_blocks):
    T = ROW_TILE
    s = pl.program_id(1)

    @pl.when(s == 0)
    def _():
        kmt_scr[...] = jnp.zeros_like(kmt_scr)
        ext_scr[0:MAX_WINDOW, :] = jnp.zeros((MAX_WINDOW, B_WIDTH), F32)

    h = h_ref[0]
    hn = h * lax.rsqrt(jnp.mean(h * h, axis=-1, keepdims=True) + EPS) * ng_ref[...]
    hb = hn.astype(BF16)

    def proj(c0, width):
        return _dot(hb, win_ref[:, c0:c0 + width])

    zqt = _dot_nt(wqt_ref[...], hb) * (HEAD_DIM ** -0.5 * LOG2E)

    km_hi, km_lo = _split_bf16(kmt_scr[...])
    q_hi, q_lo = _split_bf16(zqt)
    gate_t = _dot(km_hi, q_hi) + _dot(km_hi, q_lo) + _dot(km_lo, q_hi)
    jrow = lax.broadcasted_iota(jnp.int32, (n_blocks, T), 0)
    for hd in range(A_HEADS):
        g = jnp.where(jrow < s, gate_t[hd * n_blocks:(hd + 1) * n_blocks], NEG)
        sel = jnp.zeros((n_blocks, T), jnp.bool_)
        for _ in range(MOBA_TOPK):
            m = jnp.max(g, axis=0, keepdims=True)
            idx = jnp.min(jnp.where(g == m, jrow, n_blocks), axis=0, keepdims=True)
            pick = jrow == idx
            sel = sel | (pick & (m > NEG * 0.5))
            g = jnp.where(pick, -jnp.inf, g)
        far_bias = rb_ref[REL_BUCKETS - 1, hd] * LOG2E
        term = jnp.where(sel, jnp.where(jrow <= s - 2, far_bias, 0.0), NEG)
        term = jnp.where(jrow == s, 0.0, term)
        term_hi = term.astype(BF16).astype(F32)
        q_aug_t = jnp.concatenate(
            [zqt[hd * HEAD_DIM:(hd + 1) * HEAD_DIM], term_hi, term - term_hi], axis=0)
        q_ref[0, hd] = q_aug_t.T.astype(BF16)

    zk = proj(K0, A_WIDTH)
    lane = lax.broadcasted_iota(jnp.int32, (1, LANES), 1)
    indicator = jnp.where((lane == MASK_HI + s) | (lane == MASK_LO + s), 1.0, 0.0)
    for pair in range(A_HEADS // 2):
        zk_pair = zk[:, pair * LANES:(pair + 1) * LANES]
        k_ref[0, 2 * pair, 0] = jnp.where(lane < HEAD_DIM, zk_pair, indicator).astype(BF16)
        k_ref[0, 2 * pair + 1, 0] = jnp.where(
            lane < HEAD_DIM, pltpu.roll(zk_pair, HEAD_DIM, 1), indicator).astype(BF16)
    k_mean = jnp.mean(zk, axis=0, keepdims=True)
    lane_a = lax.broadcasted_iota(jnp.int32, (1, A_WIDTH), 1)
    for hd in range(A_HEADS):
        in_head = (lane_a >= hd * HEAD_DIM) & (lane_a < (hd + 1) * HEAD_DIM)
        kmt_scr[pl.ds(hd * n_blocks + s, 1), :] = jnp.where(in_head, k_mean, 0.0)

    zvt = _dot_nt(wvt_ref[...], hb)
    tail_row = lax.broadcasted_iota(jnp.int32, (V_ROWS - HEAD_DIM, T), 0)
    tail = jnp.where(tail_row == 0, 1.0, 0.0)
    for hd in range(A_HEADS):
        vt_ref[0, hd, 0] = jnp.concatenate(
            [zvt[hd * HEAD_DIM:(hd + 1) * HEAD_DIM], tail], axis=0).astype(BF16)

    gsa_ref[0] = _silu(proj(GA0, A_WIDTH))

    xb = proj(XB0, B_WIDTH)
    ext_scr[MAX_WINDOW:MAX_WINDOW + T, :] = xb
    lane_b = lax.broadcasted_iota(jnp.int32, (1, B_WIDTH), 1)
    win = jnp.zeros((1, B_WIDTH), F32)
    for gi, w in enumerate(POOL_WINDOWS):
        win = jnp.where(lane_b // B_GROUP == gi, float(w), win)
    run = jnp.zeros((T, B_WIDTH), F32)
    wsum = jnp.zeros((T, B_WIDTH), F32)
    for lag in range(MAX_WINDOW):
        run = run + ext_scr[MAX_WINDOW - lag:MAX_WINDOW - lag + T, :]
        if lag + 1 in POOL_WINDOWS:
            wsum = jnp.where(win == float(lag + 1), run, wsum)
    pos = (s * T + lax.broadcasted_iota(jnp.int32, (T, 1), 0) + 1).astype(F32)
    pooled = wsum / jnp.minimum(pos, win) - xb
    ext_scr[0:MAX_WINDOW, :] = xb[T - MAX_WINDOW:T, :]
    mixed_b = _dot(pooled.astype(BF16), poolw_ref[...]) * pscale_ref[...]
    ybc_ref[0, :, 0:B_WIDTH] = mixed_b * _silu(proj(GB0, B_WIDTH))

    vc = proj(VC0, C_WIDTH)
    mu = jnp.mean(vc, axis=-1, keepdims=True)
    cen = vc - mu
    var = jnp.mean(cen * cen, axis=-1, keepdims=True)
    vn = cen * lax.rsqrt(var + EPS)
    rows = lax.broadcasted_iota(jnp.int32, (C_HEADS * SGU_CHUNK, C_WIDTH), 0)
    cols = lax.broadcasted_iota(jnp.int32, (C_HEADS * SGU_CHUNK, C_WIDTH), 1)
    head_sel = (rows // SGU_CHUNK) == (cols // HEAD_DIM)
    ug = proj(UC0, C_WIDTH) * _silu(proj(GC0, C_WIDTH))
    for c in range(T // SGU_CHUNK):
        vn_c = vn[c * SGU_CHUNK:(c + 1) * SGU_CHUNK]
        stack = jnp.where(head_sel, jnp.concatenate([vn_c] * C_HEADS, axis=0), 0.0)
        mixed_c = _dot(sguw_ref[...], stack.astype(BF16)) + sgub_ref[...]
        ybc_ref[0, c * SGU_CHUNK:(c + 1) * SGU_CHUNK, B_WIDTH:B_WIDTH + C_WIDTH] = (
            ug[c * SGU_CHUNK:(c + 1) * SGU_CHUNK] * mixed_c)


def _proj_call(rel_bias, h, norm_g, w_qt, w_vt, w_in, pool_bd, pool_scale, sgu_wcat, sgu_btile):
    B, S, _ = h.shape
    T = ROW_TILE
    nb = S // MOBA_BLOCK
    assert nb <= MASK_LO - MASK_HI and MASK_LO + nb <= LANES
    full = lambda shape: pl.BlockSpec(shape, lambda b, s: (0,) * len(shape))
    return pl.pallas_call(
        functools.partial(_proj_kernel, n_blocks=nb),
        grid=(B, S // T),
        in_specs=[
            pl.BlockSpec(memory_space=pltpu.SMEM),
            pl.BlockSpec((1, T, D_MODEL), lambda b, s: (b, s, 0)),
            full((1, D_MODEL)),
            full((A_WIDTH, D_MODEL)),
            full((A_WIDTH, D_MODEL)),
            full((D_MODEL, D_IN)),
            full((B_WIDTH, B_WIDTH)),
            full((1, B_WIDTH)),
            full((SGU_CHUNK, C_HEADS * SGU_CHUNK)),
            full((SGU_CHUNK, C_WIDTH)),
        ],
        out_specs=[
            pl.BlockSpec((1, A_HEADS, T, LANES), lambda b, s: (b, 0, s, 0)),
            pl.BlockSpec((1, A_HEADS, 1, MOBA_BLOCK, LANES), lambda b, s: (b, 0, s, 0, 0)),
            pl.BlockSpec((1, A_HEADS, 1, V_ROWS, MOBA_BLOCK), lambda b, s: (b, 0, s, 0, 0)),
            pl.BlockSpec((1, T, A_WIDTH), lambda b, s: (b, s, 0)),
            pl.BlockSpec((1, T, B_WIDTH + C_WIDTH), lambda b, s: (b, s, 0)),
        ],
        out_shape=[
            jax.ShapeDtypeStruct((B, A_HEADS, S, LANES), BF16),
            jax.ShapeDtypeStruct((B, A_HEADS, nb, MOBA_BLOCK, LANES), BF16),
            jax.ShapeDtypeStruct((B, A_HEADS, nb, V_ROWS, MOBA_BLOCK), BF16),
            jax.ShapeDtypeStruct((B, S, A_WIDTH), F32),
            jax.ShapeDtypeStruct((B, S, B_WIDTH + C_WIDTH), F32),
        ],
        scratch_shapes=[
            pltpu.VMEM((A_HEADS * nb, A_WIDTH), F32),
            pltpu.VMEM((MAX_WINDOW + T, B_WIDTH), F32),
        ],
        compiler_params=pltpu.CompilerParams(
            dimension_semantics=("arbitrary", "arbitrary"), vmem_limit_bytes=VMEM_LIMIT),
        name="proj_mix",
    )(rel_bias, h, norm_g, w_qt, w_vt, w_in, pool_bd, pool_scale, sgu_wcat, sgu_btile)


def _attn_kernel(q_ref, k_ref, vt_ref, bias_ref, o_ref):
    qi = pl.program_id(2)
    prev = jnp.maximum(qi - 1, 0)
    n_far = prev
    left = n_far % FAR_GROUP
    pen_prev = jnp.where(qi >= 1, 0.0, NEG)
    pen_left = [jnp.where(left > i, 0.0, NEG) for i in range(FAR_GROUP - 1)]
    qs = [q_ref[0, hd] for hd in range(ATTN_HEADS)]

    def group(hd, m, acc, blocks, extras):
        sts = []
        for j, extra in zip(blocks, extras):
            st = _dot_nt(k_ref[0, hd, j], qs[hd])
            sts.append(st if extra is None else st + extra)
        m_new = functools.reduce(jnp.maximum, [jnp.max(st, axis=0, keepdims=True) for st in sts])
        if m is not None:
            m_new = jnp.maximum(m, m_new)
        pv = None
        for j, st in zip(blocks, sts):
            p = jnp.exp2(st - m_new).astype(BF16)
            d = _dot(vt_ref[0, hd, j], p)
            pv = d if pv is None else pv + d
        if m is None:
            return m_new, pv
        return m_new, jnp.exp2(m - m_new) * acc + pv

    carry = []
    for hd in range(ATTN_HEADS):
        blocks = [qi, prev] + list(range(FAR_GROUP - 1))
        extras = [bias_ref[hd, 0], bias_ref[hd, 1] + pen_prev] + pen_left
        carry.extend(group(hd, None, None, blocks, extras))

    def far_step(i, carry):
        j0 = left + i * FAR_GROUP
        out = []
        for hd in range(ATTN_HEADS):
            out.extend(group(hd, carry[2 * hd], carry[2 * hd + 1],
                             [j0 + g for g in range(FAR_GROUP)], [None] * FAR_GROUP))
        return tuple(out)

    carry = lax.fori_loop(0, n_far // FAR_GROUP, far_step, tuple(carry))
    outs = []
    for hd in range(ATTN_HEADS):
        acc = carry[2 * hd + 1]
        outs.append(acc[0:HEAD_DIM] / acc[HEAD_DIM:HEAD_DIM + 1])
    o_ref[0] = jnp.concatenate(outs, axis=0).T


def _attn_call(q, k_blk, vt_blk, bias_tiles):
    B, H, S, _ = q.shape
    nb = S // MOBA_BLOCK
    Tq = MOBA_BLOCK
    G = ATTN_HEADS
    return pl.pallas_call(
        _attn_kernel,
        grid=(B, H // G, S // Tq),
        in_specs=[
            pl.BlockSpec((1, G, Tq, LANES), lambda b, hg, qi: (b, hg, qi, 0)),
            pl.BlockSpec((1, G, nb, MOBA_BLOCK, LANES), lambda b, hg, qi: (b, hg, 0, 0, 0)),
            pl.BlockSpec((1, G, nb, V_ROWS, MOBA_BLOCK), lambda b, hg, qi: (b, hg, 0, 0, 0)),
            pl.BlockSpec((G, 2, MOBA_BLOCK, MOBA_BLOCK), lambda b, hg, qi: (hg, 0, 0, 0)),
        ],
        out_specs=pl.BlockSpec((1, Tq, G * HEAD_DIM), lambda b, hg, qi: (b, qi, hg)),
        out_shape=jax.ShapeDtypeStruct((B, S, H * HEAD_DIM), F32),
        compiler_params=pltpu.CompilerParams(
            dimension_semantics=("arbitrary", "arbitrary", "arbitrary"),
            vmem_limit_bytes=VMEM_LIMIT),
        name="moba_attn",
    )(q, k_blk, vt_blk, bias_tiles)


def _out_kernel(h_ref, ya_ref, gsa_ref, ybc_ref, p_ref, wout_ref, plew_ref, gatew_ref,
                fg_ref, o_ref, *, final):
    ya = (ya_ref[...] * gsa_ref[...]).astype(BF16)
    y = _dot(ya, wout_ref[0:A_WIDTH, :]) + _dot(ybc_ref[...].astype(BF16), wout_ref[A_WIDTH:, :])
    h1 = h_ref[...] + y
    emb = _dot(p_ref[...].astype(BF16), plew_ref[...])
    gate = 1.0 / (1.0 + jnp.exp(-_dot(h1.astype(BF16), gatew_ref[...])))
    h2 = h1 + emb * gate
    if final:
        h2 = h2 * lax.rsqrt(jnp.mean(h2 * h2, axis=-1, keepdims=True) + EPS) * fg_ref[...]
    o_ref[...] = h2


def _out_call(h, ya, gsa, ybc, p, w_out, ple_w, gate_w, final_g, final):
    N = h.shape[0]
    T = OUT_TILE
    row = lambda width: pl.BlockSpec((T, width), lambda i: (i, 0))
    full = lambda shape: pl.BlockSpec(shape, lambda i: (0,) * len(shape))
    return pl.pallas_call(
        functools.partial(_out_kernel, final=final),
        grid=(N // T,),
        in_specs=[
            row(D_MODEL), row(A_WIDTH), row(A_WIDTH), row(B_WIDTH + C_WIDTH), row(D_PLE),
            full((D_MODEL, D_MODEL)), full((D_PLE, D_MODEL)), full((D_MODEL, D_MODEL)),
            full((1, D_MODEL)),
        ],
        out_specs=row(D_MODEL),
        out_shape=jax.ShapeDtypeStruct((N, D_MODEL), F32),
        compiler_params=pltpu.CompilerParams(
            dimension_semantics=("arbitrary",), vmem_limit_bytes=VMEM_LIMIT),
        name="out_ple",
    )(h, ya, gsa, ybc, p, w_out, ple_w, gate_w, final_g)


def _block_diag(w):
    G, c, d = w.shape
    eye = jnp.eye(G, dtype=w.dtype)
    return (eye[:, None, :, None] * w[:, :, None, :]).reshape(G * c, G * d)


def kernel(x, p, norm_g, w_in, w_out, rel_bias, pool_w, pool_scale, sgu_w, sgu_b,
           ple_w, ple_gate_w, final_g):
    B, S, D = x.shape
    depth = w_in.shape[0]
    N = B * S
    bias_tiles = _bias_tiles(rel_bias)
    h = x
    for i in range(depth):
        w_in_b = w_in[i].astype(BF16)
        sgu_wcat = jnp.transpose(jnp.tril(sgu_w[i]), (1, 0, 2)).reshape(
            SGU_CHUNK, C_HEADS * SGU_CHUNK).astype(BF16)
        sgu_btile = jnp.repeat(jnp.transpose(sgu_b[i]), HEAD_DIM, axis=1)
        q, k_blk, vt_blk, gsa, ybc = _proj_call(
            rel_bias, h, norm_g[i][None, :],
            jnp.transpose(w_in_b[:, Q0:Q0 + A_WIDTH]), jnp.transpose(w_in_b[:, V0:V0 + A_WIDTH]),
            w_in_b, _block_diag(pool_w[i]).astype(BF16), pool_scale[i][None, :],
            sgu_wcat, sgu_btile)
        ya = _attn_call(q, k_blk, vt_blk, bias_tiles)
        h = _out_call(
            h.reshape(N, D), ya.reshape(N, A_WIDTH), gsa.reshape(N, A_WIDTH),
            ybc.reshape(N, B_WIDTH + C_WIDTH), p[i].reshape(N, D_PLE),
            w_out[i].astype(BF16), ple_w[i].astype(BF16), ple_gate_w[i].astype(BF16),
            final_g[None, :], final=(i == depth - 1)).reshape(B, S, D)
    return h
```

```python
import functools
import math

import numpy as np
import jax
import jax.numpy as jnp
from jax import lax
from jax.experimental import pallas as pl
from jax.experimental.pallas import tpu as pltpu

D_MODEL = 1024
HEAD_DIM = 64
A_WIDTH = 512
A_HEADS = 8
MOBA_BLOCK = 256
MOBA_TOPK = 3
REL_BUCKETS = 32
REL_MAX_DIST = 128
B_WIDTH = 256
POOL_WINDOWS = (2, 4, 8, 16)
B_GROUP = 64
C_WIDTH = 256
C_HEADS = 4
SGU_CHUNK = 128
D_PLE = 256
D_IN = 3328
EPS = 1e-6
NEG = -1e30
LOG2E = math.log2(math.e)

MAX_WINDOW = max(POOL_WINDOWS)
ROW_TILE = MOBA_BLOCK
OUT_TILE = 512
LANES = 128
MASK_HI = HEAD_DIM
MASK_LO = HEAD_DIM + 32
V_ROWS = HEAD_DIM + 16
ATTN_HEADS = 4
VMEM_LIMIT = 48 * 1024 * 1024

_OFF = np.cumsum((0,) + (A_WIDTH,) * 4 + (B_WIDTH,) * 2 + (C_WIDTH,) * 3)
Q0, K0, V0, GA0, XB0, GB0, UC0, VC0, GC0, _ = (int(o) for o in _OFF)

BF16 = jnp.bfloat16
F32 = jnp.float32


def _bucket_thresholds():
    max_exact = REL_BUCKETS // 2
    n = np.arange(0, 4 * MOBA_BLOCK)
    nf = np.maximum(n, 1).astype(np.float64)
    large = max_exact + (np.log(nf / max_exact) / math.log(REL_MAX_DIST / max_exact)
                         * (REL_BUCKETS - max_exact)).astype(np.int64)
    large = np.minimum(large, REL_BUCKETS - 1)
    bucket = np.where(n < max_exact, n, large)
    return [int(np.argmax(bucket >= b)) for b in range(1, REL_BUCKETS)]


_THRESHOLDS = _bucket_thresholds()


def _silu(x):
    return x * (1.0 / (1.0 + jnp.exp(-x)))


def _split_bf16(x):
    hi = x.astype(BF16)
    lo = (x - hi.astype(F32)).astype(BF16)
    return hi, lo


def _dot(a, b):
    return jnp.dot(a, b, preferred_element_type=F32)


def _dot_nt(a, b):
    return lax.dot_general(a, b, (((1,), (1,)), ((), ())), preferred_element_type=F32)


def _bias_kernel(rb_ref, out_ref):
    h = pl.program_id(0)
    key = lax.broadcasted_iota(jnp.int32, (MOBA_BLOCK, MOBA_BLOCK), 0)
    qry = lax.broadcasted_iota(jnp.int32, (MOBA_BLOCK, MOBA_BLOCK), 1)
    for kind, shift in ((0, 0), (1, MOBA_BLOCK)):
        dist = qry - key + shift
        val = jnp.full((MOBA_BLOCK, MOBA_BLOCK), rb_ref[0, h], F32)
        for b in range(1, REL_BUCKETS):
            val = jnp.where(dist >= _THRESHOLDS[b - 1], rb_ref[b, h], val)
        val = val * LOG2E
        if kind == 0:
            val = jnp.where(dist >= 0, val, NEG)
        out_ref[0, kind] = val


def _bias_tiles(rel_bias):
    return pl.pallas_call(
        _bias_kernel,
        grid=(A_HEADS,),
        in_specs=[pl.BlockSpec(memory_space=pltpu.SMEM)],
        out_specs=pl.BlockSpec((1, 2, MOBA_BLOCK, MOBA_BLOCK), lambda h: (h, 0, 0, 0)),
        out_shape=jax.ShapeDtypeStruct((A_HEADS, 2, MOBA_BLOCK, MOBA_BLOCK), F32),
        name="bias_tiles",
    )(rel_bias)


def _proj_kernel(rb_ref, h_ref, ng_ref, wqt_ref, wvt_ref, win_ref, poolw_ref, pscale_ref,
                 sguw_ref, sgub_ref,
                 q_ref, k_ref, vt_ref, gsa_ref, ybc_ref,
                 kmt_scr, ext_scr, *, n_blocks):
    T = ROW_TILE
    s = pl.program_id(1)

    @pl.when(s == 0)
    def _():
        kmt_scr[...] = jnp.zeros_like(kmt_scr)
        ext_scr[0:MAX_WINDOW, :] = jnp.zeros((MAX_WINDOW, B_WIDTH), F32)

    h = h_ref[0]
    hn = h * lax.rsqrt(jnp.mean(h * h, axis=-1, keepdims=True) + EPS) * ng_ref[...]
    hb = hn.astype(BF16)

    def proj(c0, width):
        return _dot(hb, win_ref[:, c0:c0 + width])

    zqt = _dot_nt(wqt_ref[...], hb) * (HEAD_DIM ** -0.5 * LOG2E)

    km_hi, km_lo = _split_bf16(kmt_scr[...])
    q_hi, q_lo = _split_bf16(zqt)
    gate_t = _dot(km_hi, q_hi) + _dot(km_hi, q_lo) + _dot(km_lo, q_hi)
    jrow = lax.broadcasted_iota(jnp.int32, (n_blocks, T), 0)
    for hd in range(A_HEADS):
        g = jnp.where(jrow < s, gate_t[hd * n_blocks:(hd + 1) * n_blocks], NEG)
        sel = jnp.zeros((n_blocks, T), jnp.bool_)
        for _ in range(MOBA_TOPK):
            m = jnp.max(g, axis=0, keepdims=True)
            idx = jnp.min(jnp.where(g == m, jrow, n_blocks), axis=0, keepdims=True)
            pick = jrow == idx
            sel = sel | (pick & (m > NEG * 0.5))
            g = jnp.where(pick, -jnp.inf, g)
        far_bias = rb_ref[REL_BUCKETS - 1, hd] * LOG2E
        term = jnp.where(sel, jnp.where(jrow <= s - 2, far_bias, 0.0), NEG)
        term = jnp.where(jrow == s, 0.0, term)
        term_hi = term.astype(BF16).astype(F32)
        q_aug_t = jnp.concatenate(
            [zqt[hd * HEAD_DIM:(hd + 1) * HEAD_DIM], term_hi, term - term_hi], axis=0)
        q_ref[0, hd] = q_aug_t.T.astype(BF16)

    zk = proj(K0, A_WIDTH)
    lane = lax.broadcasted_iota(jnp.int32, (1, LANES), 1)
    indicator = jnp.where((lane == MASK_HI + s) | (lane == MASK_LO + s), 1.0, 0.0)
    for pair in range(A_HEADS // 2):
        zk_pair = zk[:, pair * LANES:(pair + 1) * LANES]
        k_ref[0, 2 * pair, 0] = jnp.where(lane < HEAD_DIM, zk_pair, indicator).astype(BF16)
        k_ref[0, 2 * pair + 1, 0] = jnp.where(
            lane < HEAD_DIM, pltpu.roll(zk_pair, HEAD_DIM, 1), indicator).astype(BF16)
    k_mean = jnp.mean(zk, axis=0, keepdims=True)
    lane_a = lax.broadcasted_iota(jnp.int32, (1, A_WIDTH), 1)
    for hd in range(A_HEADS):
        in_head = (lane_a >= hd * HEAD_DIM) & (lane_a < (hd + 1) * HEAD_DIM)
        kmt_scr[pl.ds(hd * n_blocks + s, 1), :] = jnp.where(in_head, k_mean, 0.0)

    zvt = _dot_nt(wvt_ref[...], hb)
    tail_row = lax.broadcasted_iota(jnp.int32, (V_ROWS - HEAD_DIM, T), 0)
    tail = jnp.where(tail_row == 0, 1.0, 0.0)
    for hd in range(A_HEADS):
        vt_ref[0, hd, 0] = jnp.concatenate(
            [zvt[hd * HEAD_DIM:(hd + 1) * HEAD_DIM], tail], axis=0).astype(BF16)

    gsa_ref[0] = _silu(proj(GA0, A_WIDTH))

    xb = proj(XB0, B_WIDTH)
    ext_scr[MAX_WINDOW:MAX_WINDOW + T, :] = xb
    lane_b = lax.broadcasted_iota(jnp.int32, (1, B_WIDTH), 1)
    win = jnp.zeros((1, B_WIDTH), F32)
    for gi, w in enumerate(POOL_WINDOWS):
        win = jnp.where(lane_b // B_GROUP == gi, float(w), win)
    run = jnp.zeros((T, B_WIDTH), F32)
    wsum = jnp.zeros((T, B_WIDTH), F32)
    for lag in range(MAX_WINDOW):
        run = run + ext_scr[MAX_WINDOW - lag:MAX_WINDOW - lag + T, :]
        if lag + 1 in POOL_WINDOWS:
            wsum = jnp.where(win == float(lag + 1), run, wsum)
    pos = (s * T + lax.broadcasted_iota(jnp.int32, (T, 1), 0) + 1).astype(F32)
    pooled = wsum / jnp.minimum(pos, win) - xb
    ext_scr[0:MAX_WINDOW, :] = xb[T - MAX_WINDOW:T, :]
    mixed_b = _dot(pooled.astype(BF16), poolw_ref[...]) * pscale_ref[...]
    ybc_ref[0, :, 0:B_WIDTH] = mixed_b * _silu(proj(GB0, B_WIDTH))

    vc = proj(VC0, C_WIDTH)
    mu = jnp.mean(vc, axis=-1, keepdims=True)
    cen = vc - mu
    var = jnp.mean(cen * cen, axis=-1, keepdims=True)
    vn = cen * lax.rsqrt(var + EPS)
    rows = lax.broadcasted_iota(jnp.int32, (C_HEADS * SGU_CHUNK, C_WIDTH), 0)
    cols = lax.broadcasted_iota(jnp.int32, (C_HEADS * SGU_CHUNK, C_WIDTH), 1)
    head_sel = (rows // SGU_CHUNK) == (cols // HEAD_DIM)
    ug = proj(UC0, C_WIDTH) * _silu(proj(GC0, C_WIDTH))
    for c in range(T // SGU_CHUNK):
        vn_c = vn[c * SGU_CHUNK:(c + 1) * SGU_CHUNK]
        stack = jnp.where(head_sel, jnp.concatenate([vn_c] * C_HEADS, axis=0), 0.0)
        mixed_c = _dot(sguw_ref[...], stack.astype(BF16)) + sgub_ref[...]
        ybc_ref[0, c * SGU_CHUNK:(c + 1) * SGU_CHUNK, B_WIDTH:B_WIDTH + C_WIDTH] = (
            ug[c * SGU_CHUNK:(c + 1) * SGU_CHUNK] * mixed_c)


def _proj_call(rel_bias, h, norm_g, w_qt, w_vt, w_in, pool_bd, pool_scale, sgu_wcat, sgu_btile):
    B, S, _ = h.shape
    T = ROW_TILE
    nb = S // MOBA_BLOCK
    assert nb <= MASK_LO - MASK_HI and MASK_LO + nb <= LANES
    full = lambda shape: pl.BlockSpec(shape, lambda b, s: (0,) * len(shape))
    return pl.pallas_call(
        functools.partial(_proj_kernel, n_blocks=nb),
        grid=(B, S // T),
        in_specs=[
            pl.BlockSpec(memory_space=pltpu.SMEM),
            pl.BlockSpec((1, T, D_MODEL), lambda b, s: (b, s, 0)),
            full((1, D_MODEL)),
            full((A_WIDTH, D_MODEL)),
            full((A_WIDTH, D_MODEL)),
            full((D_MODEL, D_IN)),
            full((B_WIDTH, B_WIDTH)),
            full((1, B_WIDTH)),
            full((SGU_CHUNK, C_HEADS * SGU_CHUNK)),
            full((SGU_CHUNK, C_WIDTH)),
        ],
        out_specs=[
            pl.BlockSpec((1, A_HEADS, T, LANES), lambda b, s: (b, 0, s, 0)),
            pl.BlockSpec((1, A_HEADS, 1, MOBA_BLOCK, LANES), lambda b, s: (b, 0, s, 0, 0)),
            pl.BlockSpec((1, A_HEADS, 1, V_ROWS, MOBA_BLOCK), lambda b, s: (b, 0, s, 0, 0)),
            pl.BlockSpec((1, T, A_WIDTH), lambda b, s: (b, s, 0)),
            pl.BlockSpec((1, T, B_WIDTH + C_WIDTH), lambda b, s: (b, s, 0)),
        ],
        out_shape=[
            jax.ShapeDtypeStruct((B, A_HEADS, S, LANES), BF16),
            jax.ShapeDtypeStruct((B, A_HEADS, nb, MOBA_BLOCK, LANES), BF16),
            jax.ShapeDtypeStruct((B, A_HEADS, nb, V_ROWS, MOBA_BLOCK), BF16),
            jax.ShapeDtypeStruct((B, S, A_WIDTH), F32),
            jax.ShapeDtypeStruct((B, S, B_WIDTH + C_WIDTH), F32),
        ],
        scratch_shapes=[
            pltpu.VMEM((A_HEADS * nb, A_WIDTH), F32),
            pltpu.VMEM((MAX_WINDOW + T, B_WIDTH), F32),
        ],
        compiler_params=pltpu.CompilerParams(
            dimension_semantics=("arbitrary", "arbitrary"), vmem_limit_bytes=VMEM_LIMIT),
        name="proj_mix",
    )(rel_bias, h, norm_g, w_qt, w_vt, w_in, pool_bd, pool_scale, sgu_wcat, sgu_btile)


def _attn_kernel(q_ref, k_ref, vt_ref, bias_ref, o_ref, sa_scr, sb_scr):
    qi = pl.program_id(2)
    Tq = o_ref.shape[1]
    prev = jnp.maximum(qi - 1, 0)
    n_far = prev
    n_pairs = jnp.maximum(n_far - 1, 0) // 2
    parked = 2 * n_pairs
    extra = parked + 1
    pen_parked = jnp.where(n_far >= 1, 0.0, NEG)
    pen_extra = jnp.where(extra < n_far, 0.0, NEG)
    pen_prev = jnp.where(qi >= 1, 0.0, NEG)
    heads = range(ATTN_HEADS)

    def scores(hd, j):
        return _dot_nt(k_ref[0, hd, j], q_ref[0, hd])

    def park(scr, j):
        for hd in heads:
            scr[hd] = scores(hd, j)

    def softmax_pv(hd, m, acc, items):
        col_max = [m]
        for _, st, pen in items:
            cm = jnp.max(st, axis=0, keepdims=True)
            col_max.append(cm if pen is None else cm + pen)
        m_new = functools.reduce(jnp.maximum, col_max)
        pv = None
        for j, st, pen in items:
            shift = m_new if pen is None else m_new - pen
            d = _dot(vt_ref[0, hd, j], jnp.exp2(st - shift).astype(BF16))
            pv = d if pv is None else pv + d
        return m_new, jnp.exp2(m - m_new) * acc + pv

    def consume(scr, j, state):
        out = []
        for hd in heads:
            out.extend(softmax_pv(hd, state[2 * hd], state[2 * hd + 1], [(j, scr[hd], None)]))
        return tuple(out)

    park(sa_scr, 0)

    def pair_step(t, state):
        park(sb_scr, 2 * t + 1)
        state = consume(sa_scr, 2 * t, state)
        park(sa_scr, 2 * t + 2)
        return consume(sb_scr, 2 * t + 1, state)

    init = (jnp.full((1, Tq), -jnp.inf, F32), jnp.zeros((V_ROWS, Tq), F32)) * ATTN_HEADS
    state = lax.fori_loop(0, n_pairs, pair_step, init)

    outs = []
    for hd in heads:
        items = [(parked, sa_scr[hd], pen_parked),
                 (extra, scores(hd, extra), pen_extra),
                 (prev, scores(hd, prev) + bias_ref[hd, 1], pen_prev),
                 (qi, scores(hd, qi) + bias_ref[hd, 0], None)]
        _, acc = softmax_pv(hd, state[2 * hd], state[2 * hd + 1], items)
        outs.append(acc[0:HEAD_DIM] / acc[HEAD_DIM:HEAD_DIM + 1])
    o_ref[0] = jnp.concatenate(outs, axis=0).T


def _attn_call(q, k_blk, vt_blk, bias_tiles):
    B, H, S, _ = q.shape
    nb = S // MOBA_BLOCK
    Tq = MOBA_BLOCK
    G = ATTN_HEADS
    return pl.pallas_call(
        _attn_kernel,
        grid=(B, H // G, S // Tq),
        in_specs=[
            pl.BlockSpec((1, G, Tq, LANES), lambda b, hg, qi: (b, hg, qi, 0)),
            pl.BlockSpec((1, G, nb, MOBA_BLOCK, LANES), lambda b, hg, qi: (b, hg, 0, 0, 0)),
            pl.BlockSpec((1, G, nb, V_ROWS, MOBA_BLOCK), lambda b, hg, qi: (b, hg, 0, 0, 0)),
            pl.BlockSpec((G, 2, MOBA_BLOCK, MOBA_BLOCK), lambda b, hg, qi: (hg, 0, 0, 0)),
        ],
        out_specs=pl.BlockSpec((1, Tq, G * HEAD_DIM), lambda b, hg, qi: (b, qi, hg)),
        out_shape=jax.ShapeDtypeStruct((B, S, H * HEAD_DIM), F32),
        scratch_shapes=[pltpu.VMEM((G, MOBA_BLOCK, Tq), F32)] * 2,
        compiler_params=pltpu.CompilerParams(
            dimension_semantics=("arbitrary", "arbitrary", "arbitrary"),
            vmem_limit_bytes=VMEM_LIMIT),
        name="moba_attn",
    )(q, k_blk, vt_blk, bias_tiles)


def _out_kernel(h_ref, ya_ref, gsa_ref, ybc_ref, p_ref, wout_ref, plew_ref, gatew_ref,
                fg_ref, o_ref, *, final):
    ya = (ya_ref[...] * gsa_ref[...]).astype(BF16)
    y = _dot(ya, wout_ref[0:A_WIDTH, :]) + _dot(ybc_ref[...].astype(BF16), wout_ref[A_WIDTH:, :])
    h1 = h_ref[...] + y
    emb = _dot(p_ref[...].astype(BF16), plew_ref[...])
    gate = 1.0 / (1.0 + jnp.exp(-_dot(h1.astype(BF16), gatew_ref[...])))
    h2 = h1 + emb * gate
    if final:
        h2 = h2 * lax.rsqrt(jnp.mean(h2 * h2, axis=-1, keepdims=True) + EPS) * fg_ref[...]
    o_ref[...] = h2


def _out_call(h, ya, gsa, ybc, p, w_out, ple_w, gate_w, final_g, final):
    N = h.shape[0]
    T = OUT_TILE
    row = lambda width: pl.BlockSpec((T, width), lambda i: (i, 0))
    full = lambda shape: pl.BlockSpec(shape, lambda i: (0,) * len(shape))
    return pl.pallas_call(
        functools.partial(_out_kernel, final=final),
        grid=(N // T,),
        in_specs=[
            row(D_MODEL), row(A_WIDTH), row(A_WIDTH), row(B_WIDTH + C_WIDTH), row(D_PLE),
            full((D_MODEL, D_MODEL)), full((D_PLE, D_MODEL)), full((D_MODEL, D_MODEL)),
            full((1, D_MODEL)),
        ],
        out_specs=row(D_MODEL),
        out_shape=jax.ShapeDtypeStruct((N, D_MODEL), F32),
        compiler_params=pltpu.CompilerParams(
            dimension_semantics=("arbitrary",), vmem_limit_bytes=VMEM_LIMIT),
        name="out_ple",
    )(h, ya, gsa, ybc, p, w_out, ple_w, gate_w, final_g)


def _block_diag(w):
    G, c, d = w.shape
    eye = jnp.eye(G, dtype=w.dtype)
    return (eye[:, None, :, None] * w[:, :, None, :]).reshape(G * c, G * d)


def kernel(x, p, norm_g, w_in, w_out, rel_bias, pool_w, pool_scale, sgu_w, sgu_b,
           ple_w, ple_gate_w, final_g):
    B, S, D = x.shape
    depth = w_in.shape[0]
    N = B * S
    bias_tiles = _bias_tiles(rel_bias)
    h = x
    for i in range(depth):
        w_in_b = w_in[i].astype(BF16)
        sgu_wcat = jnp.transpose(jnp.tril(sgu_w[i]), (1, 0, 2)).reshape(
            SGU_CHUNK, C_HEADS * SGU_CHUNK).astype(BF16)
        sgu_btile = jnp.repeat(jnp.transpose(sgu_b[i]), HEAD_DIM, axis=1)
        q, k_blk, vt_blk, gsa, ybc = _proj_call(
            rel_bias, h, norm_g[i][None, :],
            jnp.transpose(w_in_b[:, Q0:Q0 + A_WIDTH]), jnp.transpose(w_in_b[:, V0:V0 + A_WIDTH]),
            w_in_b, _block_diag(pool_w[i]).astype(BF16), pool_scale[i][None, :],
            sgu_wcat, sgu_btile)
        ya = _attn_call(q, k_blk, vt_blk, bias_tiles)
        h = _out_call(
            h.reshape(N, D), ya.reshape(N, A_WIDTH), gsa.reshape(N, A_WIDTH),
            ybc.reshape(N, B_WIDTH + C_WIDTH), p[i].reshape(N, D_PLE),
            w_out[i].astype(BF16), ple_w[i].astype(BF16), ple_gate_w[i].astype(BF16),
            final_g[None, :], final=(i == depth - 1)).reshape(B, S, D)
    return h
```

```python
import functools
import math

import numpy as np
import jax
import jax.numpy as jnp
from jax import lax
from jax.experimental import pallas as pl
from jax.experimental.pallas import tpu as pltpu

D_MODEL = 1024
HEAD_DIM = 64
A_WIDTH = 512
A_HEADS = 8
MOBA_BLOCK = 256
MOBA_TOPK = 3
REL_BUCKETS = 32
REL_MAX_DIST = 128
B_WIDTH = 256
POOL_WINDOWS = (2, 4, 8, 16)
B_GROUP = 64
C_WIDTH = 256
C_HEADS = 4
SGU_CHUNK = 128
D_PLE = 256
D_IN = 3328
EPS = 1e-6
NEG = -1e30
LOG2E = math.log2(math.e)

MAX_WINDOW = max(POOL_WINDOWS)
ROW_TILE = MOBA_BLOCK
OUT_TILE = 512
LANES = 128
MASK_HI = HEAD_DIM
MASK_LO = HEAD_DIM + 32
V_ROWS = HEAD_DIM + 16
ATTN_HEADS = 4
PAIR_UNROLL = 2
VMEM_LIMIT = 48 * 1024 * 1024

_OFF = np.cumsum((0,) + (A_WIDTH,) * 4 + (B_WIDTH,) * 2 + (C_WIDTH,) * 3)
Q0, K0, V0, GA0, XB0, GB0, UC0, VC0, GC0, _ = (int(o) for o in _OFF)

BF16 = jnp.bfloat16
F32 = jnp.float32


def _bucket_thresholds():
    max_exact = REL_BUCKETS // 2
    n = np.arange(0, 4 * MOBA_BLOCK)
    nf = np.maximum(n, 1).astype(np.float64)
    large = max_exact + (np.log(nf / max_exact) / math.log(REL_MAX_DIST / max_exact)
                         * (REL_BUCKETS - max_exact)).astype(np.int64)
    large = np.minimum(large, REL_BUCKETS - 1)
    bucket = np.where(n < max_exact, n, large)
    return [int(np.argmax(bucket >= b)) for b in range(1, REL_BUCKETS)]


_THRESHOLDS = _bucket_thresholds()


def _silu(x):
    return x * (1.0 / (1.0 + jnp.exp(-x)))


def _split_bf16(x):
    hi = x.astype(BF16)
    lo = (x - hi.astype(F32)).astype(BF16)
    return hi, lo


def _dot(a, b):
    return jnp.dot(a, b, preferred_element_type=F32)


def _dot_nt(a, b):
    return lax.dot_general(a, b, (((1,), (1,)), ((), ())), preferred_element_type=F32)


def _bias_kernel(rb_ref, out_ref):
    h = pl.program_id(0)
    key = lax.broadcasted_iota(jnp.int32, (MOBA_BLOCK, MOBA_BLOCK), 0)
    qry = lax.broadcasted_iota(jnp.int32, (MOBA_BLOCK, MOBA_BLOCK), 1)
    for kind, shift in ((0, 0), (1, MOBA_BLOCK)):
        dist = qry - key + shift
        val = jnp.full((MOBA_BLOCK, MOBA_BLOCK), rb_ref[0, h], F32)
        for b in range(1, REL_BUCKETS):
            val = jnp.where(dist >= _THRESHOLDS[b - 1], rb_ref[b, h], val)
        val = val * LOG2E
        if kind == 0:
            val = jnp.where(dist >= 0, val, NEG)
        out_ref[0, kind] = val


def _bias_tiles(rel_bias):
    return pl.pallas_call(
        _bias_kernel,
        grid=(A_HEADS,),
        in_specs=[pl.BlockSpec(memory_space=pltpu.SMEM)],
        out_specs=pl.BlockSpec((1, 2, MOBA_BLOCK, MOBA_BLOCK), lambda h: (h, 0, 0, 0)),
        out_shape=jax.ShapeDtypeStruct((A_HEADS, 2, MOBA_BLOCK, MOBA_BLOCK), F32),
        name="bias_tiles",
    )(rel_bias)


def _proj_kernel(rb_ref, h_ref, ng_ref, wqt_ref, wvt_ref, win_ref, poolw_ref, pscale_ref,
                 sguw_ref, sgub_ref,
                 q_ref, k_ref, vt_ref, gsa_ref, ybc_ref,
                 kmt_scr, ext_scr, *, n_blocks):
    T = ROW_TILE
    s = pl.program_id(1)

    @pl.when(s == 0)
    def _():
        kmt_scr[...] = jnp.zeros_like(kmt_scr)
        ext_scr[0:MAX_WINDOW, :] = jnp.zeros((MAX_WINDOW, B_WIDTH), F32)

    h = h_ref[0]
    hn = h * lax.rsqrt(jnp.mean(h * h, axis=-1, keepdims=True) + EPS) * ng_ref[...]
    hb = hn.astype(BF16)

    def proj(c0, width):
        return _dot(hb, win_ref[:, c0:c0 + width])

    zqt = _dot_nt(wqt_ref[...], hb) * (HEAD_DIM ** -0.5 * LOG2E)

    km_hi, km_lo = _split_bf16(kmt_scr[...])
    q_hi, q_lo = _split_bf16(zqt)
    gate_t = _dot(km_hi, q_hi) + _dot(km_hi, q_lo) + _dot(km_lo, q_hi)
    jrow = lax.broadcasted_iota(jnp.int32, (n_blocks, T), 0)
    for hd in range(A_HEADS):
        g = jnp.where(jrow < s, gate_t[hd * n_blocks:(hd + 1) * n_blocks], NEG)
        sel = jnp.zeros((n_blocks, T), jnp.bool_)
        for _ in range(MOBA_TOPK):
            m = jnp.max(g, axis=0, keepdims=True)
            idx = jnp.min(jnp.where(g == m, jrow, n_blocks), axis=0, keepdims=True)
            pick = jrow == idx
            sel = sel | (pick & (m > NEG * 0.5))
            g = jnp.where(pick, -jnp.inf, g)
        far_bias = rb_ref[REL_BUCKETS - 1, hd] * LOG2E
        term = jnp.where(sel, jnp.where(jrow <= s - 2, far_bias, 0.0), NEG)
        term = jnp.where(jrow == s, 0.0, term)
        term_hi = term.astype(BF16).astype(F32)
        q_aug_t = jnp.concatenate(
            [zqt[hd * HEAD_DIM:(hd + 1) * HEAD_DIM], term_hi, term - term_hi], axis=0)
        q_ref[0, hd] = q_aug_t.T.astype(BF16)

    zk = proj(K0, A_WIDTH)
    lane = lax.broadcasted_iota(jnp.int32, (1, LANES), 1)
    indicator = jnp.where((lane == MASK_HI + s) | (lane == MASK_LO + s), 1.0, 0.0)
    for pair in range(A_HEADS // 2):
        zk_pair = zk[:, pair * LANES:(pair + 1) * LANES]
        k_ref[0, 2 * pair, 0] = jnp.where(lane < HEAD_DIM, zk_pair, indicator).astype(BF16)
        k_ref[0, 2 * pair + 1, 0] = jnp.where(
            lane < HEAD_DIM, pltpu.roll(zk_pair, HEAD_DIM, 1), indicator).astype(BF16)
    k_mean = jnp.mean(zk, axis=0, keepdims=True)
    lane_a = lax.broadcasted_iota(jnp.int32, (1, A_WIDTH), 1)
    for hd in range(A_HEADS):
        in_head = (lane_a >= hd * HEAD_DIM) & (lane_a < (hd + 1) * HEAD_DIM)
        kmt_scr[pl.ds(hd * n_blocks + s, 1), :] = jnp.where(in_head, k_mean, 0.0)

    zvt = _dot_nt(wvt_ref[...], hb)
    tail_row = lax.broadcasted_iota(jnp.int32, (V_ROWS - HEAD_DIM, T), 0)
    tail = jnp.where(tail_row == 0, 1.0, 0.0)
    for hd in range(A_HEADS):
        vt_ref[0, hd, 0] = jnp.concatenate(
            [zvt[hd * HEAD_DIM:(hd + 1) * HEAD_DIM], tail], axis=0).astype(BF16)

    gsa_ref[0] = _silu(proj(GA0, A_WIDTH))

    xb = proj(XB0, B_WIDTH)
    ext_scr[MAX_WINDOW:MAX_WINDOW + T, :] = xb
    lane_b = lax.broadcasted_iota(jnp.int32, (1, B_WIDTH), 1)
    win = jnp.zeros((1, B_WIDTH), F32)
    for gi, w in enumerate(POOL_WINDOWS):
        win = jnp.where(lane_b // B_GROUP == gi, float(w), win)
    run = jnp.zeros((T, B_WIDTH), F32)
    wsum = jnp.zeros((T, B_WIDTH), F32)
    for lag in range(MAX_WINDOW):
        run = run + ext_scr[MAX_WINDOW - lag:MAX_WINDOW - lag + T, :]
        if lag + 1 in POOL_WINDOWS:
            wsum = jnp.where(win == float(lag + 1), run, wsum)
    pos = (s * T + lax.broadcasted_iota(jnp.int32, (T, 1), 0) + 1).astype(F32)
    pooled = wsum / jnp.minimum(pos, win) - xb
    ext_scr[0:MAX_WINDOW, :] = xb[T - MAX_WINDOW:T, :]
    mixed_b = _dot(pooled.astype(BF16), poolw_ref[...]) * pscale_ref[...]
    ybc_ref[0, :, 0:B_WIDTH] = mixed_b * _silu(proj(GB0, B_WIDTH))

    vc = proj(VC0, C_WIDTH)
    mu = jnp.mean(vc, axis=-1, keepdims=True)
    cen = vc - mu
    var = jnp.mean(cen * cen, axis=-1, keepdims=True)
    vn = cen * lax.rsqrt(var + EPS)
    rows = lax.broadcasted_iota(jnp.int32, (C_HEADS * SGU_CHUNK, C_WIDTH), 0)
    cols = lax.broadcasted_iota(jnp.int32, (C_HEADS * SGU_CHUNK, C_WIDTH), 1)
    head_sel = (rows // SGU_CHUNK) == (cols // HEAD_DIM)
    ug = proj(UC0, C_WIDTH) * _silu(proj(GC0, C_WIDTH))
    for c in range(T // SGU_CHUNK):
        vn_c = vn[c * SGU_CHUNK:(c + 1) * SGU_CHUNK]
        stack = jnp.where(head_sel, jnp.concatenate([vn_c] * C_HEADS, axis=0), 0.0)
        mixed_c = _dot(sguw_ref[...], stack.astype(BF16)) + sgub_ref[...]
        ybc_ref[0, c * SGU_CHUNK:(c + 1) * SGU_CHUNK, B_WIDTH:B_WIDTH + C_WIDTH] = (
            ug[c * SGU_CHUNK:(c + 1) * SGU_CHUNK] * mixed_c)


def _proj_call(rel_bias, h, norm_g, w_qt, w_vt, w_in, pool_bd, pool_scale, sgu_wcat, sgu_btile):
    B, S, _ = h.shape
    T = ROW_TILE
    nb = S // MOBA_BLOCK
    assert nb <= MASK_LO - MASK_HI and MASK_LO + nb <= LANES
    full = lambda shape: pl.BlockSpec(shape, lambda b, s: (0,) * len(shape))
    return pl.pallas_call(
        functools.partial(_proj_kernel, n_blocks=nb),
        grid=(B, S // T),
        in_specs=[
            pl.BlockSpec(memory_space=pltpu.SMEM),
            pl.BlockSpec((1, T, D_MODEL), lambda b, s: (b, s, 0)),
            full((1, D_MODEL)),
            full((A_WIDTH, D_MODEL)),
            full((A_WIDTH, D_MODEL)),
            full((D_MODEL, D_IN)),
            full((B_WIDTH, B_WIDTH)),
            full((1, B_WIDTH)),
            full((SGU_CHUNK, C_HEADS * SGU_CHUNK)),
            full((SGU_CHUNK, C_WIDTH)),
        ],
        out_specs=[
            pl.BlockSpec((1, A_HEADS, T, LANES), lambda b, s: (b, 0, s, 0)),
            pl.BlockSpec((1, A_HEADS, 1, MOBA_BLOCK, LANES), lambda b, s: (b, 0, s, 0, 0)),
            pl.BlockSpec((1, A_HEADS, 1, V_ROWS, MOBA_BLOCK), lambda b, s: (b, 0, s, 0, 0)),
            pl.BlockSpec((1, T, A_WIDTH), lambda b, s: (b, s, 0)),
            pl.BlockSpec((1, T, B_WIDTH + C_WIDTH), lambda b, s: (b, s, 0)),
        ],
        out_shape=[
            jax.ShapeDtypeStruct((B, A_HEADS, S, LANES), BF16),
            jax.ShapeDtypeStruct((B, A_HEADS, nb, MOBA_BLOCK, LANES), BF16),
            jax.ShapeDtypeStruct((B, A_HEADS, nb, V_ROWS, MOBA_BLOCK), BF16),
            jax.ShapeDtypeStruct((B, S, A_WIDTH), F32),
            jax.ShapeDtypeStruct((B, S, B_WIDTH + C_WIDTH), F32),
        ],
        scratch_shapes=[
            pltpu.VMEM((A_HEADS * nb, A_WIDTH), F32),
            pltpu.VMEM((MAX_WINDOW + T, B_WIDTH), F32),
        ],
        compiler_params=pltpu.CompilerParams(
            dimension_semantics=("arbitrary", "arbitrary"), vmem_limit_bytes=VMEM_LIMIT),
        name="proj_mix",
    )(rel_bias, h, norm_g, w_qt, w_vt, w_in, pool_bd, pool_scale, sgu_wcat, sgu_btile)


def _attn_kernel(q_ref, k_ref, vt_ref, bias_ref, o_ref, sa_scr, sb_scr, m_scr, acc_scr):
    i = pl.program_id(2)
    W = MOBA_BLOCK
    n_far = jnp.maximum(2 * i - 1, 0)
    n_pairs = jnp.maximum(n_far - 1, 0) // 2
    parked = 2 * n_pairs
    extra = parked + 1
    before = jnp.maximum(2 * i - 1, 0)
    pen_parked = jnp.where(n_far >= 1, 0.0, NEG)
    pen_extra = jnp.where(extra < n_far, 0.0, NEG)
    pen_before = jnp.where(i >= 1, 0.0, NEG)
    heads = range(ATTN_HEADS)
    OWN, PREV = 0, 1

    def park(hd, scr, j, bias_kinds=(None, None)):
        st = _dot_nt(k_ref[0, hd, j], q_ref[0, hd])
        for half, kind in enumerate(bias_kinds):
            cols = slice(half * W, (half + 1) * W)
            part = st[:, cols]
            scr[hd, :, cols] = part if kind is None else part + bias_ref[hd, kind]

    def consume(hd, scr, j, pen=None):
        for half in range(2):
            cols = slice(half * W, (half + 1) * W)
            st = scr[hd, :, cols]
            m = m_scr[hd, :, cols]
            cm = jnp.max(st, axis=0, keepdims=True)
            m_new = jnp.maximum(m, cm if pen is None else cm + pen)
            shift = m_new if pen is None else m_new - pen
            pv = _dot(vt_ref[0, hd, j], jnp.exp2(st - shift).astype(BF16))
            m_scr[hd, :, cols] = m_new
            acc_scr[hd, :, cols] = jnp.exp2(m - m_new) * acc_scr[hd, :, cols] + pv

    def stage(produce=None, take=None):
        for hd in heads:
            if produce is not None:
                park(hd, *produce)
            if take is not None:
                consume(hd, *take)

    m_scr[...] = jnp.full(m_scr.shape, -jnp.inf, F32)
    acc_scr[...] = jnp.zeros(acc_scr.shape, F32)

    stage(produce=(sa_scr, 0))

    def pair_step(t, carry):
        stage(produce=(sb_scr, 2 * t + 1), take=(sa_scr, 2 * t))
        stage(produce=(sa_scr, 2 * t + 2), take=(sb_scr, 2 * t + 1))
        return carry

    def long_step(u, carry):
        for r in range(PAIR_UNROLL):
            pair_step(u * PAIR_UNROLL + r, carry)
        return carry

    n_long = n_pairs // PAIR_UNROLL
    lax.fori_loop(0, n_long, long_step, 0)
    lax.fori_loop(n_long * PAIR_UNROLL, n_pairs, pair_step, 0)

    stage(produce=(sb_scr, extra), take=(sa_scr, parked, pen_parked))
    stage(produce=(sa_scr, before, (PREV, None)), take=(sb_scr, extra, pen_extra))
    stage(produce=(sb_scr, 2 * i, (OWN, PREV)), take=(sa_scr, before, pen_before))
    stage(produce=(sa_scr, 2 * i + 1, (None, OWN)), take=(sb_scr, 2 * i))
    stage(take=(sa_scr, 2 * i + 1))
    outs = [acc_scr[hd, 0:HEAD_DIM, :] / acc_scr[hd, HEAD_DIM:HEAD_DIM + 1, :] for hd in heads]
    o_ref[0] = jnp.concatenate(outs, axis=0).T


def _attn_call(q, k_blk, vt_blk, bias_tiles):
    B, H, S, _ = q.shape
    nb = S // MOBA_BLOCK
    Tq = 2 * MOBA_BLOCK
    G = ATTN_HEADS
    return pl.pallas_call(
        _attn_kernel,
        grid=(B, H // G, S // Tq),
        in_specs=[
            pl.BlockSpec((1, G, Tq, LANES), lambda b, hg, i: (b, hg, i, 0)),
            pl.BlockSpec((1, G, nb, MOBA_BLOCK, LANES), lambda b, hg, i: (b, hg, 0, 0, 0)),
            pl.BlockSpec((1, G, nb, V_ROWS, MOBA_BLOCK), lambda b, hg, i: (b, hg, 0, 0, 0)),
            pl.BlockSpec((G, 2, MOBA_BLOCK, MOBA_BLOCK), lambda b, hg, i: (hg, 0, 0, 0)),
        ],
        out_specs=pl.BlockSpec((1, Tq, G * HEAD_DIM), lambda b, hg, i: (b, i, hg)),
        out_shape=jax.ShapeDtypeStruct((B, S, H * HEAD_DIM), F32),
        scratch_shapes=[
            pltpu.VMEM((G, MOBA_BLOCK, Tq), F32),
            pltpu.VMEM((G, MOBA_BLOCK, Tq), F32),
            pltpu.VMEM((G, 1, Tq), F32),
            pltpu.VMEM((G, V_ROWS, Tq), F32),
        ],
        compiler_params=pltpu.CompilerParams(
            dimension_semantics=("arbitrary", "arbitrary", "arbitrary"),
            vmem_limit_bytes=VMEM_LIMIT),
        name="moba_attn",
    )(q, k_blk, vt_blk, bias_tiles)


def _out_kernel(h_ref, ya_ref, gsa_ref, ybc_ref, p_ref, wout_ref, plew_ref, gatew_ref,
                fg_ref, o_ref, *, final):
    ya = (ya_ref[...] * gsa_ref[...]).astype(BF16)
    y = _dot(ya, wout_ref[0:A_WIDTH, :]) + _dot(ybc_ref[...].astype(BF16), wout_ref[A_WIDTH:, :])
    h1 = h_ref[...] + y
    emb = _dot(p_ref[...].astype(BF16), plew_ref[...])
    gate = 1.0 / (1.0 + jnp.exp(-_dot(h1.astype(BF16), gatew_ref[...])))
    h2 = h1 + emb * gate
    if final:
        h2 = h2 * lax.rsqrt(jnp.mean(h2 * h2, axis=-1, keepdims=True) + EPS) * fg_ref[...]
    o_ref[...] = h2


def _out_call(h, ya, gsa, ybc, p, w_out, ple_w, gate_w, final_g, final):
    N = h.shape[0]
    T = OUT_TILE
    row = lambda width: pl.BlockSpec((T, width), lambda i: (i, 0))
    full = lambda shape: pl.BlockSpec(shape, lambda i: (0,) * len(shape))
    return pl.pallas_call(
        functools.partial(_out_kernel, final=final),
        grid=(N // T,),
        in_specs=[
            row(D_MODEL), row(A_WIDTH), row(A_WIDTH), row(B_WIDTH + C_WIDTH), row(D_PLE),
            full((D_MODEL, D_MODEL)), full((D_PLE, D_MODEL)), full((D_MODEL, D_MODEL)),
            full((1, D_MODEL)),
        ],
        out_specs=row(D_MODEL),
        out_shape=jax.ShapeDtypeStruct((N, D_MODEL), F32),
        compiler_params=pltpu.CompilerParams(
            dimension_semantics=("arbitrary",), vmem_limit_bytes=VMEM_LIMIT),
        name="out_ple",
    )(h, ya, gsa, ybc, p, w_out, ple_w, gate_w, final_g)


def _block_diag(w):
    G, c, d = w.shape
    eye = jnp.eye(G, dtype=w.dtype)
    return (eye[:, None, :, None] * w[:, :, None, :]).reshape(G * c, G * d)


def kernel(x, p, norm_g, w_in, w_out, rel_bias, pool_w, pool_scale, sgu_w, sgu_b,
           ple_w, ple_gate_w, final_g):
    B, S, D = x.shape
    depth = w_in.shape[0]
    N = B * S
    bias_tiles = _bias_tiles(rel_bias)
    h = x
    for i in range(depth):
        w_in_b = w_in[i].astype(BF16)
        sgu_wcat = jnp.transpose(jnp.tril(sgu_w[i]), (1, 0, 2)).reshape(
            SGU_CHUNK, C_HEADS * SGU_CHUNK).astype(BF16)
        sgu_btile = jnp.repeat(jnp.transpose(sgu_b[i]), HEAD_DIM, axis=1)
        q, k_blk, vt_blk, gsa, ybc = _proj_call(
            rel_bias, h, norm_g[i][None, :],
            jnp.transpose(w_in_b[:, Q0:Q0 + A_WIDTH]), jnp.transpose(w_in_b[:, V0:V0 + A_WIDTH]),
            w_in_b, _block_diag(pool_w[i]).astype(BF16), pool_scale[i][None, :],
            sgu_wcat, sgu_btile)
        ya = _attn_call(q, k_blk, vt_blk, bias_tiles)
        h = _out_call(
            h.reshape(N, D), ya.reshape(N, A_WIDTH), gsa.reshape(N, A_WIDTH),
            ybc.reshape(N, B_WIDTH + C_WIDTH), p[i].reshape(N, D_PLE),
            w_out[i].astype(BF16), ple_w[i].astype(BF16), ple_gate_w[i].astype(BF16),
            final_g[None, :], final=(i == depth - 1)).reshape(B, S, D)
    return h
```

```python
import functools
import math

import numpy as np
import jax
import jax.numpy as jnp
from jax import lax
from jax.experimental import pallas as pl
from jax.experimental.pallas import tpu as pltpu

D_MODEL = 1024
HEAD_DIM = 64
A_WIDTH = 512
A_HEADS = 8
MOBA_BLOCK = 256
MOBA_TOPK = 3
REL_BUCKETS = 32
REL_MAX_DIST = 128
B_WIDTH = 256
POOL_WINDOWS = (2, 4, 8, 16)
B_GROUP = 64
C_WIDTH = 256
C_HEADS = 4
SGU_CHUNK = 128
D_PLE = 256
D_IN = 3328
EPS = 1e-6
NEG = -1e30
LOG2E = math.log2(math.e)
Q_SCALE = HEAD_DIM ** -0.5 * LOG2E

MAX_WINDOW = max(POOL_WINDOWS)
ROW_BLOCKS = 2
ROW_TILE = ROW_BLOCKS * MOBA_BLOCK
OUT_TILE = 512
LANES = 128
MASK_HI = HEAD_DIM
MASK_LO = HEAD_DIM + 32
V_ROWS = HEAD_DIM + 16
ATTN_HEADS = 4
PAIR_UNROLL = 2
VMEM_LIMIT = 48 * 1024 * 1024

_OFF = np.cumsum((0,) + (A_WIDTH,) * 4 + (B_WIDTH,) * 2 + (C_WIDTH,) * 3)
Q0, K0, V0, GA0, XB0, GB0, UC0, VC0, GC0, _ = (int(o) for o in _OFF)

BF16 = jnp.bfloat16
F32 = jnp.float32


def _bucket_thresholds():
    max_exact = REL_BUCKETS // 2
    n = np.arange(0, 4 * MOBA_BLOCK)
    nf = np.maximum(n, 1).astype(np.float64)
    large = max_exact + (np.log(nf / max_exact) / math.log(REL_MAX_DIST / max_exact)
                         * (REL_BUCKETS - max_exact)).astype(np.int64)
    large = np.minimum(large, REL_BUCKETS - 1)
    bucket = np.where(n < max_exact, n, large)
    return [int(np.argmax(bucket >= b)) for b in range(1, REL_BUCKETS)]


_THRESHOLDS = _bucket_thresholds()


def _sigmoid(x):
    return 0.5 * jnp.tanh(0.5 * x) + 0.5


def _silu(x):
    return x * _sigmoid(x)


def _split_bf16(x):
    hi = x.astype(BF16)
    lo = (x - hi.astype(F32)).astype(BF16)
    return hi, lo


def _dot(a, b):
    return jnp.dot(a, b, preferred_element_type=F32)


def _dot_nt(a, b):
    return lax.dot_general(a, b, (((1,), (1,)), ((), ())), preferred_element_type=F32)


def _bias_kernel(rb_ref, out_ref):
    h = pl.program_id(0)
    key = lax.broadcasted_iota(jnp.int32, (MOBA_BLOCK, MOBA_BLOCK), 0)
    qry = lax.broadcasted_iota(jnp.int32, (MOBA_BLOCK, MOBA_BLOCK), 1)
    for kind, shift in ((0, 0), (1, MOBA_BLOCK)):
        dist = qry - key + shift
        val = jnp.full((MOBA_BLOCK, MOBA_BLOCK), rb_ref[0, h], F32)
        for b in range(1, REL_BUCKETS):
            val = jnp.where(dist >= _THRESHOLDS[b - 1], rb_ref[b, h], val)
        val = val * LOG2E
        if kind == 0:
            val = jnp.where(dist >= 0, val, NEG)
        out_ref[0, kind] = val


def _bias_tiles(rel_bias):
    return pl.pallas_call(
        _bias_kernel,
        grid=(A_HEADS,),
        in_specs=[pl.BlockSpec(memory_space=pltpu.SMEM)],
        out_specs=pl.BlockSpec((1, 2, MOBA_BLOCK, MOBA_BLOCK), lambda h: (h, 0, 0, 0)),
        out_shape=jax.ShapeDtypeStruct((A_HEADS, 2, MOBA_BLOCK, MOBA_BLOCK), F32),
        name="bias_tiles",
    )(rel_bias)


def _proj_kernel(rb_ref, h_ref, ng_ref, wqt_ref, wvt_ref, win_ref, poolw_ref, pscale_ref,
                 sguw_ref, sgub_ref,
                 q_ref, k_ref, vt_ref, gsa_ref, ybc_ref,
                 kmt_scr, ext_scr, *, n_blocks):
    T = ROW_TILE
    W = MOBA_BLOCK
    s = pl.program_id(1)
    first_block = s * ROW_BLOCKS

    @pl.when(s == 0)
    def _():
        kmt_scr[...] = jnp.zeros_like(kmt_scr)
        ext_scr[0:MAX_WINDOW, :] = jnp.zeros((MAX_WINDOW, B_WIDTH), F32)

    h = h_ref[0]
    hn = h * lax.rsqrt(jnp.mean(h * h, axis=-1, keepdims=True) + EPS) * ng_ref[...]
    hb = hn.astype(BF16)

    def proj(c0, width):
        return _dot(hb, win_ref[:, c0:c0 + width])

    zk = proj(K0, A_WIDTH)
    lane = lax.broadcasted_iota(jnp.int32, (1, LANES), 1)
    lane_a = lax.broadcasted_iota(jnp.int32, (1, A_WIDTH), 1)
    for blk in range(ROW_BLOCKS):
        j = first_block + blk
        zk_blk = zk[blk * W:(blk + 1) * W]
        indicator = jnp.where((lane == MASK_HI + j) | (lane == MASK_LO + j), 1.0, 0.0)
        for pair in range(A_HEADS // 2):
            zk_pair = zk_blk[:, pair * LANES:(pair + 1) * LANES]
            k_ref[0, 2 * pair, blk] = jnp.where(lane < HEAD_DIM, zk_pair, indicator).astype(BF16)
            k_ref[0, 2 * pair + 1, blk] = jnp.where(
                lane < HEAD_DIM, pltpu.roll(zk_pair, HEAD_DIM, 1), indicator).astype(BF16)
        k_mean = jnp.mean(zk_blk, axis=0, keepdims=True)
        for hd in range(A_HEADS):
            in_head = (lane_a >= hd * HEAD_DIM) & (lane_a < (hd + 1) * HEAD_DIM)
            kmt_scr[pl.ds(hd * n_blocks + j, 1), :] = jnp.where(in_head, k_mean, 0.0)

    zqt = _dot_nt(wqt_ref[...], hb)

    km_hi, km_lo = _split_bf16(kmt_scr[...])
    q_hi, q_lo = _split_bf16(zqt)
    gate_t = _dot(km_hi, q_hi) + _dot(km_hi, q_lo) + _dot(km_lo, q_hi)
    jrow = lax.broadcasted_iota(jnp.int32, (n_blocks, T), 0)
    own = first_block + lax.broadcasted_iota(jnp.int32, (1, T), 1) // W
    for hd in range(A_HEADS):
        g = jnp.where(jrow < own, gate_t[hd * n_blocks:(hd + 1) * n_blocks], NEG)
        sel = jnp.zeros((n_blocks, T), jnp.bool_)
        for _ in range(MOBA_TOPK):
            m = jnp.max(g, axis=0, keepdims=True)
            idx = jnp.min(jnp.where(g == m, jrow, n_blocks), axis=0, keepdims=True)
            pick = jrow == idx
            sel = sel | (pick & (m > NEG * 0.5))
            g = jnp.where(pick, -jnp.inf, g)
        far_bias = rb_ref[REL_BUCKETS - 1, hd] * LOG2E
        term = jnp.where(sel, jnp.where(jrow <= own - 2, far_bias, 0.0), NEG)
        term = jnp.where(jrow == own, 0.0, term)
        term_hi = term.astype(BF16).astype(F32)
        q_aug_t = jnp.concatenate(
            [zqt[hd * HEAD_DIM:(hd + 1) * HEAD_DIM], term_hi, term - term_hi], axis=0)
        q_ref[0, hd] = q_aug_t.T.astype(BF16)

    zvt = _dot_nt(wvt_ref[...], hb)
    tail_row = lax.broadcasted_iota(jnp.int32, (V_ROWS - HEAD_DIM, W), 0)
    tail = jnp.where(tail_row == 0, 1.0, 0.0)
    for hd in range(A_HEADS):
        for blk in range(ROW_BLOCKS):
            vt_ref[0, hd, blk] = jnp.concatenate(
                [zvt[hd * HEAD_DIM:(hd + 1) * HEAD_DIM, blk * W:(blk + 1) * W], tail],
                axis=0).astype(BF16)

    gsa_ref[0] = _silu(proj(GA0, A_WIDTH))

    xb = proj(XB0, B_WIDTH)
    ext_scr[MAX_WINDOW:MAX_WINDOW + T, :] = xb
    lane_b = lax.broadcasted_iota(jnp.int32, (1, B_WIDTH), 1)
    win = jnp.zeros((1, B_WIDTH), F32)
    for gi, w in enumerate(POOL_WINDOWS):
        win = jnp.where(lane_b // B_GROUP == gi, float(w), win)
    halves = []
    for half, (w_small, w_big) in enumerate(zip(POOL_WINDOWS[0::2], POOL_WINDOWS[1::2])):
        cols = slice(half * LANES, (half + 1) * LANES)
        run = xb[:, cols]
        for lag in range(1, w_small):
            run = run + ext_scr[MAX_WINDOW - lag:MAX_WINDOW - lag + T, cols]
        small = run
        for lag in range(w_small, w_big):
            run = run + ext_scr[MAX_WINDOW - lag:MAX_WINDOW - lag + T, cols]
        halves.append(jnp.where(lane < B_GROUP, small, run))
    wsum = jnp.concatenate(halves, axis=1)
    pos = (s * T + lax.broadcasted_iota(jnp.int32, (T, 1), 0) + 1).astype(F32)
    pooled = wsum / jnp.minimum(pos, win) - xb
    ext_scr[0:MAX_WINDOW, :] = xb[T - MAX_WINDOW:T, :]
    mixed_b = _dot(pooled.astype(BF16), poolw_ref[...]) * pscale_ref[...]
    ybc_ref[0, :, 0:B_WIDTH] = mixed_b * _silu(proj(GB0, B_WIDTH))

    vc = proj(VC0, C_WIDTH)
    mu = jnp.mean(vc, axis=-1, keepdims=True)
    cen = vc - mu
    var = jnp.mean(cen * cen, axis=-1, keepdims=True)
    vn = cen * lax.rsqrt(var + EPS)
    rows = lax.broadcasted_iota(jnp.int32, (C_HEADS * SGU_CHUNK, C_WIDTH), 0)
    cols = lax.broadcasted_iota(jnp.int32, (C_HEADS * SGU_CHUNK, C_WIDTH), 1)
    head_sel = (rows // SGU_CHUNK) == (cols // HEAD_DIM)
    ug = proj(UC0, C_WIDTH) * _silu(proj(GC0, C_WIDTH))
    for c in range(T // SGU_CHUNK):
        vn_c = vn[c * SGU_CHUNK:(c + 1) * SGU_CHUNK]
        stack = jnp.where(head_sel, jnp.concatenate([vn_c] * C_HEADS, axis=0), 0.0)
        mixed_c = _dot(sguw_ref[...], stack.astype(BF16)) + sgub_ref[...]
        ybc_ref[0, c * SGU_CHUNK:(c + 1) * SGU_CHUNK, B_WIDTH:B_WIDTH + C_WIDTH] = (
            ug[c * SGU_CHUNK:(c + 1) * SGU_CHUNK] * mixed_c)


def _proj_call(rel_bias, h, norm_g, w_qt, w_vt, w_in, pool_bd, pool_scale, sgu_wcat, sgu_btile):
    B, S, _ = h.shape
    T = ROW_TILE
    nb = S // MOBA_BLOCK
    assert nb <= MASK_LO - MASK_HI and MASK_LO + nb <= LANES
    full = lambda shape: pl.BlockSpec(shape, lambda b, s: (0,) * len(shape))
    return pl.pallas_call(
        functools.partial(_proj_kernel, n_blocks=nb),
        grid=(B, S // T),
        in_specs=[
            pl.BlockSpec(memory_space=pltpu.SMEM),
            pl.BlockSpec((1, T, D_MODEL), lambda b, s: (b, s, 0)),
            full((1, D_MODEL)),
            full((A_WIDTH, D_MODEL)),
            full((A_WIDTH, D_MODEL)),
            full((D_MODEL, D_IN)),
            full((B_WIDTH, B_WIDTH)),
            full((1, B_WIDTH)),
            full((SGU_CHUNK, C_HEADS * SGU_CHUNK)),
            full((SGU_CHUNK, C_WIDTH)),
        ],
        out_specs=[
            pl.BlockSpec((1, A_HEADS, T, LANES), lambda b, s: (b, 0, s, 0)),
            pl.BlockSpec((1, A_HEADS, ROW_BLOCKS, MOBA_BLOCK, LANES), lambda b, s: (b, 0, s, 0, 0)),
            pl.BlockSpec((1, A_HEADS, ROW_BLOCKS, V_ROWS, MOBA_BLOCK), lambda b, s: (b, 0, s, 0, 0)),
            pl.BlockSpec((1, T, A_WIDTH), lambda b, s: (b, s, 0)),
            pl.BlockSpec((1, T, B_WIDTH + C_WIDTH), lambda b, s: (b, s, 0)),
        ],
        out_shape=[
            jax.ShapeDtypeStruct((B, A_HEADS, S, LANES), BF16),
            jax.ShapeDtypeStruct((B, A_HEADS, nb, MOBA_BLOCK, LANES), BF16),
            jax.ShapeDtypeStruct((B, A_HEADS, nb, V_ROWS, MOBA_BLOCK), BF16),
            jax.ShapeDtypeStruct((B, S, A_WIDTH), F32),
            jax.ShapeDtypeStruct((B, S, B_WIDTH + C_WIDTH), F32),
        ],
        scratch_shapes=[
            pltpu.VMEM((A_HEADS * nb, A_WIDTH), F32),
            pltpu.VMEM((MAX_WINDOW + T, B_WIDTH), F32),
        ],
        compiler_params=pltpu.CompilerParams(
            dimension_semantics=("arbitrary", "arbitrary"), vmem_limit_bytes=VMEM_LIMIT),
        name="proj_mix",
    )(rel_bias, h, norm_g, w_qt, w_vt, w_in, pool_bd, pool_scale, sgu_wcat, sgu_btile)


def _attn_kernel(q_ref, k_ref, vt_ref, bias_ref, o_ref, sa_scr, sb_scr, m_scr, acc_scr):
    i = pl.program_id(2)
    W = MOBA_BLOCK
    n_far = jnp.maximum(2 * i - 1, 0)
    n_pairs = jnp.maximum(n_far - 1, 0) // 2
    parked = 2 * n_pairs
    extra = parked + 1
    before = jnp.maximum(2 * i - 1, 0)
    pen_parked = jnp.where(n_far >= 1, 0.0, NEG)
    pen_extra = jnp.where(extra < n_far, 0.0, NEG)
    pen_before = jnp.where(i >= 1, 0.0, NEG)
    heads = range(ATTN_HEADS)
    OWN, PREV = 0, 1

    def park(hd, scr, j, bias_kinds=(None, None)):
        st = _dot_nt(k_ref[0, hd, j], q_ref[0, hd])
        for half, kind in enumerate(bias_kinds):
            cols = slice(half * W, (half + 1) * W)
            part = st[:, cols]
            scr[hd, :, cols] = part if kind is None else part + bias_ref[hd, kind]

    def consume(hd, scr, j, pen=None):
        for half in range(2):
            cols = slice(half * W, (half + 1) * W)
            st = scr[hd, :, cols]
            m = m_scr[hd, :, cols]
            cm = jnp.max(st, axis=0, keepdims=True)
            m_new = jnp.maximum(m, cm if pen is None else cm + pen)
            shift = m_new if pen is None else m_new - pen
            pv = _dot(vt_ref[0, hd, j], jnp.exp2(st - shift).astype(BF16))
            m_scr[hd, :, cols] = m_new
            acc_scr[hd, :, cols] = jnp.exp2(m - m_new) * acc_scr[hd, :, cols] + pv

    def stage(produce=None, take=None):
        for hd in heads:
            if produce is not None:
                park(hd, *produce)
            if take is not None:
                consume(hd, *take)

    m_scr[...] = jnp.full(m_scr.shape, -jnp.inf, F32)
    acc_scr[...] = jnp.zeros(acc_scr.shape, F32)

    stage(produce=(sa_scr, 0))

    def pair_step(t, carry):
        stage(produce=(sb_scr, 2 * t + 1), take=(sa_scr, 2 * t))
        stage(produce=(sa_scr, 2 * t + 2), take=(sb_scr, 2 * t + 1))
        return carry

    def long_step(u, carry):
        for r in range(PAIR_UNROLL):
            pair_step(u * PAIR_UNROLL + r, carry)
        return carry

    n_long = n_pairs // PAIR_UNROLL
    lax.fori_loop(0, n_long, long_step, 0)
    lax.fori_loop(n_long * PAIR_UNROLL, n_pairs, pair_step, 0)

    stage(produce=(sb_scr, extra), take=(sa_scr, parked, pen_parked))
    stage(produce=(sa_scr, before, (PREV, None)), take=(sb_scr, extra, pen_extra))
    stage(produce=(sb_scr, 2 * i, (OWN, PREV)), take=(sa_scr, before, pen_before))
    stage(produce=(sa_scr, 2 * i + 1, (None, OWN)), take=(sb_scr, 2 * i))
    stage(take=(sa_scr, 2 * i + 1))
    outs = [acc_scr[hd, 0:HEAD_DIM, :] / acc_scr[hd, HEAD_DIM:HEAD_DIM + 1, :] for hd in heads]
    o_ref[0] = jnp.concatenate(outs, axis=0).T


def _attn_call(q, k_blk, vt_blk, bias_tiles):
    B, H, S, _ = q.shape
    nb = S // MOBA_BLOCK
    Tq = ROW_TILE
    G = ATTN_HEADS
    return pl.pallas_call(
        _attn_kernel,
        grid=(B, H // G, S // Tq),
        in_specs=[
            pl.BlockSpec((1, G, Tq, LANES), lambda b, hg, i: (b, hg, i, 0)),
            pl.BlockSpec((1, G, nb, MOBA_BLOCK, LANES), lambda b, hg, i: (b, hg, 0, 0, 0)),
            pl.BlockSpec((1, G, nb, V_ROWS, MOBA_BLOCK), lambda b, hg, i: (b, hg, 0, 0, 0)),
            pl.BlockSpec((G, 2, MOBA_BLOCK, MOBA_BLOCK), lambda b, hg, i: (hg, 0, 0, 0)),
        ],
        out_specs=pl.BlockSpec((1, Tq, G * HEAD_DIM), lambda b, hg, i: (b, i, hg)),
        out_shape=jax.ShapeDtypeStruct((B, S, H * HEAD_DIM), F32),
        scratch_shapes=[
            pltpu.VMEM((G, MOBA_BLOCK, Tq), F32),
            pltpu.VMEM((G, MOBA_BLOCK, Tq), F32),
            pltpu.VMEM((G, 1, Tq), F32),
            pltpu.VMEM((G, V_ROWS, Tq), F32),
        ],
        compiler_params=pltpu.CompilerParams(
            dimension_semantics=("arbitrary", "arbitrary", "arbitrary"),
            vmem_limit_bytes=VMEM_LIMIT),
        name="moba_attn",
    )(q, k_blk, vt_blk, bias_tiles)


def _out_kernel(h_ref, ya_ref, gsa_ref, ybc_ref, p_ref, wout_ref, plew_ref, gatew_ref,
                fg_ref, o_ref, *, final):
    ya = (ya_ref[...] * gsa_ref[...]).astype(BF16)
    y = _dot(ya, wout_ref[0:A_WIDTH, :]) + _dot(ybc_ref[...].astype(BF16), wout_ref[A_WIDTH:, :])
    h1 = h_ref[...] + y
    emb = _dot(p_ref[...].astype(BF16), plew_ref[...])
    gate = _sigmoid(_dot(h1.astype(BF16), gatew_ref[...]))
    h2 = h1 + emb * gate
    if final:
        h2 = h2 * lax.rsqrt(jnp.mean(h2 * h2, axis=-1, keepdims=True) + EPS) * fg_ref[...]
    o_ref[...] = h2


def _out_call(h, ya, gsa, ybc, p, w_out, ple_w, gate_w, final_g, final):
    N = h.shape[0]
    T = OUT_TILE
    row = lambda width: pl.BlockSpec((T, width), lambda i: (i, 0))
    full = lambda shape: pl.BlockSpec(shape, lambda i: (0,) * len(shape))
    return pl.pallas_call(
        functools.partial(_out_kernel, final=final),
        grid=(N // T,),
        in_specs=[
            row(D_MODEL), row(A_WIDTH), row(A_WIDTH), row(B_WIDTH + C_WIDTH), row(D_PLE),
            full((D_MODEL, D_MODEL)), full((D_PLE, D_MODEL)), full((D_MODEL, D_MODEL)),
            full((1, D_MODEL)),
        ],
        out_specs=row(D_MODEL),
        out_shape=jax.ShapeDtypeStruct((N, D_MODEL), F32),
        compiler_params=pltpu.CompilerParams(
            dimension_semantics=("arbitrary",), vmem_limit_bytes=VMEM_LIMIT),
        name="out_ple",
    )(h, ya, gsa, ybc, p, w_out, ple_w, gate_w, final_g)


def _block_diag(w):
    G, c, d = w.shape
    eye = jnp.eye(G, dtype=w.dtype)
    return (eye[:, None, :, None] * w[:, :, None, :]).reshape(G * c, G * d)


def kernel(x, p, norm_g, w_in, w_out, rel_bias, pool_w, pool_scale, sgu_w, sgu_b,
           ple_w, ple_gate_w, final_g):
    B, S, D = x.shape
    depth = w_in.shape[0]
    N = B * S
    bias_tiles = _bias_tiles(rel_bias)
    h = x
    for i in range(depth):
        w_in_b = w_in[i].astype(BF16)
        sgu_wcat = jnp.transpose(jnp.tril(sgu_w[i]), (1, 0, 2)).reshape(
            SGU_CHUNK, C_HEADS * SGU_CHUNK).astype(BF16)
        sgu_btile = jnp.repeat(jnp.transpose(sgu_b[i]), HEAD_DIM, axis=1)
        q, k_blk, vt_blk, gsa, ybc = _proj_call(
            rel_bias, h, norm_g[i][None, :],
            jnp.transpose(w_in[i][:, Q0:Q0 + A_WIDTH] * Q_SCALE).astype(BF16),
            jnp.transpose(w_in_b[:, V0:V0 + A_WIDTH]),
            w_in_b, _block_diag(pool_w[i]).astype(BF16), pool_scale[i][None, :],
            sgu_wcat, sgu_btile)
        ya = _attn_call(q, k_blk, vt_blk, bias_tiles)
        h = _out_call(
            h.reshape(N, D), ya.reshape(N, A_WIDTH), gsa.reshape(N, A_WIDTH),
            ybc.reshape(N, B_WIDTH + C_WIDTH), p[i].reshape(N, D_PLE),
            w_out[i].astype(BF16), ple_w[i].astype(BF16), ple_gate_w[i].astype(BF16),
            final_g[None, :], final=(i == depth - 1)).reshape(B, S, D)
    return h
```

```python
import functools
import math

import numpy as np
import jax
import jax.numpy as jnp
from jax import lax
from jax.experimental import pallas as pl
from jax.experimental.pallas import tpu as pltpu

D_MODEL = 1024
HEAD_DIM = 64
A_WIDTH = 512
A_HEADS = 8
MOBA_BLOCK = 256
MOBA_TOPK = 3
REL_BUCKETS = 32
REL_MAX_DIST = 128
B_WIDTH = 256
POOL_WINDOWS = (2, 4, 8, 16)
B_GROUP = 64
C_WIDTH = 256
C_HEADS = 4
SGU_CHUNK = 128
D_PLE = 256
D_IN = 3328
EPS = 1e-6
NEG = -1e30
LOG2E = math.log2(math.e)
Q_SCALE = HEAD_DIM ** -0.5 * LOG2E

MAX_WINDOW = max(POOL_WINDOWS)
ROW_BLOCKS = 2
ROW_TILE = ROW_BLOCKS * MOBA_BLOCK
OUT_TILE = 1024
LANES = 128
MASK_HI = HEAD_DIM
MASK_LO = HEAD_DIM + 32
V_ROWS = HEAD_DIM + 16
ATTN_HEADS = 4
PAIR_UNROLL = 2
VMEM_LIMIT = 48 * 1024 * 1024

_OFF = np.cumsum((0,) + (A_WIDTH,) * 4 + (B_WIDTH,) * 2 + (C_WIDTH,) * 3)
Q0, K0, V0, GA0, XB0, GB0, UC0, VC0, GC0, _ = (int(o) for o in _OFF)

BF16 = jnp.bfloat16
F32 = jnp.float32


def _bucket_thresholds():
    max_exact = REL_BUCKETS // 2
    n = np.arange(0, 4 * MOBA_BLOCK)
    nf = np.maximum(n, 1).astype(np.float64)
    large = max_exact + (np.log(nf / max_exact) / math.log(REL_MAX_DIST / max_exact)
                         * (REL_BUCKETS - max_exact)).astype(np.int64)
    large = np.minimum(large, REL_BUCKETS - 1)
    bucket = np.where(n < max_exact, n, large)
    return [int(np.argmax(bucket >= b)) for b in range(1, REL_BUCKETS)]


_THRESHOLDS = _bucket_thresholds()


def _sigmoid(x):
    return 0.5 * jnp.tanh(0.5 * x) + 0.5


def _silu(x):
    return x * _sigmoid(x)


def _split_bf16(x):
    hi = x.astype(BF16)
    lo = (x - hi.astype(F32)).astype(BF16)
    return hi, lo


def _dot(a, b):
    return jnp.dot(a, b, preferred_element_type=F32)


def _dot_nt(a, b):
    return lax.dot_general(a, b, (((1,), (1,)), ((), ())), preferred_element_type=F32)


def _bias_kernel(rb_ref, out_ref):
    h = pl.program_id(0)
    key = lax.broadcasted_iota(jnp.int32, (MOBA_BLOCK, MOBA_BLOCK), 0)
    qry = lax.broadcasted_iota(jnp.int32, (MOBA_BLOCK, MOBA_BLOCK), 1)
    for kind, shift in ((0, 0), (1, MOBA_BLOCK)):
        dist = qry - key + shift
        val = jnp.full((MOBA_BLOCK, MOBA_BLOCK), rb_ref[0, h], F32)
        for b in range(1, REL_BUCKETS):
            val = jnp.where(dist >= _THRESHOLDS[b - 1], rb_ref[b, h], val)
        val = val * LOG2E
        if kind == 0:
            val = jnp.where(dist >= 0, val, NEG)
        out_ref[0, kind] = val


def _bias_tiles(rel_bias):
    return pl.pallas_call(
        _bias_kernel,
        grid=(A_HEADS,),
        in_specs=[pl.BlockSpec(memory_space=pltpu.SMEM)],
        out_specs=pl.BlockSpec((1, 2, MOBA_BLOCK, MOBA_BLOCK), lambda h: (h, 0, 0, 0)),
        out_shape=jax.ShapeDtypeStruct((A_HEADS, 2, MOBA_BLOCK, MOBA_BLOCK), F32),
        name="bias_tiles",
    )(rel_bias)


def _proj_kernel(rb_ref, h_ref, ng_ref, wqt_ref, wvt_ref, win_ref, poolw_ref, pscale_ref,
                 sguw_ref, sgub_ref,
                 q_ref, k_ref, vt_ref, gsa_ref, ybc_ref,
                 kmt_scr, ext_scr, *, n_blocks):
    T = ROW_TILE
    W = MOBA_BLOCK
    s = pl.program_id(1)
    first_block = s * ROW_BLOCKS

    @pl.when(s == 0)
    def _():
        kmt_scr[...] = jnp.zeros_like(kmt_scr)
        ext_scr[0:MAX_WINDOW, :] = jnp.zeros((MAX_WINDOW, B_WIDTH), F32)

    h = h_ref[0]
    hn = h * lax.rsqrt(jnp.mean(h * h, axis=-1, keepdims=True) + EPS) * ng_ref[...]
    hb = hn.astype(BF16)

    def proj(c0, width):
        return _dot(hb, win_ref[:, c0:c0 + width])

    zk = proj(K0, A_WIDTH)
    lane = lax.broadcasted_iota(jnp.int32, (1, LANES), 1)
    lane_a = lax.broadcasted_iota(jnp.int32, (1, A_WIDTH), 1)
    for blk in range(ROW_BLOCKS):
        j = first_block + blk
        zk_blk = zk[blk * W:(blk + 1) * W]
        indicator = jnp.where((lane == MASK_HI + j) | (lane == MASK_LO + j), 1.0, 0.0)
        for pair in range(A_HEADS // 2):
            zk_pair = zk_blk[:, pair * LANES:(pair + 1) * LANES]
            k_ref[0, 2 * pair, blk] = jnp.where(lane < HEAD_DIM, zk_pair, indicator).astype(BF16)
            k_ref[0, 2 * pair + 1, blk] = jnp.where(
                lane < HEAD_DIM, pltpu.roll(zk_pair, HEAD_DIM, 1), indicator).astype(BF16)
        k_mean = jnp.mean(zk_blk, axis=0, keepdims=True)
        for hd in range(A_HEADS):
            in_head = (lane_a >= hd * HEAD_DIM) & (lane_a < (hd + 1) * HEAD_DIM)
            kmt_scr[pl.ds(hd * n_blocks + j, 1), :] = jnp.where(in_head, k_mean, 0.0)

    zqt = _dot_nt(wqt_ref[...], hb)

    km_hi, km_lo = _split_bf16(kmt_scr[...])
    q_hi, q_lo = _split_bf16(zqt)
    gate_t = _dot(km_hi, q_hi) + _dot(km_hi, q_lo) + _dot(km_lo, q_hi)
    jrow = lax.broadcasted_iota(jnp.int32, (n_blocks, T), 0)
    own = first_block + lax.broadcasted_iota(jnp.int32, (1, T), 1) // W
    for hd in range(A_HEADS):
        g = jnp.where(jrow < own, gate_t[hd * n_blocks:(hd + 1) * n_blocks], NEG)
        sel = jnp.zeros((n_blocks, T), jnp.bool_)
        for _ in range(MOBA_TOPK):
            m = jnp.max(g, axis=0, keepdims=True)
            idx = jnp.min(jnp.where(g == m, jrow, n_blocks), axis=0, keepdims=True)
            pick = jrow == idx
            sel = sel | (pick & (m > NEG * 0.5))
            g = jnp.where(pick, -jnp.inf, g)
        far_bias = rb_ref[REL_BUCKETS - 1, hd] * LOG2E
        term = jnp.where(sel, jnp.where(jrow <= own - 2, far_bias, 0.0), NEG)
        term = jnp.where(jrow == own, 0.0, term)
        term_hi = term.astype(BF16).astype(F32)
        q_aug_t = jnp.concatenate(
            [zqt[hd * HEAD_DIM:(hd + 1) * HEAD_DIM], term_hi, term - term_hi], axis=0)
        q_ref[0, hd] = q_aug_t.T.astype(BF16)

    zvt = _dot_nt(wvt_ref[...], hb)
    tail_row = lax.broadcasted_iota(jnp.int32, (V_ROWS - HEAD_DIM, W), 0)
    tail = jnp.where(tail_row == 0, 1.0, 0.0)
    for hd in range(A_HEADS):
        for blk in range(ROW_BLOCKS):
            vt_ref[0, hd, blk] = jnp.concatenate(
                [zvt[hd * HEAD_DIM:(hd + 1) * HEAD_DIM, blk * W:(blk + 1) * W], tail],
                axis=0).astype(BF16)

    gsa_ref[0] = _silu(proj(GA0, A_WIDTH))

    xb = proj(XB0, B_WIDTH)
    ext_scr[MAX_WINDOW:MAX_WINDOW + T, :] = xb
    lane_b = lax.broadcasted_iota(jnp.int32, (1, B_WIDTH), 1)
    win = jnp.zeros((1, B_WIDTH), F32)
    for gi, w in enumerate(POOL_WINDOWS):
        win = jnp.where(lane_b // B_GROUP == gi, float(w), win)
    halves = []
    for half, (w_small, w_big) in enumerate(zip(POOL_WINDOWS[0::2], POOL_WINDOWS[1::2])):
        cols = slice(half * LANES, (half + 1) * LANES)
        run = xb[:, cols]
        for lag in range(1, w_small):
            run = run + ext_scr[MAX_WINDOW - lag:MAX_WINDOW - lag + T, cols]
        small = run
        for lag in range(w_small, w_big):
            run = run + ext_scr[MAX_WINDOW - lag:MAX_WINDOW - lag + T, cols]
        halves.append(jnp.where(lane < B_GROUP, small, run))
    wsum = jnp.concatenate(halves, axis=1)
    pos = (s * T + lax.broadcasted_iota(jnp.int32, (T, 1), 0) + 1).astype(F32)
    pooled = wsum / jnp.minimum(pos, win) - xb
    ext_scr[0:MAX_WINDOW, :] = xb[T - MAX_WINDOW:T, :]
    mixed_b = _dot(pooled.astype(BF16), poolw_ref[...]) * pscale_ref[...]
    ybc_ref[0, :, 0:B_WIDTH] = (mixed_b * _silu(proj(GB0, B_WIDTH))).astype(BF16)

    vc = proj(VC0, C_WIDTH)
    mu = jnp.mean(vc, axis=-1, keepdims=True)
    cen = vc - mu
    var = jnp.mean(cen * cen, axis=-1, keepdims=True)
    vn = cen * lax.rsqrt(var + EPS)
    rows = lax.broadcasted_iota(jnp.int32, (C_HEADS * SGU_CHUNK, C_WIDTH), 0)
    cols = lax.broadcasted_iota(jnp.int32, (C_HEADS * SGU_CHUNK, C_WIDTH), 1)
    head_sel = (rows // SGU_CHUNK) == (cols // HEAD_DIM)
    ug = proj(UC0, C_WIDTH) * _silu(proj(GC0, C_WIDTH))
    for c in range(T // SGU_CHUNK):
        vn_c = vn[c * SGU_CHUNK:(c + 1) * SGU_CHUNK]
        stack = jnp.where(head_sel, jnp.concatenate([vn_c] * C_HEADS, axis=0), 0.0)
        mixed_c = _dot(sguw_ref[...], stack.astype(BF16)) + sgub_ref[...]
        ybc_ref[0, c * SGU_CHUNK:(c + 1) * SGU_CHUNK, B_WIDTH:B_WIDTH + C_WIDTH] = (
            ug[c * SGU_CHUNK:(c + 1) * SGU_CHUNK] * mixed_c).astype(BF16)


def _proj_call(rel_bias, h, norm_g, w_qt, w_vt, w_in, pool_bd, pool_scale, sgu_wcat, sgu_btile):
    B, S, _ = h.shape
    T = ROW_TILE
    nb = S // MOBA_BLOCK
    assert nb <= MASK_LO - MASK_HI and MASK_LO + nb <= LANES
    full = lambda shape: pl.BlockSpec(shape, lambda b, s: (0,) * len(shape))
    return pl.pallas_call(
        functools.partial(_proj_kernel, n_blocks=nb),
        grid=(B, S // T),
        in_specs=[
            pl.BlockSpec(memory_space=pltpu.SMEM),
            pl.BlockSpec((1, T, D_MODEL), lambda b, s: (b, s, 0)),
            full((1, D_MODEL)),
            full((A_WIDTH, D_MODEL)),
            full((A_WIDTH, D_MODEL)),
            full((D_MODEL, D_IN)),
            full((B_WIDTH, B_WIDTH)),
            full((1, B_WIDTH)),
            full((SGU_CHUNK, C_HEADS * SGU_CHUNK)),
            full((SGU_CHUNK, C_WIDTH)),
        ],
        out_specs=[
            pl.BlockSpec((1, A_HEADS, T, LANES), lambda b, s: (b, 0, s, 0)),
            pl.BlockSpec((1, A_HEADS, ROW_BLOCKS, MOBA_BLOCK, LANES), lambda b, s: (b, 0, s, 0, 0)),
            pl.BlockSpec((1, A_HEADS, ROW_BLOCKS, V_ROWS, MOBA_BLOCK), lambda b, s: (b, 0, s, 0, 0)),
            pl.BlockSpec((1, T, A_WIDTH), lambda b, s: (b, s, 0)),
            pl.BlockSpec((1, T, B_WIDTH + C_WIDTH), lambda b, s: (b, s, 0)),
        ],
        out_shape=[
            jax.ShapeDtypeStruct((B, A_HEADS, S, LANES), BF16),
            jax.ShapeDtypeStruct((B, A_HEADS, nb, MOBA_BLOCK, LANES), BF16),
            jax.ShapeDtypeStruct((B, A_HEADS, nb, V_ROWS, MOBA_BLOCK), BF16),
            jax.ShapeDtypeStruct((B, S, A_WIDTH), F32),
            jax.ShapeDtypeStruct((B, S, B_WIDTH + C_WIDTH), BF16),
        ],
        scratch_shapes=[
            pltpu.VMEM((A_HEADS * nb, A_WIDTH), F32),
            pltpu.VMEM((MAX_WINDOW + T, B_WIDTH), F32),
        ],
        compiler_params=pltpu.CompilerParams(
            dimension_semantics=("arbitrary", "arbitrary"), vmem_limit_bytes=VMEM_LIMIT),
        name="proj_mix",
    )(rel_bias, h, norm_g, w_qt, w_vt, w_in, pool_bd, pool_scale, sgu_wcat, sgu_btile)


def _attn_kernel(q_ref, k_ref, vt_ref, bias_ref, gsa_ref, o_ref, sa_scr, sb_scr, m_scr, acc_scr):
    i = pl.program_id(2)
    W = MOBA_BLOCK
    n_far = jnp.maximum(2 * i - 1, 0)
    n_pairs = jnp.maximum(n_far - 1, 0) // 2
    parked = 2 * n_pairs
    extra = parked + 1
    before = jnp.maximum(2 * i - 1, 0)
    pen_parked = jnp.where(n_far >= 1, 0.0, NEG)
    pen_extra = jnp.where(extra < n_far, 0.0, NEG)
    pen_before = jnp.where(i >= 1, 0.0, NEG)
    heads = range(ATTN_HEADS)
    OWN, PREV = 0, 1

    def park(hd, scr, j, bias_kinds=(None, None)):
        st = _dot_nt(k_ref[0, hd, j], q_ref[0, hd])
        for half, kind in enumerate(bias_kinds):
            cols = slice(half * W, (half + 1) * W)
            part = st[:, cols]
            scr[hd, :, cols] = part if kind is None else part + bias_ref[hd, kind]

    def consume(hd, scr, j, pen=None):
        for half in range(2):
            cols = slice(half * W, (half + 1) * W)
            st = scr[hd, :, cols]
            m = m_scr[hd, :, cols]
            cm = jnp.max(st, axis=0, keepdims=True)
            m_new = jnp.maximum(m, cm if pen is None else cm + pen)
            shift = m_new if pen is None else m_new - pen
            pv = _dot(vt_ref[0, hd, j], jnp.exp2(st - shift).astype(BF16))
            m_scr[hd, :, cols] = m_new
            acc_scr[hd, :, cols] = jnp.exp2(m - m_new) * acc_scr[hd, :, cols] + pv

    def stage(produce=None, take=None):
        for hd in heads:
            if produce is not None:
                park(hd, *produce)
            if take is not None:
                consume(hd, *take)

    m_scr[...] = jnp.full(m_scr.shape, -jnp.inf, F32)
    acc_scr[...] = jnp.zeros(acc_scr.shape, F32)

    stage(produce=(sa_scr, 0))

    def pair_step(t, carry):
        stage(produce=(sb_scr, 2 * t + 1), take=(sa_scr, 2 * t))
        stage(produce=(sa_scr, 2 * t + 2), take=(sb_scr, 2 * t + 1))
        return carry

    def long_step(u, carry):
        for r in range(PAIR_UNROLL):
            pair_step(u * PAIR_UNROLL + r, carry)
        return carry

    n_long = n_pairs // PAIR_UNROLL
    lax.fori_loop(0, n_long, long_step, 0)
    lax.fori_loop(n_long * PAIR_UNROLL, n_pairs, pair_step, 0)

    stage(produce=(sb_scr, extra), take=(sa_scr, parked, pen_parked))
    stage(produce=(sa_scr, before, (PREV, None)), take=(sb_scr, extra, pen_extra))
    stage(produce=(sb_scr, 2 * i, (OWN, PREV)), take=(sa_scr, before, pen_before))
    stage(produce=(sa_scr, 2 * i + 1, (None, OWN)), take=(sb_scr, 2 * i))
    stage(take=(sa_scr, 2 * i + 1))
    outs = [acc_scr[hd, 0:HEAD_DIM, :] / acc_scr[hd, HEAD_DIM:HEAD_DIM + 1, :] for hd in heads]
    o_ref[0] = (jnp.concatenate(outs, axis=0).T * gsa_ref[0]).astype(BF16)


def _attn_call(q, k_blk, vt_blk, bias_tiles, gsa):
    B, H, S, _ = q.shape
    nb = S // MOBA_BLOCK
    Tq = ROW_TILE
    G = ATTN_HEADS
    return pl.pallas_call(
        _attn_kernel,
        grid=(B, H // G, S // Tq),
        in_specs=[
            pl.BlockSpec((1, G, Tq, LANES), lambda b, hg, i: (b, hg, i, 0)),
            pl.BlockSpec((1, G, nb, MOBA_BLOCK, LANES), lambda b, hg, i: (b, hg, 0, 0, 0)),
            pl.BlockSpec((1, G, nb, V_ROWS, MOBA_BLOCK), lambda b, hg, i: (b, hg, 0, 0, 0)),
            pl.BlockSpec((G, 2, MOBA_BLOCK, MOBA_BLOCK), lambda b, hg, i: (hg, 0, 0, 0)),
            pl.BlockSpec((1, Tq, G * HEAD_DIM), lambda b, hg, i: (b, i, hg)),
        ],
        out_specs=pl.BlockSpec((1, Tq, G * HEAD_DIM), lambda b, hg, i: (b, i, hg)),
        out_shape=jax.ShapeDtypeStruct((B, S, H * HEAD_DIM), BF16),
        scratch_shapes=[
            pltpu.VMEM((G, MOBA_BLOCK, Tq), F32),
            pltpu.VMEM((G, MOBA_BLOCK, Tq), F32),
            pltpu.VMEM((G, 1, Tq), F32),
            pltpu.VMEM((G, V_ROWS, Tq), F32),
        ],
        compiler_params=pltpu.CompilerParams(
            dimension_semantics=("arbitrary", "arbitrary", "arbitrary"),
            vmem_limit_bytes=VMEM_LIMIT),
        name="moba_attn",
    )(q, k_blk, vt_blk, bias_tiles, gsa)


def _out_kernel(h_ref, ya_ref, ybc_ref, p_ref, wout_ref, plew_ref, gatew_ref,
                fg_ref, o_ref, *, final):
    y = _dot(ya_ref[...], wout_ref[0:A_WIDTH, :]) + _dot(ybc_ref[...], wout_ref[A_WIDTH:, :])
    h1 = h_ref[...] + y
    emb = _dot(p_ref[...].astype(BF16), plew_ref[...])
    gate = _sigmoid(_dot(h1.astype(BF16), gatew_ref[...]))
    h2 = h1 + emb * gate
    if final:
        h2 = h2 * lax.rsqrt(jnp.mean(h2 * h2, axis=-1, keepdims=True) + EPS) * fg_ref[...]
    o_ref[...] = h2


def _out_call(h, ya, ybc, p, w_out, ple_w, gate_w, final_g, final):
    N = h.shape[0]
    T = OUT_TILE
    row = lambda width: pl.BlockSpec((T, width), lambda i: (i, 0))
    full = lambda shape: pl.BlockSpec(shape, lambda i: (0,) * len(shape))
    return pl.pallas_call(
        functools.partial(_out_kernel, final=final),
        grid=(N // T,),
        in_specs=[
            row(D_MODEL), row(A_WIDTH), row(B_WIDTH + C_WIDTH), row(D_PLE),
            full((D_MODEL, D_MODEL)), full((D_PLE, D_MODEL)), full((D_MODEL, D_MODEL)),
            full((1, D_MODEL)),
        ],
        out_specs=row(D_MODEL),
        out_shape=jax.ShapeDtypeStruct((N, D_MODEL), F32),
        compiler_params=pltpu.CompilerParams(
            dimension_semantics=("arbitrary",), vmem_limit_bytes=VMEM_LIMIT),
        name="out_ple",
    )(h, ya, ybc, p, w_out, ple_w, gate_w, final_g)


def _block_diag(w):
    G, c, d = w.shape
    eye = jnp.eye(G, dtype=w.dtype)
    return (eye[:, None, :, None] * w[:, :, None, :]).reshape(G * c, G * d)


def kernel(x, p, norm_g, w_in, w_out, rel_bias, pool_w, pool_scale, sgu_w, sgu_b,
           ple_w, ple_gate_w, final_g):
    B, S, D = x.shape
    depth = w_in.shape[0]
    N = B * S
    bias_tiles = _bias_tiles(rel_bias)
    h = x
    for i in range(depth):
        w_in_b = w_in[i].astype(BF16)
        sgu_wcat = jnp.transpose(jnp.tril(sgu_w[i]), (1, 0, 2)).reshape(
            SGU_CHUNK, C_HEADS * SGU_CHUNK).astype(BF16)
        sgu_btile = jnp.repeat(jnp.transpose(sgu_b[i]), HEAD_DIM, axis=1)
        q, k_blk, vt_blk, gsa, ybc = _proj_call(
            rel_bias, h, norm_g[i][None, :],
            jnp.transpose(w_in[i][:, Q0:Q0 + A_WIDTH] * Q_SCALE).astype(BF16),
            jnp.transpose(w_in_b[:, V0:V0 + A_WIDTH]),
            w_in_b, _block_diag(pool_w[i]).astype(BF16), pool_scale[i][None, :],
            sgu_wcat, sgu_btile)
        ya = _attn_call(q, k_blk, vt_blk, bias_tiles, gsa)
        h = _out_call(
            h.reshape(N, D), ya.reshape(N, A_WIDTH),
            ybc.reshape(N, B_WIDTH + C_WIDTH), p[i].reshape(N, D_PLE),
            w_out[i].astype(BF16), ple_w[i].astype(BF16), ple_gate_w[i].astype(BF16),
            final_g[None, :], final=(i == depth - 1)).reshape(B, S, D)
    return h
```

```python
import functools
import math

import numpy as np
import jax
import jax.numpy as jnp
from jax import lax
from jax.experimental import pallas as pl
from jax.experimental.pallas import tpu as pltpu

D_MODEL = 1024
HEAD_DIM = 64
A_WIDTH = 512
A_HEADS = 8
MOBA_BLOCK = 256
MOBA_TOPK = 3
REL_BUCKETS = 32
REL_MAX_DIST = 128
B_WIDTH = 256
POOL_WINDOWS = (2, 4, 8, 16)
B_GROUP = 64
C_WIDTH = 256
C_HEADS = 4
SGU_CHUNK = 128
D_PLE = 256
D_IN = 3328
EPS = 1e-6
NEG = -1e30
LOG2E = math.log2(math.e)
Q_SCALE = HEAD_DIM ** -0.5 * LOG2E

MAX_WINDOW = max(POOL_WINDOWS)
ROW_BLOCKS = 2
ROW_TILE = ROW_BLOCKS * MOBA_BLOCK
OUT_TILE = 1024
LANES = 128
MASK_HI = HEAD_DIM
MASK_LO = HEAD_DIM + 32
V_ROWS = HEAD_DIM + 16
ATTN_HEADS = 4
PAIR_UNROLL = 2
VMEM_LIMIT = 48 * 1024 * 1024

_OFF = np.cumsum((0,) + (A_WIDTH,) * 4 + (B_WIDTH,) * 2 + (C_WIDTH,) * 3)
Q0, K0, V0, GA0, XB0, GB0, UC0, VC0, GC0, _ = (int(o) for o in _OFF)

BF16 = jnp.bfloat16
F32 = jnp.float32


def _bucket_thresholds():
    max_exact = REL_BUCKETS // 2
    n = np.arange(0, 4 * MOBA_BLOCK)
    nf = np.maximum(n, 1).astype(np.float64)
    large = max_exact + (np.log(nf / max_exact) / math.log(REL_MAX_DIST / max_exact)
                         * (REL_BUCKETS - max_exact)).astype(np.int64)
    large = np.minimum(large, REL_BUCKETS - 1)
    bucket = np.where(n < max_exact, n, large)
    return [int(np.argmax(bucket >= b)) for b in range(1, REL_BUCKETS)]


_THRESHOLDS = _bucket_thresholds()


def _sigmoid(x):
    return 0.5 * jnp.tanh(0.5 * x) + 0.5


def _silu(x):
    return x * _sigmoid(x)


def _split_bf16(x):
    hi = x.astype(BF16)
    lo = (x - hi.astype(F32)).astype(BF16)
    return hi, lo


def _dot(a, b):
    return jnp.dot(a, b, preferred_element_type=F32)


def _dot_nt(a, b):
    return lax.dot_general(a, b, (((1,), (1,)), ((), ())), preferred_element_type=F32)


def _bias_kernel(rb_ref, out_ref):
    h = pl.program_id(0)
    key = lax.broadcasted_iota(jnp.int32, (MOBA_BLOCK, MOBA_BLOCK), 0)
    qry = lax.broadcasted_iota(jnp.int32, (MOBA_BLOCK, MOBA_BLOCK), 1)
    for kind, shift in ((0, 0), (1, MOBA_BLOCK)):
        dist = qry - key + shift
        val = jnp.full((MOBA_BLOCK, MOBA_BLOCK), rb_ref[0, h], F32)
        for b in range(1, REL_BUCKETS):
            val = jnp.where(dist >= _THRESHOLDS[b - 1], rb_ref[b, h], val)
        val = val * LOG2E
        if kind == 0:
            val = jnp.where(dist >= 0, val, NEG)
        out_ref[0, kind] = val


def _bias_tiles(rel_bias):
    return pl.pallas_call(
        _bias_kernel,
        grid=(A_HEADS,),
        in_specs=[pl.BlockSpec(memory_space=pltpu.SMEM)],
        out_specs=pl.BlockSpec((1, 2, MOBA_BLOCK, MOBA_BLOCK), lambda h: (h, 0, 0, 0)),
        out_shape=jax.ShapeDtypeStruct((A_HEADS, 2, MOBA_BLOCK, MOBA_BLOCK), F32),
        name="bias_tiles",
    )(rel_bias)


def _proj_kernel(rb_ref, h_ref, ng_ref, wqt_ref, wvt_ref, win_ref, poolw_ref, pscale_ref,
                 sguw_ref, sgub_ref,
                 q_ref, k_ref, vt_ref, gsa_ref, ybc_ref,
                 kmt_scr, ext_scr, *, n_blocks):
    T = ROW_TILE
    W = MOBA_BLOCK
    s = pl.program_id(1)
    first_block = s * ROW_BLOCKS

    @pl.when(s == 0)
    def _():
        kmt_scr[...] = jnp.zeros_like(kmt_scr)
        ext_scr[0:MAX_WINDOW, :] = jnp.zeros((MAX_WINDOW, B_WIDTH), F32)

    h = h_ref[0]
    hn = h * lax.rsqrt(jnp.mean(h * h, axis=-1, keepdims=True) + EPS) * ng_ref[...]
    hb = hn.astype(BF16)

    def proj(c0, width):
        return _dot(hb, win_ref[:, c0:c0 + width])

    zk = proj(K0, A_WIDTH)
    lane = lax.broadcasted_iota(jnp.int32, (1, LANES), 1)
    lane_a = lax.broadcasted_iota(jnp.int32, (1, A_WIDTH), 1)
    for blk in range(ROW_BLOCKS):
        j = first_block + blk
        zk_blk = zk[blk * W:(blk + 1) * W]
        indicator = jnp.where((lane == MASK_HI + j) | (lane == MASK_LO + j), 1.0, 0.0)
        for pair in range(A_HEADS // 2):
            zk_pair = zk_blk[:, pair * LANES:(pair + 1) * LANES]
            k_ref[0, 2 * pair, blk] = jnp.where(lane < HEAD_DIM, zk_pair, indicator).astype(BF16)
            k_ref[0, 2 * pair + 1, blk] = jnp.where(
                lane < HEAD_DIM, pltpu.roll(zk_pair, HEAD_DIM, 1), indicator).astype(BF16)
        k_mean = jnp.mean(zk_blk, axis=0, keepdims=True)
        for hd in range(A_HEADS):
            in_head = (lane_a >= hd * HEAD_DIM) & (lane_a < (hd + 1) * HEAD_DIM)
            kmt_scr[pl.ds(hd * n_blocks + j, 1), :] = jnp.where(in_head, k_mean, 0.0)

    zqt = _dot_nt(wqt_ref[...], hb)

    km_hi, km_lo = _split_bf16(kmt_scr[...])
    q_hi, q_lo = _split_bf16(zqt)
    gate_t = _dot(km_hi, q_hi) + _dot(km_hi, q_lo) + _dot(km_lo, q_hi)
    jrow = lax.broadcasted_iota(jnp.int32, (n_blocks, T), 0)
    own = first_block + lax.broadcasted_iota(jnp.int32, (1, T), 1) // W
    for hd in range(A_HEADS):
        g = jnp.where(jrow < own, gate_t[hd * n_blocks:(hd + 1) * n_blocks], NEG)
        sel = jnp.zeros((n_blocks, T), jnp.bool_)
        for _ in range(MOBA_TOPK):
            m = jnp.max(g, axis=0, keepdims=True)
            idx = jnp.min(jnp.where(g == m, jrow, n_blocks), axis=0, keepdims=True)
            pick = jrow == idx
            sel = sel | (pick & (m > NEG * 0.5))
            g = jnp.where(pick, -jnp.inf, g)
        far_bias = rb_ref[REL_BUCKETS - 1, hd] * LOG2E
        term = jnp.where(sel, jnp.where(jrow <= own - 2, far_bias, 0.0), NEG)
        term = jnp.where(jrow == own, 0.0, term)
        term_hi = term.astype(BF16).astype(F32)
        q_aug_t = jnp.concatenate(
            [zqt[hd * HEAD_DIM:(hd + 1) * HEAD_DIM], term_hi, term - term_hi], axis=0)
        q_ref[0, hd] = q_aug_t.T.astype(BF16)

    zvt = _dot_nt(wvt_ref[...], hb)
    tail_row = lax.broadcasted_iota(jnp.int32, (V_ROWS - HEAD_DIM, W), 0)
    tail = jnp.where(tail_row == 0, 1.0, 0.0)
    for hd in range(A_HEADS):
        for blk in range(ROW_BLOCKS):
            vt_ref[0, hd, blk] = jnp.concatenate(
                [zvt[hd * HEAD_DIM:(hd + 1) * HEAD_DIM, blk * W:(blk + 1) * W], tail],
                axis=0).astype(BF16)

    gsa_ref[0] = _silu(proj(GA0, A_WIDTH))

    xb = proj(XB0, B_WIDTH)
    ext_scr[MAX_WINDOW:MAX_WINDOW + T, :] = xb
    lane_b = lax.broadcasted_iota(jnp.int32, (1, B_WIDTH), 1)
    win = jnp.zeros((1, B_WIDTH), F32)
    for gi, w in enumerate(POOL_WINDOWS):
        win = jnp.where(lane_b // B_GROUP == gi, float(w), win)
    halves = []
    for half, (w_small, w_big) in enumerate(zip(POOL_WINDOWS[0::2], POOL_WINDOWS[1::2])):
        cols = slice(half * LANES, (half + 1) * LANES)
        run = xb[:, cols]
        for lag in range(1, w_small):
            run = run + ext_scr[MAX_WINDOW - lag:MAX_WINDOW - lag + T, cols]
        small = run
        for lag in range(w_small, w_big):
            run = run + ext_scr[MAX_WINDOW - lag:MAX_WINDOW - lag + T, cols]
        halves.append(jnp.where(lane < B_GROUP, small, run))
    wsum = jnp.concatenate(halves, axis=1)
    pos = (s * T + lax.broadcasted_iota(jnp.int32, (T, 1), 0) + 1).astype(F32)
    pooled = wsum / jnp.minimum(pos, win) - xb
    ext_scr[0:MAX_WINDOW, :] = xb[T - MAX_WINDOW:T, :]
    mixed_b = _dot(pooled.astype(BF16), poolw_ref[...]) * pscale_ref[...]
    ybc_ref[0, :, 0:B_WIDTH] = (mixed_b * _silu(proj(GB0, B_WIDTH))).astype(BF16)

    vc = proj(VC0, C_WIDTH)
    mu = jnp.mean(vc, axis=-1, keepdims=True)
    cen = vc - mu
    var = jnp.mean(cen * cen, axis=-1, keepdims=True)
    vn = cen * lax.rsqrt(var + EPS)
    rows = lax.broadcasted_iota(jnp.int32, (C_HEADS * SGU_CHUNK, C_WIDTH), 0)
    cols = lax.broadcasted_iota(jnp.int32, (C_HEADS * SGU_CHUNK, C_WIDTH), 1)
    head_sel = (rows // SGU_CHUNK) == (cols // HEAD_DIM)
    ug = proj(UC0, C_WIDTH) * _silu(proj(GC0, C_WIDTH))
    for c in range(T // SGU_CHUNK):
        vn_c = vn[c * SGU_CHUNK:(c + 1) * SGU_CHUNK]
        stack = jnp.where(head_sel, jnp.concatenate([vn_c] * C_HEADS, axis=0), 0.0)
        mixed_c = _dot(sguw_ref[...], stack.astype(BF16)) + sgub_ref[...]
        ybc_ref[0, c * SGU_CHUNK:(c + 1) * SGU_CHUNK, B_WIDTH:B_WIDTH + C_WIDTH] = (
            ug[c * SGU_CHUNK:(c + 1) * SGU_CHUNK] * mixed_c).astype(BF16)


def _proj_call(layer, rel_bias, h, norm_g, w_qt, w_vt, w_in, pool_bd, pool_scale, sgu_wcat,
               sgu_btile):
    B, S, _ = h.shape
    T = ROW_TILE
    nb = S // MOBA_BLOCK
    assert nb <= MASK_LO - MASK_HI and MASK_LO + nb <= LANES
    full = lambda shape: pl.BlockSpec((None,) + shape, lambda b, s: (layer,) + (0,) * len(shape))
    return pl.pallas_call(
        functools.partial(_proj_kernel, n_blocks=nb),
        grid=(B, S // T),
        in_specs=[
            pl.BlockSpec(memory_space=pltpu.SMEM),
            pl.BlockSpec((1, T, D_MODEL), lambda b, s: (b, s, 0)),
            full((1, D_MODEL)),
            full((A_WIDTH, D_MODEL)),
            full((A_WIDTH, D_MODEL)),
            full((D_MODEL, D_IN)),
            full((B_WIDTH, B_WIDTH)),
            full((1, B_WIDTH)),
            full((SGU_CHUNK, C_HEADS * SGU_CHUNK)),
            full((SGU_CHUNK, C_WIDTH)),
        ],
        out_specs=[
            pl.BlockSpec((1, A_HEADS, T, LANES), lambda b, s: (b, 0, s, 0)),
            pl.BlockSpec((1, A_HEADS, ROW_BLOCKS, MOBA_BLOCK, LANES), lambda b, s: (b, 0, s, 0, 0)),
            pl.BlockSpec((1, A_HEADS, ROW_BLOCKS, V_ROWS, MOBA_BLOCK), lambda b, s: (b, 0, s, 0, 0)),
            pl.BlockSpec((1, T, A_WIDTH), lambda b, s: (b, s, 0)),
            pl.BlockSpec((1, T, B_WIDTH + C_WIDTH), lambda b, s: (b, s, 0)),
        ],
        out_shape=[
            jax.ShapeDtypeStruct((B, A_HEADS, S, LANES), BF16),
            jax.ShapeDtypeStruct((B, A_HEADS, nb, MOBA_BLOCK, LANES), BF16),
            jax.ShapeDtypeStruct((B, A_HEADS, nb, V_ROWS, MOBA_BLOCK), BF16),
            jax.ShapeDtypeStruct((B, S, A_WIDTH), F32),
            jax.ShapeDtypeStruct((B, S, B_WIDTH + C_WIDTH), BF16),
        ],
        scratch_shapes=[
            pltpu.VMEM((A_HEADS * nb, A_WIDTH), F32),
            pltpu.VMEM((MAX_WINDOW + T, B_WIDTH), F32),
        ],
        compiler_params=pltpu.CompilerParams(
            dimension_semantics=("arbitrary", "arbitrary"), vmem_limit_bytes=VMEM_LIMIT),
        name="proj_mix",
    )(rel_bias, h, norm_g, w_qt, w_vt, w_in, pool_bd, pool_scale, sgu_wcat, sgu_btile)


def _attn_kernel(q_ref, k_ref, vt_ref, bias_ref, gsa_ref, o_ref, sa_scr, sb_scr, m_scr, acc_scr):
    i = pl.program_id(2)
    W = MOBA_BLOCK
    n_far = jnp.maximum(2 * i - 1, 0)
    n_pairs = jnp.maximum(n_far - 1, 0) // 2
    parked = 2 * n_pairs
    extra = parked + 1
    before = jnp.maximum(2 * i - 1, 0)
    pen_parked = jnp.where(n_far >= 1, 0.0, NEG)
    pen_extra = jnp.where(extra < n_far, 0.0, NEG)
    pen_before = jnp.where(i >= 1, 0.0, NEG)
    heads = range(ATTN_HEADS)
    OWN, PREV = 0, 1

    BOTH = (0, 1)

    def park(hd, scr, j, bias_kinds=(None, None), halves=BOTH):
        rows = slice(halves[0] * W, (halves[-1] + 1) * W)
        st = _dot_nt(k_ref[0, hd, j], q_ref[0, hd, rows, :])
        for n, half in enumerate(halves):
            kind = bias_kinds[half]
            part = st[:, n * W:(n + 1) * W]
            cols = slice(half * W, (half + 1) * W)
            scr[hd, :, cols] = part if kind is None else part + bias_ref[hd, kind]

    def consume(hd, scr, j, pen=None, halves=BOTH):
        for half in halves:
            cols = slice(half * W, (half + 1) * W)
            ps, ms = [], []
            for quarter in range(2):
                qcols = slice(half * W + quarter * LANES, half * W + (quarter + 1) * LANES)
                st = scr[hd, :, qcols]
                cm = jnp.max(st, axis=0, keepdims=True)
                m_new = jnp.maximum(m_scr[hd, :, qcols], cm if pen is None else cm + pen)
                shift = m_new if pen is None else m_new - pen
                ps.append(jnp.exp2(st - shift).astype(BF16))
                ms.append(m_new)
            pv = _dot(vt_ref[0, hd, j], jnp.concatenate(ps, axis=1))
            m = m_scr[hd, :, cols]
            m_new = jnp.concatenate(ms, axis=1)
            m_scr[hd, :, cols] = m_new
            acc_scr[hd, :, cols] = jnp.exp2(m - m_new) * acc_scr[hd, :, cols] + pv

    def stage(produce=None, take=None):
        for hd in heads:
            if produce is not None:
                park(hd, *produce)
            if take is not None:
                consume(hd, *take)

    m_scr[...] = jnp.full(m_scr.shape, -jnp.inf, F32)
    acc_scr[...] = jnp.zeros(acc_scr.shape, F32)

    stage(produce=(sa_scr, 0))

    def pair_step(t, carry):
        stage(produce=(sb_scr, 2 * t + 1), take=(sa_scr, 2 * t))
        stage(produce=(sa_scr, 2 * t + 2), take=(sb_scr, 2 * t + 1))
        return carry

    def long_step(u, carry):
        for r in range(PAIR_UNROLL):
            pair_step(u * PAIR_UNROLL + r, carry)
        return carry

    n_long = n_pairs // PAIR_UNROLL
    lax.fori_loop(0, n_long, long_step, 0)
    lax.fori_loop(n_long * PAIR_UNROLL, n_pairs, pair_step, 0)

    second = (1,)
    stage(produce=(sb_scr, extra), take=(sa_scr, parked, pen_parked))
    stage(produce=(sa_scr, before, (PREV, None)), take=(sb_scr, extra, pen_extra))
    stage(produce=(sb_scr, 2 * i, (OWN, PREV)), take=(sa_scr, before, pen_before))
    stage(produce=(sa_scr, 2 * i + 1, (None, OWN), second), take=(sb_scr, 2 * i))
    stage(take=(sa_scr, 2 * i + 1, None, second))
    outs = [acc_scr[hd, 0:HEAD_DIM, :] / acc_scr[hd, HEAD_DIM:HEAD_DIM + 1, :] for hd in heads]
    o_ref[0] = (jnp.concatenate(outs, axis=0).T * gsa_ref[0]).astype(BF16)


def _attn_call(q, k_blk, vt_blk, bias_tiles, gsa):
    B, H, S, _ = q.shape
    nb = S // MOBA_BLOCK
    Tq = ROW_TILE
    G = ATTN_HEADS
    return pl.pallas_call(
        _attn_kernel,
        grid=(B, H // G, S // Tq),
        in_specs=[
            pl.BlockSpec((1, G, Tq, LANES), lambda b, hg, i: (b, hg, i, 0)),
            pl.BlockSpec((1, G, nb, MOBA_BLOCK, LANES), lambda b, hg, i: (b, hg, 0, 0, 0)),
            pl.BlockSpec((1, G, nb, V_ROWS, MOBA_BLOCK), lambda b, hg, i: (b, hg, 0, 0, 0)),
            pl.BlockSpec((G, 2, MOBA_BLOCK, MOBA_BLOCK), lambda b, hg, i: (hg, 0, 0, 0)),
            pl.BlockSpec((1, Tq, G * HEAD_DIM), lambda b, hg, i: (b, i, hg)),
        ],
        out_specs=pl.BlockSpec((1, Tq, G * HEAD_DIM), lambda b, hg, i: (b, i, hg)),
        out_shape=jax.ShapeDtypeStruct((B, S, H * HEAD_DIM), BF16),
        scratch_shapes=[
            pltpu.VMEM((G, MOBA_BLOCK, Tq), F32),
            pltpu.VMEM((G, MOBA_BLOCK, Tq), F32),
            pltpu.VMEM((G, 1, Tq), F32),
            pltpu.VMEM((G, V_ROWS, Tq), F32),
        ],
        compiler_params=pltpu.CompilerParams(
            dimension_semantics=("arbitrary", "arbitrary", "arbitrary"),
            vmem_limit_bytes=VMEM_LIMIT),
        name="moba_attn",
    )(q, k_blk, vt_blk, bias_tiles, gsa)


def _out_kernel(h_ref, ya_ref, ybc_ref, p_ref, wout_ref, plew_ref, gatew_ref,
                fg_ref, o_ref, *, final):
    y = _dot(ya_ref[...], wout_ref[0:A_WIDTH, :]) + _dot(ybc_ref[...], wout_ref[A_WIDTH:, :])
    h1 = h_ref[...] + y
    emb = _dot(p_ref[...].astype(BF16), plew_ref[...])
    gate = _sigmoid(_dot(h1.astype(BF16), gatew_ref[...]))
    h2 = h1 + emb * gate
    if final:
        h2 = h2 * lax.rsqrt(jnp.mean(h2 * h2, axis=-1, keepdims=True) + EPS) * fg_ref[...]
    o_ref[...] = h2


def _out_call(layer, h, ya, ybc, p, w_out, ple_w, gate_w, final_g, final):
    N = h.shape[0]
    T = OUT_TILE
    row = lambda width: pl.BlockSpec((T, width), lambda i: (i, 0))
    full = lambda shape: pl.BlockSpec((None,) + shape, lambda i: (layer,) + (0,) * len(shape))
    return pl.pallas_call(
        functools.partial(_out_kernel, final=final),
        grid=(N // T,),
        in_specs=[
            row(D_MODEL), row(A_WIDTH), row(B_WIDTH + C_WIDTH),
            pl.BlockSpec((None, T, D_PLE), lambda i: (layer, i, 0)),
            full((D_MODEL, D_MODEL)), full((D_PLE, D_MODEL)), full((D_MODEL, D_MODEL)),
            pl.BlockSpec((1, D_MODEL), lambda i: (0, 0)),
        ],
        out_specs=row(D_MODEL),
        out_shape=jax.ShapeDtypeStruct((N, D_MODEL), F32),
        compiler_params=pltpu.CompilerParams(
            dimension_semantics=("arbitrary",), vmem_limit_bytes=VMEM_LIMIT),
        name="out_ple",
    )(h, ya, ybc, p, w_out, ple_w, gate_w, final_g)


def _block_diag(w):
    G, c, d = w.shape
    eye = jnp.eye(G, dtype=w.dtype)
    return (eye[:, None, :, None] * w[:, :, None, :]).reshape(G * c, G * d)


def kernel(x, p, norm_g, w_in, w_out, rel_bias, pool_w, pool_scale, sgu_w, sgu_b,
           ple_w, ple_gate_w, final_g):
    B, S, D = x.shape
    depth = w_in.shape[0]
    N = B * S
    bias_tiles = _bias_tiles(rel_bias)
    w_in_b = w_in.astype(BF16)
    w_qt = jnp.transpose(w_in[:, :, Q0:Q0 + A_WIDTH] * Q_SCALE, (0, 2, 1)).astype(BF16)
    w_vt = jnp.transpose(w_in_b[:, :, V0:V0 + A_WIDTH], (0, 2, 1))
    pool_bd = jax.vmap(_block_diag)(pool_w).astype(BF16)
    sgu_wcat = jnp.transpose(jnp.tril(sgu_w), (0, 2, 1, 3)).reshape(
        depth, SGU_CHUNK, C_HEADS * SGU_CHUNK).astype(BF16)
    sgu_btile = jnp.repeat(jnp.transpose(sgu_b, (0, 2, 1)), HEAD_DIM, axis=2)
    w_out_b, ple_w_b, gate_w_b = (w.astype(BF16) for w in (w_out, ple_w, ple_gate_w))
    p_rows = p.reshape(depth, N, D_PLE)
    h = x
    for i in range(depth):
        q, k_blk, vt_blk, gsa, ybc = _proj_call(
            i, rel_bias, h, norm_g[:, None, :], w_qt, w_vt, w_in_b, pool_bd,
            pool_scale[:, None, :], sgu_wcat, sgu_btile)
        ya = _attn_call(q, k_blk, vt_blk, bias_tiles, gsa)
        h = _out_call(
            i, h.reshape(N, D), ya.reshape(N, A_WIDTH), ybc.reshape(N, B_WIDTH + C_WIDTH),
            p_rows, w_out_b, ple_w_b, gate_w_b, final_g[None, :],
            final=(i == depth - 1)).reshape(B, S, D)
    return h
```

```python
import functools
import math

import numpy as np
import jax
import jax.numpy as jnp
from jax import lax
from jax.experimental import pallas as pl
from jax.experimental.pallas import tpu as pltpu

D_MODEL = 1024
HEAD_DIM = 64
A_WIDTH = 512
A_HEADS = 8
MOBA_BLOCK = 256
MOBA_TOPK = 3
REL_BUCKETS = 32
REL_MAX_DIST = 128
B_WIDTH = 256
POOL_WINDOWS = (2, 4, 8, 16)
B_GROUP = 64
C_WIDTH = 256
C_HEADS = 4
SGU_CHUNK = 128
D_PLE = 256
D_IN = 3328
EPS = 1e-6
NEG = -1e30
LOG2E = math.log2(math.e)
Q_SCALE = HEAD_DIM ** -0.5 * LOG2E

MAX_WINDOW = max(POOL_WINDOWS)
ROW_BLOCKS = 2
ROW_TILE = ROW_BLOCKS * MOBA_BLOCK
OUT_TILE = 1024
LANES = 128
MASK_HI = HEAD_DIM
MASK_LO = HEAD_DIM + 32
V_ROWS = HEAD_DIM + 16
ATTN_HEADS = 4
PAIR_UNROLL = 2
VMEM_LIMIT = 48 * 1024 * 1024

_OFF = np.cumsum((0,) + (A_WIDTH,) * 4 + (B_WIDTH,) * 2 + (C_WIDTH,) * 3)
Q0, K0, V0, GA0, XB0, GB0, UC0, VC0, GC0, _ = (int(o) for o in _OFF)

BF16 = jnp.bfloat16
F32 = jnp.float32


def _bucket_thresholds():
    max_exact = REL_BUCKETS // 2
    n = np.arange(0, 4 * MOBA_BLOCK)
    nf = np.maximum(n, 1).astype(np.float64)
    large = max_exact + (np.log(nf / max_exact) / math.log(REL_MAX_DIST / max_exact)
                         * (REL_BUCKETS - max_exact)).astype(np.int64)
    large = np.minimum(large, REL_BUCKETS - 1)
    bucket = np.where(n < max_exact, n, large)
    return [int(np.argmax(bucket >= b)) for b in range(1, REL_BUCKETS)]


_THRESHOLDS = _bucket_thresholds()


def _sigmoid(x):
    return 0.5 * jnp.tanh(0.5 * x) + 0.5


def _silu(x):
    return x * _sigmoid(x)


def _split_bf16(x):
    hi = x.astype(BF16)
    lo = (x - hi.astype(F32)).astype(BF16)
    return hi, lo


def _dot(a, b):
    return jnp.dot(a, b, preferred_element_type=F32)


def _dot_nt(a, b):
    return lax.dot_general(a, b, (((1,), (1,)), ((), ())), preferred_element_type=F32)


def _bias_kernel(rb_ref, out_ref):
    h = pl.program_id(0)
    key = lax.broadcasted_iota(jnp.int32, (MOBA_BLOCK, MOBA_BLOCK), 0)
    qry = lax.broadcasted_iota(jnp.int32, (MOBA_BLOCK, MOBA_BLOCK), 1)
    for kind, shift in ((0, 0), (1, MOBA_BLOCK)):
        dist = qry - key + shift
        val = jnp.full((MOBA_BLOCK, MOBA_BLOCK), rb_ref[0, h], F32)
        for b in range(1, REL_BUCKETS):
            val = jnp.where(dist >= _THRESHOLDS[b - 1], rb_ref[b, h], val)
        val = val * LOG2E
        if kind == 0:
            val = jnp.where(dist >= 0, val, NEG)
        out_ref[0, kind] = val


def _bias_tiles(rel_bias):
    return pl.pallas_call(
        _bias_kernel,
        grid=(A_HEADS,),
        in_specs=[pl.BlockSpec(memory_space=pltpu.SMEM)],
        out_specs=pl.BlockSpec((1, 2, MOBA_BLOCK, MOBA_BLOCK), lambda h: (h, 0, 0, 0)),
        out_shape=jax.ShapeDtypeStruct((A_HEADS, 2, MOBA_BLOCK, MOBA_BLOCK), F32),
        name="bias_tiles",
    )(rel_bias)


def _proj_kernel(rb_ref, h_ref, ng_ref, wqt_ref, wvt_ref, win_ref, poolw_ref, pscale_ref,
                 sguw_ref, sgub_ref,
                 q_ref, k_ref, vt_ref, gsa_ref, ybc_ref,
                 kmt_scr, ext_scr, *, n_blocks):
    T = ROW_TILE
    W = MOBA_BLOCK
    s = pl.program_id(1)
    first_block = s * ROW_BLOCKS

    @pl.when(s == 0)
    def _():
        kmt_scr[...] = jnp.zeros_like(kmt_scr)
        ext_scr[0:MAX_WINDOW, :] = jnp.zeros((MAX_WINDOW, B_WIDTH), F32)

    h = h_ref[0]
    hn = h * lax.rsqrt(jnp.mean(h * h, axis=-1, keepdims=True) + EPS) * ng_ref[...]
    hb = hn.astype(BF16)

    def proj(c0, width):
        return _dot(hb, win_ref[:, c0:c0 + width])

    zk = proj(K0, A_WIDTH)
    zqt = _dot_nt(wqt_ref[...], hb)

    lane = lax.broadcasted_iota(jnp.int32, (1, LANES), 1)
    lane_a = lax.broadcasted_iota(jnp.int32, (1, A_WIDTH), 1)
    for blk in range(ROW_BLOCKS):
        j = first_block + blk
        zk_blk = zk[blk * W:(blk + 1) * W]
        indicator = jnp.where((lane == MASK_HI + j) | (lane == MASK_LO + j), 1.0, 0.0)
        for pair in range(A_HEADS // 2):
            zk_pair = zk_blk[:, pair * LANES:(pair + 1) * LANES]
            k_ref[0, 2 * pair, blk] = jnp.where(lane < HEAD_DIM, zk_pair, indicator).astype(BF16)
            k_ref[0, 2 * pair + 1, blk] = jnp.where(
                lane < HEAD_DIM, pltpu.roll(zk_pair, HEAD_DIM, 1), indicator).astype(BF16)
        k_mean = jnp.mean(zk_blk, axis=0, keepdims=True)
        for hd in range(A_HEADS):
            in_head = (lane_a >= hd * HEAD_DIM) & (lane_a < (hd + 1) * HEAD_DIM)
            kmt_scr[pl.ds(hd * n_blocks + j, 1), :] = jnp.where(in_head, k_mean, 0.0)

    km_hi, km_lo = _split_bf16(kmt_scr[...])
    q_hi, q_lo = _split_bf16(zqt)
    gate_t = _dot(km_hi, q_hi) + _dot(km_hi, q_lo) + _dot(km_lo, q_hi)

    zvt = _dot_nt(wvt_ref[...], hb)
    zga = proj(GA0, A_WIDTH)
    xb = proj(XB0, B_WIDTH)
    zgb = proj(GB0, B_WIDTH)
    vc = proj(VC0, C_WIDTH)
    zuc = proj(UC0, C_WIDTH)
    zgc = proj(GC0, C_WIDTH)

    jrow = lax.broadcasted_iota(jnp.int32, (n_blocks, T), 0)
    own = first_block + lax.broadcasted_iota(jnp.int32, (1, T), 1) // W
    for hd in range(A_HEADS):
        g = jnp.where(jrow < own, gate_t[hd * n_blocks:(hd + 1) * n_blocks], NEG)
        sel = jnp.zeros((n_blocks, T), jnp.bool_)
        for _ in range(MOBA_TOPK):
            m = jnp.max(g, axis=0, keepdims=True)
            idx = jnp.min(jnp.where(g == m, jrow, n_blocks), axis=0, keepdims=True)
            pick = jrow == idx
            sel = sel | (pick & (m > NEG * 0.5))
            g = jnp.where(pick, -jnp.inf, g)
        far_bias = rb_ref[REL_BUCKETS - 1, hd] * LOG2E
        term = jnp.where(sel, jnp.where(jrow <= own - 2, far_bias, 0.0), NEG)
        term = jnp.where(jrow == own, 0.0, term)
        term_hi = term.astype(BF16).astype(F32)
        q_aug_t = jnp.concatenate(
            [zqt[hd * HEAD_DIM:(hd + 1) * HEAD_DIM], term_hi, term - term_hi], axis=0)
        q_ref[0, hd] = q_aug_t.T.astype(BF16)

    tail_row = lax.broadcasted_iota(jnp.int32, (V_ROWS - HEAD_DIM, W), 0)
    tail = jnp.where(tail_row == 0, 1.0, 0.0)
    for hd in range(A_HEADS):
        for blk in range(ROW_BLOCKS):
            vt_ref[0, hd, blk] = jnp.concatenate(
                [zvt[hd * HEAD_DIM:(hd + 1) * HEAD_DIM, blk * W:(blk + 1) * W], tail],
                axis=0).astype(BF16)

    gsa_ref[0] = _silu(zga)

    ext_scr[MAX_WINDOW:MAX_WINDOW + T, :] = xb
    lane_b = lax.broadcasted_iota(jnp.int32, (1, B_WIDTH), 1)
    win = jnp.zeros((1, B_WIDTH), F32)
    for gi, w in enumerate(POOL_WINDOWS):
        win = jnp.where(lane_b // B_GROUP == gi, float(w), win)
    halves = []
    for half, (w_small, w_big) in enumerate(zip(POOL_WINDOWS[0::2], POOL_WINDOWS[1::2])):
        cols = slice(half * LANES, (half + 1) * LANES)
        run = xb[:, cols]
        for lag in range(1, w_small):
            run = run + ext_scr[MAX_WINDOW - lag:MAX_WINDOW - lag + T, cols]
        small = run
        for lag in range(w_small, w_big):
            run = run + ext_scr[MAX_WINDOW - lag:MAX_WINDOW - lag + T, cols]
        halves.append(jnp.where(lane < B_GROUP, small, run))
    wsum = jnp.concatenate(halves, axis=1)
    pos = (s * T + lax.broadcasted_iota(jnp.int32, (T, 1), 0) + 1).astype(F32)
    pooled = wsum / jnp.minimum(pos, win) - xb
    ext_scr[0:MAX_WINDOW, :] = xb[T - MAX_WINDOW:T, :]
    mixed_b = _dot(pooled.astype(BF16), poolw_ref[...]) * pscale_ref[...]
    ybc_ref[0, :, 0:B_WIDTH] = (mixed_b * _silu(zgb)).astype(BF16)

    mu = jnp.mean(vc, axis=-1, keepdims=True)
    cen = vc - mu
    var = jnp.mean(cen * cen, axis=-1, keepdims=True)
    vn = cen * lax.rsqrt(var + EPS)
    rows = lax.broadcasted_iota(jnp.int32, (C_HEADS * SGU_CHUNK, C_WIDTH), 0)
    cols = lax.broadcasted_iota(jnp.int32, (C_HEADS * SGU_CHUNK, C_WIDTH), 1)
    head_sel = (rows // SGU_CHUNK) == (cols // HEAD_DIM)
    ug = zuc * _silu(zgc)
    for c in range(T // SGU_CHUNK):
        vn_c = vn[c * SGU_CHUNK:(c + 1) * SGU_CHUNK]
        stack = jnp.where(head_sel, jnp.concatenate([vn_c] * C_HEADS, axis=0), 0.0)
        mixed_c = _dot(sguw_ref[...], stack.astype(BF16)) + sgub_ref[...]
        ybc_ref[0, c * SGU_CHUNK:(c + 1) * SGU_CHUNK, B_WIDTH:B_WIDTH + C_WIDTH] = (
            ug[c * SGU_CHUNK:(c + 1) * SGU_CHUNK] * mixed_c).astype(BF16)


def _proj_call(layer, rel_bias, h, norm_g, w_qt, w_vt, w_in, pool_bd, pool_scale, sgu_wcat,
               sgu_btile):
    B, S, _ = h.shape
    T = ROW_TILE
    nb = S // MOBA_BLOCK
    assert nb <= MASK_LO - MASK_HI and MASK_LO + nb <= LANES
    full = lambda shape: pl.BlockSpec((None,) + shape, lambda b, s: (layer,) + (0,) * len(shape))
    return pl.pallas_call(
        functools.partial(_proj_kernel, n_blocks=nb),
        grid=(B, S // T),
        in_specs=[
            pl.BlockSpec(memory_space=pltpu.SMEM),
            pl.BlockSpec((1, T, D_MODEL), lambda b, s: (b, s, 0)),
            full((1, D_MODEL)),
            full((A_WIDTH, D_MODEL)),
            full((A_WIDTH, D_MODEL)),
            full((D_MODEL, D_IN)),
            full((B_WIDTH, B_WIDTH)),
            full((1, B_WIDTH)),
            full((SGU_CHUNK, C_HEADS * SGU_CHUNK)),
            full((SGU_CHUNK, C_WIDTH)),
        ],
        out_specs=[
            pl.BlockSpec((1, A_HEADS, T, LANES), lambda b, s: (b, 0, s, 0)),
            pl.BlockSpec((1, A_HEADS, ROW_BLOCKS, MOBA_BLOCK, LANES), lambda b, s: (b, 0, s, 0, 0)),
            pl.BlockSpec((1, A_HEADS, ROW_BLOCKS, V_ROWS, MOBA_BLOCK), lambda b, s: (b, 0, s, 0, 0)),
            pl.BlockSpec((1, T, A_WIDTH), lambda b, s: (b, s, 0)),
            pl.BlockSpec((1, T, B_WIDTH + C_WIDTH), lambda b, s: (b, s, 0)),
        ],
        out_shape=[
            jax.ShapeDtypeStruct((B, A_HEADS, S, LANES), BF16),
            jax.ShapeDtypeStruct((B, A_HEADS, nb, MOBA_BLOCK, LANES), BF16),
            jax.ShapeDtypeStruct((B, A_HEADS, nb, V_ROWS, MOBA_BLOCK), BF16),
            jax.ShapeDtypeStruct((B, S, A_WIDTH), F32),
            jax.ShapeDtypeStruct((B, S, B_WIDTH + C_WIDTH), BF16),
        ],
        scratch_shapes=[
            pltpu.VMEM((A_HEADS * nb, A_WIDTH), F32),
            pltpu.VMEM((MAX_WINDOW + T, B_WIDTH), F32),
        ],
        compiler_params=pltpu.CompilerParams(
            dimension_semantics=("arbitrary", "arbitrary"), vmem_limit_bytes=VMEM_LIMIT),
        name="proj_mix",
    )(rel_bias, h, norm_g, w_qt, w_vt, w_in, pool_bd, pool_scale, sgu_wcat, sgu_btile)


def _attn_kernel(q_ref, k_ref, vt_ref, bias_ref, gsa_ref, o_ref, sa_scr, sb_scr, m_scr, acc_scr):
    i = pl.program_id(2)
    W = MOBA_BLOCK
    n_far = jnp.maximum(2 * i - 1, 0)
    n_pairs = jnp.maximum(n_far - 1, 0) // 2
    parked = 2 * n_pairs
    extra = parked + 1
    before = jnp.maximum(2 * i - 1, 0)
    pen_parked = jnp.where(n_far >= 1, 0.0, NEG)
    pen_extra = jnp.where(extra < n_far, 0.0, NEG)
    pen_before = jnp.where(i >= 1, 0.0, NEG)
    heads = range(ATTN_HEADS)
    OWN, PREV = 0, 1

    BOTH = (0, 1)

    def park(hd, scr, j, bias_kinds=(None, None), halves=BOTH):
        rows = slice(halves[0] * W, (halves[-1] + 1) * W)
        st = _dot_nt(k_ref[0, hd, j], q_ref[0, hd, rows, :])
        for n, half in enumerate(halves):
            kind = bias_kinds[half]
            part = st[:, n * W:(n + 1) * W]
            cols = slice(half * W, (half + 1) * W)
            scr[hd, :, cols] = part if kind is None else part + bias_ref[hd, kind]

    def consume(hd, scr, j, pen=None, halves=BOTH):
        for half in halves:
            cols = slice(half * W, (half + 1) * W)
            ps, ms = [], []
            for quarter in range(2):
                qcols = slice(half * W + quarter * LANES, half * W + (quarter + 1) * LANES)
                st = scr[hd, :, qcols]
                cm = jnp.max(st, axis=0, keepdims=True)
                m_new = jnp.maximum(m_scr[hd, :, qcols], cm if pen is None else cm + pen)
                shift = m_new if pen is None else m_new - pen
                ps.append(jnp.exp2(st - shift).astype(BF16))
                ms.append(m_new)
            pv = _dot(vt_ref[0, hd, j], jnp.concatenate(ps, axis=1))
            m = m_scr[hd, :, cols]
            m_new = jnp.concatenate(ms, axis=1)
            m_scr[hd, :, cols] = m_new
            acc_scr[hd, :, cols] = jnp.exp2(m - m_new) * acc_scr[hd, :, cols] + pv

    def stage(produce=None, take=None):
        for hd in heads:
            if produce is not None:
                park(hd, *produce)
            if take is not None:
                consume(hd, *take)

    m_scr[...] = jnp.full(m_scr.shape, -jnp.inf, F32)
    acc_scr[...] = jnp.zeros(acc_scr.shape, F32)

    stage(produce=(sa_scr, 0))

    def pair_step(t, carry):
        stage(produce=(sb_scr, 2 * t + 1), take=(sa_scr, 2 * t))
        stage(produce=(sa_scr, 2 * t + 2), take=(sb_scr, 2 * t + 1))
        return carry

    def long_step(u, carry):
        for r in range(PAIR_UNROLL):
            pair_step(u * PAIR_UNROLL + r, carry)
        return carry

    n_long = n_pairs // PAIR_UNROLL
    lax.fori_loop(0, n_long, long_step, 0)
    lax.fori_loop(n_long * PAIR_UNROLL, n_pairs, pair_step, 0)

    second = (1,)
    stage(produce=(sb_scr, extra), take=(sa_scr, parked, pen_parked))
    stage(produce=(sa_scr, before, (PREV, None)), take=(sb_scr, extra, pen_extra))
    stage(produce=(sb_scr, 2 * i, (OWN, PREV)), take=(sa_scr, before, pen_before))
    stage(produce=(sa_scr, 2 * i + 1, (None, OWN), second), take=(sb_scr, 2 * i))
    stage(take=(sa_scr, 2 * i + 1, None, second))
    outs = [acc_scr[hd, 0:HEAD_DIM, :] / acc_scr[hd, HEAD_DIM:HEAD_DIM + 1, :] for hd in heads]
    o_ref[0] = (jnp.concatenate(outs, axis=0).T * gsa_ref[0]).astype(BF16)


def _attn_call(q, k_blk, vt_blk, bias_tiles, gsa):
    B, H, S, _ = q.shape
    nb = S // MOBA_BLOCK
    Tq = ROW_TILE
    G = ATTN_HEADS
    return pl.pallas_call(
        _attn_kernel,
        grid=(B, H // G, S // Tq),
        in_specs=[
            pl.BlockSpec((1, G, Tq, LANES), lambda b, hg, i: (b, hg, i, 0)),
            pl.BlockSpec((1, G, nb, MOBA_BLOCK, LANES), lambda b, hg, i: (b, hg, 0, 0, 0)),
            pl.BlockSpec((1, G, nb, V_ROWS, MOBA_BLOCK), lambda b, hg, i: (b, hg, 0, 0, 0)),
            pl.BlockSpec((G, 2, MOBA_BLOCK, MOBA_BLOCK), lambda b, hg, i: (hg, 0, 0, 0)),
            pl.BlockSpec((1, Tq, G * HEAD_DIM), lambda b, hg, i: (b, i, hg)),
        ],
        out_specs=pl.BlockSpec((1, Tq, G * HEAD_DIM), lambda b, hg, i: (b, i, hg)),
        out_shape=jax.ShapeDtypeStruct((B, S, H * HEAD_DIM), BF16),
        scratch_shapes=[
            pltpu.VMEM((G, MOBA_BLOCK, Tq), F32),
            pltpu.VMEM((G, MOBA_BLOCK, Tq), F32),
            pltpu.VMEM((G, 1, Tq), F32),
            pltpu.VMEM((G, V_ROWS, Tq), F32),
        ],
        compiler_params=pltpu.CompilerParams(
            dimension_semantics=("arbitrary", "arbitrary", "arbitrary"),
            vmem_limit_bytes=VMEM_LIMIT),
        name="moba_attn",
    )(q, k_blk, vt_blk, bias_tiles, gsa)


def _out_kernel(h_ref, ya_ref, ybc_ref, p_ref, wout_ref, plew_ref, gatew_ref,
                fg_ref, o_ref, *, final):
    y = _dot(ya_ref[...], wout_ref[0:A_WIDTH, :]) + _dot(ybc_ref[...], wout_ref[A_WIDTH:, :])
    h1 = h_ref[...] + y
    emb = _dot(p_ref[...].astype(BF16), plew_ref[...])
    gate = _sigmoid(_dot(h1.astype(BF16), gatew_ref[...]))
    h2 = h1 + emb * gate
    if final:
        h2 = h2 * lax.rsqrt(jnp.mean(h2 * h2, axis=-1, keepdims=True) + EPS) * fg_ref[...]
    o_ref[...] = h2


def _out_call(layer, h, ya, ybc, p, w_out, ple_w, gate_w, final_g, final):
    N = h.shape[0]
    T = OUT_TILE
    row = lambda width: pl.BlockSpec((T, width), lambda i: (i, 0))
    full = lambda shape: pl.BlockSpec((None,) + shape, lambda i: (layer,) + (0,) * len(shape))
    return pl.pallas_call(
        functools.partial(_out_kernel, final=final),
        grid=(N // T,),
        in_specs=[
            row(D_MODEL), row(A_WIDTH), row(B_WIDTH + C_WIDTH),
            pl.BlockSpec((None, T, D_PLE), lambda i: (layer, i, 0)),
            full((D_MODEL, D_MODEL)), full((D_PLE, D_MODEL)), full((D_MODEL, D_MODEL)),
            pl.BlockSpec((1, D_MODEL), lambda i: (0, 0)),
        ],
        out_specs=row(D_MODEL),
        out_shape=jax.ShapeDtypeStruct((N, D_MODEL), F32),
        compiler_params=pltpu.CompilerParams(
            dimension_semantics=("arbitrary",), vmem_limit_bytes=VMEM_LIMIT),
        name="out_ple",
    )(h, ya, ybc, p, w_out, ple_w, gate_w, final_g)


def _block_diag(w):
    G, c, d = w.shape
    eye = jnp.eye(G, dtype=w.dtype)
    return (eye[:, None, :, None] * w[:, :, None, :]).reshape(G * c, G * d)


def kernel(x, p, norm_g, w_in, w_out, rel_bias, pool_w, pool_scale, sgu_w, sgu_b,
           ple_w, ple_gate_w, final_g):
    B, S, D = x.shape
    depth = w_in.shape[0]
    N = B * S
    bias_tiles = _bias_tiles(rel_bias)
    w_in_b = w_in.astype(BF16)
    w_qt = jnp.transpose(w_in[:, :, Q0:Q0 + A_WIDTH] * Q_SCALE, (0, 2, 1)).astype(BF16)
    w_vt = jnp.transpose(w_in_b[:, :, V0:V0 + A_WIDTH], (0, 2, 1))
    pool_bd = jax.vmap(_block_diag)(pool_w).astype(BF16)
    sgu_wcat = jnp.transpose(jnp.tril(sgu_w), (0, 2, 1, 3)).reshape(
        depth, SGU_CHUNK, C_HEADS * SGU_CHUNK).astype(BF16)
    sgu_btile = jnp.repeat(jnp.transpose(sgu_b, (0, 2, 1)), HEAD_DIM, axis=2)
    w_out_b, ple_w_b, gate_w_b = (w.astype(BF16) for w in (w_out, ple_w, ple_gate_w))
    p_rows = p.reshape(depth, N, D_PLE)
    h = x
    for i in range(depth):
        q, k_blk, vt_blk, gsa, ybc = _proj_call(
            i, rel_bias, h, norm_g[:, None, :], w_qt, w_vt, w_in_b, pool_bd,
            pool_scale[:, None, :], sgu_wcat, sgu_btile)
        ya = _attn_call(q, k_blk, vt_blk, bias_tiles, gsa)
        h = _out_call(
            i, h.reshape(N, D), ya.reshape(N, A_WIDTH), ybc.reshape(N, B_WIDTH + C_WIDTH),
            p_rows, w_out_b, ple_w_b, gate_w_b, final_g[None, :],
            final=(i == depth - 1)).reshape(B, S, D)
    return h
```

```python
import functools
import math

import numpy as np
import jax
import jax.numpy as jnp
from jax import lax
from jax.experimental import pallas as pl
from jax.experimental.pallas import tpu as pltpu

D_MODEL = 1024
HEAD_DIM = 64
A_WIDTH = 512
A_HEADS = 8
MOBA_BLOCK = 256
MOBA_TOPK = 3
REL_BUCKETS = 32
REL_MAX_DIST = 128
B_WIDTH = 256
POOL_WINDOWS = (2, 4, 8, 16)
B_GROUP = 64
C_WIDTH = 256
C_HEADS = 4
SGU_CHUNK = 128
D_PLE = 256
D_IN = 3328
EPS = 1e-6
NEG = -1e30
LOG2E = math.log2(math.e)
Q_SCALE = HEAD_DIM ** -0.5 * LOG2E

MAX_WINDOW = max(POOL_WINDOWS)
ROW_BLOCKS = 2
ROW_TILE = ROW_BLOCKS * MOBA_BLOCK
OUT_TILE = 1024
LANES = 128
MASK_HI = HEAD_DIM
MASK_LO = HEAD_DIM + 32
V_ROWS = HEAD_DIM + 16
ATTN_HEADS = 4
PAIR_UNROLLS = (4, 2, 1)
VMEM_LIMIT = 48 * 1024 * 1024

_OFF = np.cumsum((0,) + (A_WIDTH,) * 4 + (B_WIDTH,) * 2 + (C_WIDTH,) * 3)
Q0, K0, V0, GA0, XB0, GB0, UC0, VC0, GC0, _ = (int(o) for o in _OFF)

BF16 = jnp.bfloat16
F32 = jnp.float32


def _bucket_thresholds():
    max_exact = REL_BUCKETS // 2
    n = np.arange(0, 4 * MOBA_BLOCK)
    nf = np.maximum(n, 1).astype(np.float64)
    large = max_exact + (np.log(nf / max_exact) / math.log(REL_MAX_DIST / max_exact)
                         * (REL_BUCKETS - max_exact)).astype(np.int64)
    large = np.minimum(large, REL_BUCKETS - 1)
    bucket = np.where(n < max_exact, n, large)
    return [int(np.argmax(bucket >= b)) for b in range(1, REL_BUCKETS)]


_THRESHOLDS = _bucket_thresholds()


def _sigmoid(x):
    return 0.5 * jnp.tanh(0.5 * x) + 0.5


def _silu(x):
    return x * _sigmoid(x)


def _split_bf16(x):
    hi = x.astype(BF16)
    lo = (x - hi.astype(F32)).astype(BF16)
    return hi, lo


def _dot(a, b):
    return jnp.dot(a, b, preferred_element_type=F32)


def _dot_nt(a, b):
    return lax.dot_general(a, b, (((1,), (1,)), ((), ())), preferred_element_type=F32)


def _bias_kernel(rb_ref, out_ref):
    h = pl.program_id(0)
    key = lax.broadcasted_iota(jnp.int32, (MOBA_BLOCK, MOBA_BLOCK), 0)
    qry = lax.broadcasted_iota(jnp.int32, (MOBA_BLOCK, MOBA_BLOCK), 1)
    for kind, shift in ((0, 0), (1, MOBA_BLOCK)):
        dist = qry - key + shift
        val = jnp.full((MOBA_BLOCK, MOBA_BLOCK), rb_ref[0, h], F32)
        for b in range(1, REL_BUCKETS):
            val = jnp.where(dist >= _THRESHOLDS[b - 1], rb_ref[b, h], val)
        val = val * LOG2E
        if kind == 0:
            val = jnp.where(dist >= 0, val, NEG)
        out_ref[0, kind] = val


def _bias_tiles(rel_bias):
    return pl.pallas_call(
        _bias_kernel,
        grid=(A_HEADS,),
        in_specs=[pl.BlockSpec(memory_space=pltpu.SMEM)],
        out_specs=pl.BlockSpec((1, 2, MOBA_BLOCK, MOBA_BLOCK), lambda h: (h, 0, 0, 0)),
        out_shape=jax.ShapeDtypeStruct((A_HEADS, 2, MOBA_BLOCK, MOBA_BLOCK), F32),
        name="bias_tiles",
    )(rel_bias)


def _proj_kernel(rb_ref, h_ref, ng_ref, wqt_ref, wvt_ref, win_ref, poolw_ref, pscale_ref,
                 sguw_ref, sgub_ref,
                 q_ref, k_ref, vt_ref, gsa_ref, ybc_ref,
                 kmt_scr, ext_scr, *, n_blocks):
    T = ROW_TILE
    W = MOBA_BLOCK
    s = pl.program_id(1)
    first_block = s * ROW_BLOCKS

    @pl.when(s == 0)
    def _():
        kmt_scr[...] = jnp.zeros_like(kmt_scr)
        ext_scr[0:MAX_WINDOW, :] = jnp.zeros((MAX_WINDOW, B_WIDTH), F32)

    hbs, zks = [], []
    for blk in range(ROW_BLOCKS):
        h = h_ref[0, blk * W:(blk + 1) * W, :]
        hn = h * lax.rsqrt(jnp.mean(h * h, axis=-1, keepdims=True) + EPS) * ng_ref[...]
        hbs.append(hn.astype(BF16))
        zks.append(_dot(hbs[-1], win_ref[:, K0:K0 + A_WIDTH]))
    hb = jnp.concatenate(hbs, axis=0)

    def proj(c0, width):
        return _dot(hb, win_ref[:, c0:c0 + width])

    zqt = _dot_nt(wqt_ref[...], hb)

    lane = lax.broadcasted_iota(jnp.int32, (1, LANES), 1)
    for blk, zk_blk in enumerate(zks):
        j = first_block + blk
        indicator = jnp.where((lane == MASK_HI + j) | (lane == MASK_LO + j), 1.0, 0.0)
        k_mean = jnp.mean(zk_blk, axis=0, keepdims=True)
        for pair in range(A_HEADS // 2):
            zk_pair = zk_blk[:, pair * LANES:(pair + 1) * LANES]
            zk_odd = pltpu.roll(zk_pair, HEAD_DIM, 1)
            k_ref[0, 2 * pair, blk] = jnp.where(lane < HEAD_DIM, zk_pair, indicator).astype(BF16)
            k_ref[0, 2 * pair + 1, blk] = jnp.where(lane < HEAD_DIM, zk_odd, indicator).astype(BF16)
            mean_pair = k_mean[:, pair * LANES:(pair + 1) * LANES]
            kmt_scr[2 * pair, pl.ds(j, 1), :] = mean_pair
            kmt_scr[2 * pair + 1, pl.ds(j, 1), :] = pltpu.roll(mean_pair, HEAD_DIM, 1)

    q_hi, q_lo = _split_bf16(zqt)
    gates = []
    for hd in range(A_HEADS):
        rows = slice(hd * HEAD_DIM, (hd + 1) * HEAD_DIM)
        km_hi, km_lo = _split_bf16(kmt_scr[hd][:, 0:HEAD_DIM])
        gates.append(_dot(km_hi, q_hi[rows]) + _dot(km_hi, q_lo[rows]) + _dot(km_lo, q_hi[rows]))

    zvt = _dot_nt(wvt_ref[...], hb)
    zga = proj(GA0, A_WIDTH)
    xb = proj(XB0, B_WIDTH)
    zgb = proj(GB0, B_WIDTH)
    vc = proj(VC0, C_WIDTH)
    zuc = proj(UC0, C_WIDTH)
    zgc = proj(GC0, C_WIDTH)

    jrow = lax.broadcasted_iota(jnp.int32, (n_blocks, T), 0)
    own = first_block + lax.broadcasted_iota(jnp.int32, (1, T), 1) // W
    for hd in range(A_HEADS):
        g = jnp.where(jrow < own, gates[hd], NEG)
        sel = jnp.zeros((n_blocks, T), jnp.bool_)
        for _ in range(MOBA_TOPK):
            m = jnp.max(g, axis=0, keepdims=True)
            idx = jnp.min(jnp.where(g == m, jrow, n_blocks), axis=0, keepdims=True)
            pick = jrow == idx
            sel = sel | (pick & (m > NEG * 0.5))
            g = jnp.where(pick, -jnp.inf, g)
        far_bias = rb_ref[REL_BUCKETS - 1, hd] * LOG2E
        term = jnp.where(sel, jnp.where(jrow <= own - 2, far_bias, 0.0), NEG)
        term = jnp.where(jrow == own, 0.0, term)
        term_hi = term.astype(BF16).astype(F32)
        q_aug_t = jnp.concatenate(
            [zqt[hd * HEAD_DIM:(hd + 1) * HEAD_DIM], term_hi, term - term_hi], axis=0)
        q_ref[0, hd] = q_aug_t.T.astype(BF16)

    tail_row = lax.broadcasted_iota(jnp.int32, (V_ROWS - HEAD_DIM, W), 0)
    tail = jnp.where(tail_row == 0, 1.0, 0.0)
    for hd in range(A_HEADS):
        for blk in range(ROW_BLOCKS):
            vt_ref[0, hd, blk] = jnp.concatenate(
                [zvt[hd * HEAD_DIM:(hd + 1) * HEAD_DIM, blk * W:(blk + 1) * W], tail],
                axis=0).astype(BF16)

    gsa_ref[0] = _silu(zga)

    ext_scr[MAX_WINDOW:MAX_WINDOW + T, :] = xb
    lane_b = lax.broadcasted_iota(jnp.int32, (1, B_WIDTH), 1)
    win = jnp.zeros((1, B_WIDTH), F32)
    for gi, w in enumerate(POOL_WINDOWS):
        win = jnp.where(lane_b // B_GROUP == gi, float(w), win)
    halves = []
    for half, (w_small, w_big) in enumerate(zip(POOL_WINDOWS[0::2], POOL_WINDOWS[1::2])):
        cols = slice(half * LANES, (half + 1) * LANES)
        run = xb[:, cols]
        for lag in range(1, w_small):
            run = run + ext_scr[MAX_WINDOW - lag:MAX_WINDOW - lag + T, cols]
        small = run
        for lag in range(w_small, w_big):
            run = run + ext_scr[MAX_WINDOW - lag:MAX_WINDOW - lag + T, cols]
        halves.append(jnp.where(lane < B_GROUP, small, run))
    wsum = jnp.concatenate(halves, axis=1)
    pos = (s * T + lax.broadcasted_iota(jnp.int32, (T, 1), 0) + 1).astype(F32)
    pooled = wsum / jnp.minimum(pos, win) - xb
    ext_scr[0:MAX_WINDOW, :] = xb[T - MAX_WINDOW:T, :]
    mixed_b = _dot(pooled.astype(BF16), poolw_ref[...]) * pscale_ref[...]
    ybc_ref[0, :, 0:B_WIDTH] = (mixed_b * _silu(zgb)).astype(BF16)

    mu = jnp.mean(vc, axis=-1, keepdims=True)
    cen = vc - mu
    var = jnp.mean(cen * cen, axis=-1, keepdims=True)
    vn = cen * lax.rsqrt(var + EPS)
    rows = lax.broadcasted_iota(jnp.int32, (C_HEADS * SGU_CHUNK, C_WIDTH), 0)
    cols = lax.broadcasted_iota(jnp.int32, (C_HEADS * SGU_CHUNK, C_WIDTH), 1)
    head_sel = (rows // SGU_CHUNK) == (cols // HEAD_DIM)
    ug = zuc * _silu(zgc)
    for c in range(T // SGU_CHUNK):
        vn_c = vn[c * SGU_CHUNK:(c + 1) * SGU_CHUNK]
        stack = jnp.where(head_sel, jnp.concatenate([vn_c] * C_HEADS, axis=0), 0.0)
        mixed_c = _dot(sguw_ref[...], stack.astype(BF16)) + sgub_ref[...]
        ybc_ref[0, c * SGU_CHUNK:(c + 1) * SGU_CHUNK, B_WIDTH:B_WIDTH + C_WIDTH] = (
            ug[c * SGU_CHUNK:(c + 1) * SGU_CHUNK] * mixed_c).astype(BF16)


def _proj_call(layer, rel_bias, h, norm_g, w_qt, w_vt, w_in, pool_bd, pool_scale, sgu_wcat,
               sgu_btile):
    B, S, _ = h.shape
    T = ROW_TILE
    nb = S // MOBA_BLOCK
    assert nb <= MASK_LO - MASK_HI and MASK_LO + nb <= LANES
    full = lambda shape: pl.BlockSpec((None,) + shape, lambda b, s: (layer,) + (0,) * len(shape))
    return pl.pallas_call(
        functools.partial(_proj_kernel, n_blocks=nb),
        grid=(B, S // T),
        in_specs=[
            pl.BlockSpec(memory_space=pltpu.SMEM),
            pl.BlockSpec((1, T, D_MODEL), lambda b, s: (b, s, 0)),
            full((1, D_MODEL)),
            full((A_WIDTH, D_MODEL)),
            full((A_WIDTH, D_MODEL)),
            full((D_MODEL, D_IN)),
            full((B_WIDTH, B_WIDTH)),
            full((1, B_WIDTH)),
            full((SGU_CHUNK, C_HEADS * SGU_CHUNK)),
            full((SGU_CHUNK, C_WIDTH)),
        ],
        out_specs=[
            pl.BlockSpec((1, A_HEADS, T, LANES), lambda b, s: (b, 0, s, 0)),
            pl.BlockSpec((1, A_HEADS, ROW_BLOCKS, MOBA_BLOCK, LANES), lambda b, s: (b, 0, s, 0, 0)),
            pl.BlockSpec((1, A_HEADS, ROW_BLOCKS, V_ROWS, MOBA_BLOCK), lambda b, s: (b, 0, s, 0, 0)),
            pl.BlockSpec((1, T, A_WIDTH), lambda b, s: (b, s, 0)),
            pl.BlockSpec((1, T, B_WIDTH + C_WIDTH), lambda b, s: (b, s, 0)),
        ],
        out_shape=[
            jax.ShapeDtypeStruct((B, A_HEADS, S, LANES), BF16),
            jax.ShapeDtypeStruct((B, A_HEADS, nb, MOBA_BLOCK, LANES), BF16),
            jax.ShapeDtypeStruct((B, A_HEADS, nb, V_ROWS, MOBA_BLOCK), BF16),
            jax.ShapeDtypeStruct((B, S, A_WIDTH), F32),
            jax.ShapeDtypeStruct((B, S, B_WIDTH + C_WIDTH), BF16),
        ],
        scratch_shapes=[
            pltpu.VMEM((A_HEADS, nb, LANES), F32),
            pltpu.VMEM((MAX_WINDOW + T, B_WIDTH), F32),
        ],
        compiler_params=pltpu.CompilerParams(
            dimension_semantics=("arbitrary", "arbitrary"), vmem_limit_bytes=VMEM_LIMIT),
        name="proj_mix",
    )(rel_bias, h, norm_g, w_qt, w_vt, w_in, pool_bd, pool_scale, sgu_wcat, sgu_btile)


def _attn_kernel(q_ref, k_ref, vt_ref, bias_ref, gsa_ref, o_ref, sa_scr, sb_scr, m_scr, acc_scr):
    i = pl.program_id(2)
    W = MOBA_BLOCK
    n_far = jnp.maximum(2 * i - 1, 0)
    n_pairs = jnp.maximum(n_far - 1, 0) // 2
    parked = 2 * n_pairs
    extra = parked + 1
    before = jnp.maximum(2 * i - 1, 0)
    pen_parked = jnp.where(n_far >= 1, 0.0, NEG)
    pen_extra = jnp.where(extra < n_far, 0.0, NEG)
    pen_before = jnp.where(i >= 1, 0.0, NEG)
    heads = range(ATTN_HEADS)
    OWN, PREV = 0, 1

    BOTH = (0, 1)

    def park(hd, scr, j, bias_kinds=(None, None), halves=BOTH):
        rows = slice(halves[0] * W, (halves[-1] + 1) * W)
        st = _dot_nt(k_ref[0, hd, j], q_ref[0, hd, rows, :])
        for n, half in enumerate(halves):
            kind = bias_kinds[half]
            part = st[:, n * W:(n + 1) * W]
            cols = slice(half * W, (half + 1) * W)
            scr[hd, :, cols] = part if kind is None else part + bias_ref[hd, kind]

    def consume(hd, scr, j, pen=None, halves=BOTH):
        for half in halves:
            cols = slice(half * W, (half + 1) * W)
            ps, ms = [], []
            for quarter in range(2):
                qcols = slice(half * W + quarter * LANES, half * W + (quarter + 1) * LANES)
                st = scr[hd, :, qcols]
                cm = jnp.max(st, axis=0, keepdims=True)
                m_new = jnp.maximum(m_scr[hd, :, qcols], cm if pen is None else cm + pen)
                shift = m_new if pen is None else m_new - pen
                ps.append(jnp.exp2(st - shift).astype(BF16))
                ms.append(m_new)
            pv = _dot(vt_ref[0, hd, j], jnp.concatenate(ps, axis=1))
            m = m_scr[hd, :, cols]
            m_new = jnp.concatenate(ms, axis=1)
            m_scr[hd, :, cols] = m_new
            acc_scr[hd, :, cols] = jnp.exp2(m - m_new) * acc_scr[hd, :, cols] + pv

    def stage(produce=None, take=None):
        for hd in heads:
            if produce is not None:
                park(hd, *produce)
            if take is not None:
                consume(hd, *take)

    m_scr[...] = jnp.full(m_scr.shape, -jnp.inf, F32)
    acc_scr[...] = jnp.zeros(acc_scr.shape, F32)

    stage(produce=(sa_scr, 0))

    def pair_step(t, carry):
        stage(produce=(sb_scr, 2 * t + 1), take=(sa_scr, 2 * t))
        stage(produce=(sa_scr, 2 * t + 2), take=(sb_scr, 2 * t + 1))
        return carry

    done = 0
    for unroll in PAIR_UNROLLS:
        trips = (n_pairs - done) // unroll

        def body(u, carry, first=done, unroll=unroll):
            for r in range(unroll):
                pair_step(first + u * unroll + r, carry)
            return carry

        lax.fori_loop(0, trips, body, 0)
        done = done + trips * unroll

    second = (1,)
    stage(produce=(sb_scr, extra), take=(sa_scr, parked, pen_parked))
    stage(produce=(sa_scr, before, (PREV, None)), take=(sb_scr, extra, pen_extra))
    stage(produce=(sb_scr, 2 * i, (OWN, PREV)), take=(sa_scr, before, pen_before))
    stage(produce=(sa_scr, 2 * i + 1, (None, OWN), second), take=(sb_scr, 2 * i))
    stage(take=(sa_scr, 2 * i + 1, None, second))
    outs = [acc_scr[hd, 0:HEAD_DIM, :] / acc_scr[hd, HEAD_DIM:HEAD_DIM + 1, :] for hd in heads]
    o_ref[0] = (jnp.concatenate(outs, axis=0).T * gsa_ref[0]).astype(BF16)


def _attn_call(q, k_blk, vt_blk, bias_tiles, gsa):
    B, H, S, _ = q.shape
    nb = S // MOBA_BLOCK
    Tq = ROW_TILE
    G = ATTN_HEADS
    return pl.pallas_call(
        _attn_kernel,
        grid=(B, H // G, S // Tq),
        in_specs=[
            pl.BlockSpec((1, G, Tq, LANES), lambda b, hg, i: (b, hg, i, 0)),
            pl.BlockSpec((1, G, nb, MOBA_BLOCK, LANES), lambda b, hg, i: (b, hg, 0, 0, 0)),
            pl.BlockSpec((1, G, nb, V_ROWS, MOBA_BLOCK), lambda b, hg, i: (b, hg, 0, 0, 0)),
            pl.BlockSpec((G, 2, MOBA_BLOCK, MOBA_BLOCK), lambda b, hg, i: (hg, 0, 0, 0)),
            pl.BlockSpec((1, Tq, G * HEAD_DIM), lambda b, hg, i: (b, i, hg)),
        ],
        out_specs=pl.BlockSpec((1, Tq, G * HEAD_DIM), lambda b, hg, i: (b, i, hg)),
        out_shape=jax.ShapeDtypeStruct((B, S, H * HEAD_DIM), BF16),
        scratch_shapes=[
            pltpu.VMEM((G, MOBA_BLOCK, Tq), F32),
            pltpu.VMEM((G, MOBA_BLOCK, Tq), F32),
            pltpu.VMEM((G, 1, Tq), F32),
            pltpu.VMEM((G, V_ROWS, Tq), F32),
        ],
        compiler_params=pltpu.CompilerParams(
            dimension_semantics=("arbitrary", "arbitrary", "arbitrary"),
            vmem_limit_bytes=VMEM_LIMIT),
        name="moba_attn",
    )(q, k_blk, vt_blk, bias_tiles, gsa)


def _out_kernel(h_ref, ya_ref, ybc_ref, p_ref, wout_ref, plew_ref, gatew_ref,
                fg_ref, o_ref, *, final):
    y = _dot(ya_ref[...], wout_ref[0:A_WIDTH, :]) + _dot(ybc_ref[...], wout_ref[A_WIDTH:, :])
    h1 = h_ref[...] + y
    emb = _dot(p_ref[...].astype(BF16), plew_ref[...])
    gate = _sigmoid(_dot(h1.astype(BF16), gatew_ref[...]))
    h2 = h1 + emb * gate
    if final:
        h2 = h2 * lax.rsqrt(jnp.mean(h2 * h2, axis=-1, keepdims=True) + EPS) * fg_ref[...]
    o_ref[...] = h2


def _out_call(layer, h, ya, ybc, p, w_out, ple_w, gate_w, final_g, final):
    N = h.shape[0]
    T = OUT_TILE
    row = lambda width: pl.BlockSpec((T, width), lambda i: (i, 0))
    full = lambda shape: pl.BlockSpec((None,) + shape, lambda i: (layer,) + (0,) * len(shape))
    return pl.pallas_call(
        functools.partial(_out_kernel, final=final),
        grid=(N // T,),
        in_specs=[
            row(D_MODEL), row(A_WIDTH), row(B_WIDTH + C_WIDTH),
            pl.BlockSpec((None, T, D_PLE), lambda i: (layer, i, 0)),
            full((D_MODEL, D_MODEL)), full((D_PLE, D_MODEL)), full((D_MODEL, D_MODEL)),
            pl.BlockSpec((1, D_MODEL), lambda i: (0, 0)),
        ],
        out_specs=row(D_MODEL),
        out_shape=jax.ShapeDtypeStruct((N, D_MODEL), F32),
        compiler_params=pltpu.CompilerParams(
            dimension_semantics=("arbitrary",), vmem_limit_bytes=VMEM_LIMIT),
        name="out_ple",
    )(h, ya, ybc, p, w_out, ple_w, gate_w, final_g)


def _block_diag(w):
    G, c, d = w.shape
    eye = jnp.eye(G, dtype=w.dtype)
    return (eye[:, None, :, None] * w[:, :, None, :]).reshape(G * c, G * d)


def kernel(x, p, norm_g, w_in, w_out, rel_bias, pool_w, pool_scale, sgu_w, sgu_b,
           ple_w, ple_gate_w, final_g):
    B, S, D = x.shape
    depth = w_in.shape[0]
    N = B * S
    bias_tiles = _bias_tiles(rel_bias)
    w_in_b = w_in.astype(BF16)
    w_qt = jnp.transpose(w_in[:, :, Q0:Q0 + A_WIDTH] * Q_SCALE, (0, 2, 1)).astype(BF16)
    w_vt = jnp.transpose(w_in_b[:, :, V0:V0 + A_WIDTH], (0, 2, 1))
    pool_bd = jax.vmap(_block_diag)(pool_w).astype(BF16)
    sgu_wcat = jnp.transpose(jnp.tril(sgu_w), (0, 2, 1, 3)).reshape(
        depth, SGU_CHUNK, C_HEADS * SGU_CHUNK).astype(BF16)
    sgu_btile = jnp.repeat(jnp.transpose(sgu_b, (0, 2, 1)), HEAD_DIM, axis=2)
    w_out_b, ple_w_b, gate_w_b = (w.astype(BF16) for w in (w_out, ple_w, ple_gate_w))
    p_rows = p.reshape(depth, N, D_PLE)
    h = x
    for i in range(depth):
        q, k_blk, vt_blk, gsa, ybc = _proj_call(
            i, rel_bias, h, norm_g[:, None, :], w_qt, w_vt, w_in_b, pool_bd,
            pool_scale[:, None, :], sgu_wcat, sgu_btile)
        ya = _attn_call(q, k_blk, vt_blk, bias_tiles, gsa)
        h = _out_call(
            i, h.reshape(N, D), ya.reshape(N, A_WIDTH), ybc.reshape(N, B_WIDTH + C_WIDTH),
            p_rows, w_out_b, ple_w_b, gate_w_b, final_g[None, :],
            final=(i == depth - 1)).reshape(B, S, D)
    return h
```

```python
import functools
import math

import numpy as np
import jax
import jax.numpy as jnp
from jax import lax
from jax.experimental import pallas as pl
from jax.experimental.pallas import tpu as pltpu

D_MODEL = 1024
HEAD_DIM = 64
A_WIDTH = 512
A_HEADS = 8
MOBA_BLOCK = 256
MOBA_TOPK = 3
REL_BUCKETS = 32
REL_MAX_DIST = 128
B_WIDTH = 256
POOL_WINDOWS = (2, 4, 8, 16)
B_GROUP = 64
C_WIDTH = 256
C_HEADS = 4
SGU_CHUNK = 128
D_PLE = 256
D_IN = 3328
EPS = 1e-6
NEG = -1e30
LOG2E = math.log2(math.e)
Q_SCALE = HEAD_DIM ** -0.5 * LOG2E

MAX_WINDOW = max(POOL_WINDOWS)
ROW_BLOCKS = 2
ROW_TILE = ROW_BLOCKS * MOBA_BLOCK
OUT_TILE = 1024
LANES = 128
MASK_HI = HEAD_DIM
MASK_LO = HEAD_DIM + 32
V_ROWS = HEAD_DIM + 16
ATTN_HEADS = 4
PAIR_UNROLLS = (4, 2, 1)
VMEM_LIMIT = 48 * 1024 * 1024

_OFF = np.cumsum((0,) + (A_WIDTH,) * 4 + (B_WIDTH,) * 2 + (C_WIDTH,) * 3)
Q0, K0, V0, GA0, XB0, GB0, UC0, VC0, GC0, _ = (int(o) for o in _OFF)

BF16 = jnp.bfloat16
F32 = jnp.float32


def _bucket_thresholds():
    max_exact = REL_BUCKETS // 2
    n = np.arange(0, 4 * MOBA_BLOCK)
    nf = np.maximum(n, 1).astype(np.float64)
    large = max_exact + (np.log(nf / max_exact) / math.log(REL_MAX_DIST / max_exact)
                         * (REL_BUCKETS - max_exact)).astype(np.int64)
    large = np.minimum(large, REL_BUCKETS - 1)
    bucket = np.where(n < max_exact, n, large)
    return [int(np.argmax(bucket >= b)) for b in range(1, REL_BUCKETS)]


_THRESHOLDS = _bucket_thresholds()


def _sigmoid(x):
    return 0.5 * jnp.tanh(0.5 * x) + 0.5


def _silu(x):
    return x * _sigmoid(x)


def _split_bf16(x):
    hi = x.astype(BF16)
    lo = (x - hi.astype(F32)).astype(BF16)
    return hi, lo


def _dot(a, b):
    return jnp.dot(a, b, preferred_element_type=F32)


def _dot_nt(a, b):
    return lax.dot_general(a, b, (((1,), (1,)), ((), ())), preferred_element_type=F32)


def _bias_kernel(rb_ref, out_ref):
    h = pl.program_id(0)
    key = lax.broadcasted_iota(jnp.int32, (MOBA_BLOCK, MOBA_BLOCK), 0)
    qry = lax.broadcasted_iota(jnp.int32, (MOBA_BLOCK, MOBA_BLOCK), 1)
    for kind, shift in ((0, 0), (1, MOBA_BLOCK)):
        dist = qry - key + shift
        val = jnp.full((MOBA_BLOCK, MOBA_BLOCK), rb_ref[0, h], F32)
        for b in range(1, REL_BUCKETS):
            val = jnp.where(dist >= _THRESHOLDS[b - 1], rb_ref[b, h], val)
        val = val * LOG2E
        if kind == 0:
            val = jnp.where(dist >= 0, val, NEG)
        out_ref[0, kind] = val


def _bias_tiles(rel_bias):
    return pl.pallas_call(
        _bias_kernel,
        grid=(A_HEADS,),
        in_specs=[pl.BlockSpec(memory_space=pltpu.SMEM)],
        out_specs=pl.BlockSpec((1, 2, MOBA_BLOCK, MOBA_BLOCK), lambda h: (h, 0, 0, 0)),
        out_shape=jax.ShapeDtypeStruct((A_HEADS, 2, MOBA_BLOCK, MOBA_BLOCK), F32),
        name="bias_tiles",
    )(rel_bias)


def _proj_kernel(rb_ref, h_ref, ng_ref, wqt_ref, wvt_ref, win_ref, poolw_ref, pscale_ref,
                 sguw_ref, sgub_ref,
                 q_ref, k_ref, vt_ref, gsa_ref, ybc_ref,
                 kmt_scr, ext_scr, *, n_blocks):
    T = ROW_TILE
    W = MOBA_BLOCK
    s = pl.program_id(1)
    first_block = s * ROW_BLOCKS

    @pl.when(s == 0)
    def _():
        kmt_scr[...] = jnp.zeros_like(kmt_scr)
        ext_scr[0:MAX_WINDOW, :] = jnp.zeros((MAX_WINDOW, B_WIDTH), F32)

    hbs, zks = [], []
    for blk in range(ROW_BLOCKS):
        h = h_ref[0, blk * W:(blk + 1) * W, :]
        hn = h * lax.rsqrt(jnp.mean(h * h, axis=-1, keepdims=True) + EPS) * ng_ref[...]
        hbs.append(hn.astype(BF16))
        zks.append(_dot(hbs[-1], win_ref[:, K0:K0 + A_WIDTH]))
    hb = jnp.concatenate(hbs, axis=0)

    def proj(c0, width):
        return _dot(hb, win_ref[:, c0:c0 + width])

    zqt = _dot_nt(wqt_ref[...], hb)

    lane = lax.broadcasted_iota(jnp.int32, (1, LANES), 1)
    for blk, zk_blk in enumerate(zks):
        j = first_block + blk
        indicator = jnp.where((lane == MASK_HI + j) | (lane == MASK_LO + j), 1.0, 0.0)
        k_mean = jnp.mean(zk_blk, axis=0, keepdims=True)
        for pair in range(A_HEADS // 2):
            zk_pair = zk_blk[:, pair * LANES:(pair + 1) * LANES]
            zk_odd = pltpu.roll(zk_pair, HEAD_DIM, 1)
            k_ref[0, 2 * pair, blk] = jnp.where(lane < HEAD_DIM, zk_pair, indicator).astype(BF16)
            k_ref[0, 2 * pair + 1, blk] = jnp.where(lane < HEAD_DIM, zk_odd, indicator).astype(BF16)
            mean_pair = k_mean[:, pair * LANES:(pair + 1) * LANES]
            kmt_scr[2 * pair, pl.ds(j, 1), :] = mean_pair
            kmt_scr[2 * pair + 1, pl.ds(j, 1), :] = pltpu.roll(mean_pair, HEAD_DIM, 1)

    q_hi, q_lo = _split_bf16(zqt)
    gates = []
    for hd in range(A_HEADS):
        rows = slice(hd * HEAD_DIM, (hd + 1) * HEAD_DIM)
        km_hi, km_lo = _split_bf16(kmt_scr[hd][:, 0:HEAD_DIM])
        gates.append(_dot(km_hi, q_hi[rows]) + _dot(km_hi, q_lo[rows]) + _dot(km_lo, q_hi[rows]))

    zvt = _dot_nt(wvt_ref[...], hb)
    zga = proj(GA0, A_WIDTH)
    xb = proj(XB0, B_WIDTH)
    zgb = proj(GB0, B_WIDTH)
    vc = proj(VC0, C_WIDTH)
    zuc = proj(UC0, C_WIDTH)
    zgc = proj(GC0, C_WIDTH)

    jrow = lax.broadcasted_iota(jnp.int32, (n_blocks, T), 0)
    own = first_block + lax.broadcasted_iota(jnp.int32, (1, T), 1) // W
    for hd in range(A_HEADS):
        g = jnp.where(jrow < own, gates[hd], NEG)
        sel = jnp.zeros((n_blocks, T), jnp.bool_)
        for _ in range(MOBA_TOPK):
            m = jnp.max(g, axis=0, keepdims=True)
            idx = jnp.min(jnp.where(g == m, jrow, n_blocks), axis=0, keepdims=True)
            pick = jrow == idx
            sel = sel | (pick & (m > NEG * 0.5))
            g = jnp.where(pick, -jnp.inf, g)
        far_bias = rb_ref[REL_BUCKETS - 1, hd] * LOG2E
        term = jnp.where(sel, jnp.where(jrow <= own - 2, far_bias, 0.0), NEG)
        term = jnp.where(jrow == own, 0.0, term)
        term_hi = term.astype(BF16).astype(F32)
        q_aug_t = jnp.concatenate(
            [zqt[hd * HEAD_DIM:(hd + 1) * HEAD_DIM], term_hi, term - term_hi], axis=0)
        q_ref[0, hd] = q_aug_t.T.astype(BF16)

    tail_row = lax.broadcasted_iota(jnp.int32, (V_ROWS - HEAD_DIM, W), 0)
    tail = jnp.where(tail_row == 0, 1.0, 0.0)
    for hd in range(A_HEADS):
        for blk in range(ROW_BLOCKS):
            vt_ref[0, hd, blk] = jnp.concatenate(
                [zvt[hd * HEAD_DIM:(hd + 1) * HEAD_DIM, blk * W:(blk + 1) * W], tail],
                axis=0).astype(BF16)

    gsa_ref[0] = _silu(zga)

    ext_scr[MAX_WINDOW:MAX_WINDOW + T, :] = xb
    lane_b = lax.broadcasted_iota(jnp.int32, (1, B_WIDTH), 1)
    win = jnp.zeros((1, B_WIDTH), F32)
    for gi, w in enumerate(POOL_WINDOWS):
        win = jnp.where(lane_b // B_GROUP == gi, float(w), win)
    halves = []
    for half, (w_small, w_big) in enumerate(zip(POOL_WINDOWS[0::2], POOL_WINDOWS[1::2])):
        cols = slice(half * LANES, (half + 1) * LANES)
        run = xb[:, cols]
        for lag in range(1, w_small):
            run = run + ext_scr[MAX_WINDOW - lag:MAX_WINDOW - lag + T, cols]
        small = run
        for lag in range(w_small, w_big):
            run = run + ext_scr[MAX_WINDOW - lag:MAX_WINDOW - lag + T, cols]
        halves.append(jnp.where(lane < B_GROUP, small, run))
    wsum = jnp.concatenate(halves, axis=1)
    pos = (s * T + lax.broadcasted_iota(jnp.int32, (T, 1), 0) + 1).astype(F32)
    pooled = wsum / jnp.minimum(pos, win) - xb
    ext_scr[0:MAX_WINDOW, :] = xb[T - MAX_WINDOW:T, :]
    mixed_b = _dot(pooled.astype(BF16), poolw_ref[...]) * pscale_ref[...]
    ybc_ref[0, :, 0:B_WIDTH] = (mixed_b * _silu(zgb)).astype(BF16)

    mu = jnp.mean(vc, axis=-1, keepdims=True)
    cen = vc - mu
    var = jnp.mean(cen * cen, axis=-1, keepdims=True)
    vn = cen * lax.rsqrt(var + EPS)
    rows = lax.broadcasted_iota(jnp.int32, (C_HEADS * SGU_CHUNK, C_WIDTH), 0)
    cols = lax.broadcasted_iota(jnp.int32, (C_HEADS * SGU_CHUNK, C_WIDTH), 1)
    head_sel = (rows // SGU_CHUNK) == (cols // HEAD_DIM)
    ug = zuc * _silu(zgc)
    for c in range(T // SGU_CHUNK):
        vn_c = vn[c * SGU_CHUNK:(c + 1) * SGU_CHUNK]
        stack = jnp.where(head_sel, jnp.concatenate([vn_c] * C_HEADS, axis=0), 0.0)
        mixed_c = _dot(sguw_ref[...], stack.astype(BF16)) + sgub_ref[...]
        ybc_ref[0, c * SGU_CHUNK:(c + 1) * SGU_CHUNK, B_WIDTH:B_WIDTH + C_WIDTH] = (
            ug[c * SGU_CHUNK:(c + 1) * SGU_CHUNK] * mixed_c).astype(BF16)


def _proj_call(layer, rel_bias, h, norm_g, w_qt, w_vt, w_in, pool_bd, pool_scale, sgu_wcat,
               sgu_btile):
    B, S, _ = h.shape
    T = ROW_TILE
    nb = S // MOBA_BLOCK
    assert nb <= MASK_LO - MASK_HI and MASK_LO + nb <= LANES
    full = lambda shape: pl.BlockSpec((None,) + shape, lambda b, s: (layer,) + (0,) * len(shape))
    return pl.pallas_call(
        functools.partial(_proj_kernel, n_blocks=nb),
        grid=(B, S // T),
        in_specs=[
            pl.BlockSpec(memory_space=pltpu.SMEM),
            pl.BlockSpec((1, T, D_MODEL), lambda b, s: (b, s, 0)),
            full((1, D_MODEL)),
            full((A_WIDTH, D_MODEL)),
            full((A_WIDTH, D_MODEL)),
            full((D_MODEL, D_IN)),
            full((B_WIDTH, B_WIDTH)),
            full((1, B_WIDTH)),
            full((SGU_CHUNK, C_HEADS * SGU_CHUNK)),
            full((SGU_CHUNK, C_WIDTH)),
        ],
        out_specs=[
            pl.BlockSpec((1, A_HEADS, T, LANES), lambda b, s: (b, 0, s, 0)),
            pl.BlockSpec((1, A_HEADS, ROW_BLOCKS, MOBA_BLOCK, LANES), lambda b, s: (b, 0, s, 0, 0)),
            pl.BlockSpec((1, A_HEADS, ROW_BLOCKS, V_ROWS, MOBA_BLOCK), lambda b, s: (b, 0, s, 0, 0)),
            pl.BlockSpec((1, T, A_WIDTH), lambda b, s: (b, s, 0)),
            pl.BlockSpec((1, T, B_WIDTH + C_WIDTH), lambda b, s: (b, s, 0)),
        ],
        out_shape=[
            jax.ShapeDtypeStruct((B, A_HEADS, S, LANES), BF16),
            jax.ShapeDtypeStruct((B, A_HEADS, nb, MOBA_BLOCK, LANES), BF16),
            jax.ShapeDtypeStruct((B, A_HEADS, nb, V_ROWS, MOBA_BLOCK), BF16),
            jax.ShapeDtypeStruct((B, S, A_WIDTH), F32),
            jax.ShapeDtypeStruct((B, S, B_WIDTH + C_WIDTH), BF16),
        ],
        scratch_shapes=[
            pltpu.VMEM((A_HEADS, nb, LANES), F32),
            pltpu.VMEM((MAX_WINDOW + T, B_WIDTH), F32),
        ],
        compiler_params=pltpu.CompilerParams(
            dimension_semantics=("arbitrary", "arbitrary"), vmem_limit_bytes=VMEM_LIMIT),
        name="proj_mix",
    )(rel_bias, h, norm_g, w_qt, w_vt, w_in, pool_bd, pool_scale, sgu_wcat, sgu_btile)


def _attn_kernel(q_ref, k_ref, vt_ref, bias_ref, gsa_ref, o_ref, sa_scr, sb_scr, m_scr, acc_scr):
    i = pl.program_id(2)
    W = MOBA_BLOCK
    n_pairs = jnp.maximum(i - 1, 0)
    parked = 2 * n_pairs
    before = jnp.maximum(2 * i - 1, 0)
    pen_parked = jnp.where(i >= 1, 0.0, NEG)
    pen_before = jnp.where(i >= 1, 0.0, NEG)
    heads = range(ATTN_HEADS)
    OWN, PREV = 0, 1

    BOTH = (0, 1)

    def park(hd, scr, j, bias_kinds=(None, None), halves=BOTH):
        rows = slice(halves[0] * W, (halves[-1] + 1) * W)
        st = _dot_nt(k_ref[0, hd, j], q_ref[0, hd, rows, :])
        for n, half in enumerate(halves):
            kind = bias_kinds[half]
            part = st[:, n * W:(n + 1) * W]
            cols = slice(half * W, (half + 1) * W)
            scr[hd, :, cols] = part if kind is None else part + bias_ref[hd, kind]

    def consume(hd, scr, j, pen=None, halves=BOTH):
        for half in halves:
            cols = slice(half * W, (half + 1) * W)
            ps, ms = [], []
            for quarter in range(2):
                qcols = slice(half * W + quarter * LANES, half * W + (quarter + 1) * LANES)
                st = scr[hd, :, qcols]
                cm = jnp.max(st, axis=0, keepdims=True)
                m_new = jnp.maximum(m_scr[hd, :, qcols], cm if pen is None else cm + pen)
                shift = m_new if pen is None else m_new - pen
                ps.append(jnp.exp2(st - shift).astype(BF16))
                ms.append(m_new)
            pv = _dot(vt_ref[0, hd, j], jnp.concatenate(ps, axis=1))
            m = m_scr[hd, :, cols]
            m_new = jnp.concatenate(ms, axis=1)
            m_scr[hd, :, cols] = m_new
            acc_scr[hd, :, cols] = jnp.exp2(m - m_new) * acc_scr[hd, :, cols] + pv

    def stage(produce=None, take=None):
        for hd in heads:
            if produce is not None:
                park(hd, *produce)
            if take is not None:
                consume(hd, *take)

    m_scr[...] = jnp.full(m_scr.shape, -jnp.inf, F32)
    acc_scr[...] = jnp.zeros(acc_scr.shape, F32)

    stage(produce=(sa_scr, 0))

    def pair_step(t, carry):
        stage(produce=(sb_scr, 2 * t + 1), take=(sa_scr, 2 * t))
        stage(produce=(sa_scr, 2 * t + 2), take=(sb_scr, 2 * t + 1))
        return carry

    done = 0
    for unroll in PAIR_UNROLLS:
        trips = (n_pairs - done) // unroll

        def body(u, carry, first=done, unroll=unroll):
            for r in range(unroll):
                pair_step(first + u * unroll + r, carry)
            return carry

        lax.fori_loop(0, trips, body, 0)
        done = done + trips * unroll

    second = (1,)
    stage(produce=(sb_scr, before, (PREV, None)), take=(sa_scr, parked, pen_parked))
    stage(produce=(sa_scr, 2 * i, (OWN, PREV)), take=(sb_scr, before, pen_before))
    stage(produce=(sb_scr, 2 * i + 1, (None, OWN), second), take=(sa_scr, 2 * i))
    stage(take=(sb_scr, 2 * i + 1, None, second))
    outs = [acc_scr[hd, 0:HEAD_DIM, :] / acc_scr[hd, HEAD_DIM:HEAD_DIM + 1, :] for hd in heads]
    o_ref[0] = (jnp.concatenate(outs, axis=0).T * gsa_ref[0]).astype(BF16)


def _attn_call(q, k_blk, vt_blk, bias_tiles, gsa):
    B, H, S, _ = q.shape
    nb = S // MOBA_BLOCK
    Tq = ROW_TILE
    G = ATTN_HEADS
    return pl.pallas_call(
        _attn_kernel,
        grid=(B, H // G, S // Tq),
        in_specs=[
            pl.BlockSpec((1, G, Tq, LANES), lambda b, hg, i: (b, hg, i, 0)),
            pl.BlockSpec((1, G, nb, MOBA_BLOCK, LANES), lambda b, hg, i: (b, hg, 0, 0, 0)),
            pl.BlockSpec((1, G, nb, V_ROWS, MOBA_BLOCK), lambda b, hg, i: (b, hg, 0, 0, 0)),
            pl.BlockSpec((G, 2, MOBA_BLOCK, MOBA_BLOCK), lambda b, hg, i: (hg, 0, 0, 0)),
            pl.BlockSpec((1, Tq, G * HEAD_DIM), lambda b, hg, i: (b, i, hg)),
        ],
        out_specs=pl.BlockSpec((1, Tq, G * HEAD_DIM), lambda b, hg, i: (b, i, hg)),
        out_shape=jax.ShapeDtypeStruct((B, S, H * HEAD_DIM), BF16),
        scratch_shapes=[
            pltpu.VMEM((G, MOBA_BLOCK, Tq), F32),
            pltpu.VMEM((G, MOBA_BLOCK, Tq), F32),
            pltpu.VMEM((G, 1, Tq), F32),
            pltpu.VMEM((G, V_ROWS, Tq), F32),
        ],
        compiler_params=pltpu.CompilerParams(
            dimension_semantics=("arbitrary", "arbitrary", "arbitrary"),
            vmem_limit_bytes=VMEM_LIMIT),
        name="moba_attn",
    )(q, k_blk, vt_blk, bias_tiles, gsa)


def _out_kernel(h_ref, ya_ref, ybc_ref, p_ref, wout_ref, plew_ref, gatew_ref,
                fg_ref, o_ref, *, final):
    y = _dot(ya_ref[...], wout_ref[0:A_WIDTH, :]) + _dot(ybc_ref[...], wout_ref[A_WIDTH:, :])
    h1 = h_ref[...] + y
    emb = _dot(p_ref[...].astype(BF16), plew_ref[...])
    gate = _sigmoid(_dot(h1.astype(BF16), gatew_ref[...]))
    h2 = h1 + emb * gate
    if final:
        h2 = h2 * lax.rsqrt(jnp.mean(h2 * h2, axis=-1, keepdims=True) + EPS) * fg_ref[...]
    o_ref[...] = h2


def _out_call(layer, h, ya, ybc, p, w_out, ple_w, gate_w, final_g, final):
    N = h.shape[0]
    T = OUT_TILE
    row = lambda width: pl.BlockSpec((T, width), lambda i: (i, 0))
    full = lambda shape: pl.BlockSpec((None,) + shape, lambda i: (layer,) + (0,) * len(shape))
    return pl.pallas_call(
        functools.partial(_out_kernel, final=final),
        grid=(N // T,),
        in_specs=[
            row(D_MODEL), row(A_WIDTH), row(B_WIDTH + C_WIDTH),
            pl.BlockSpec((None, T, D_PLE), lambda i: (layer, i, 0)),
            full((D_MODEL, D_MODEL)), full((D_PLE, D_MODEL)), full((D_MODEL, D_MODEL)),
            pl.BlockSpec((1, D_MODEL), lambda i: (0, 0)),
        ],
        out_specs=row(D_MODEL),
        out_shape=jax.ShapeDtypeStruct((N, D_MODEL), F32),
        compiler_params=pltpu.CompilerParams(
            dimension_semantics=("arbitrary",), vmem_limit_bytes=VMEM_LIMIT),
        name="out_ple",
    )(h, ya, ybc, p, w_out, ple_w, gate_w, final_g)


def _block_diag(w):
    G, c, d = w.shape
    eye = jnp.eye(G, dtype=w.dtype)
    return (eye[:, None, :, None] * w[:, :, None, :]).reshape(G * c, G * d)


def kernel(x, p, norm_g, w_in, w_out, rel_bias, pool_w, pool_scale, sgu_w, sgu_b,
           ple_w, ple_gate_w, final_g):
    B, S, D = x.shape
    depth = w_in.shape[0]
    N = B * S
    bias_tiles = _bias_tiles(rel_bias)
    w_in_b = w_in.astype(BF16)
    w_qt = jnp.transpose(w_in[:, :, Q0:Q0 + A_WIDTH] * Q_SCALE, (0, 2, 1)).astype(BF16)
    w_vt = jnp.transpose(w_in_b[:, :, V0:V0 + A_WIDTH], (0, 2, 1))
    pool_bd = jax.vmap(_block_diag)(pool_w).astype(BF16)
    sgu_wcat = jnp.transpose(jnp.tril(sgu_w), (0, 2, 1, 3)).reshape(
        depth, SGU_CHUNK, C_HEADS * SGU_CHUNK).astype(BF16)
    sgu_btile = jnp.repeat(jnp.transpose(sgu_b, (0, 2, 1)), HEAD_DIM, axis=2)
    w_out_b, ple_w_b, gate_w_b = (w.astype(BF16) for w in (w_out, ple_w, ple_gate_w))
    p_rows = p.reshape(depth, N, D_PLE)
    h = x
    for i in range(depth):
        q, k_blk, vt_blk, gsa, ybc = _proj_call(
            i, rel_bias, h, norm_g[:, None, :], w_qt, w_vt, w_in_b, pool_bd,
            pool_scale[:, None, :], sgu_wcat, sgu_btile)
        ya = _attn_call(q, k_blk, vt_blk, bias_tiles, gsa)
        h = _out_call(
            i, h.reshape(N, D), ya.reshape(N, A_WIDTH), ybc.reshape(N, B_WIDTH + C_WIDTH),
            p_rows, w_out_b, ple_w_b, gate_w_b, final_g[None, :],
            final=(i == depth - 1)).reshape(B, S, D)
    return h
```

```python
import functools
import math

import numpy as np
import jax
import jax.numpy as jnp
from jax import lax
from jax.experimental import pallas as pl
from jax.experimental.pallas import tpu as pltpu

D_MODEL = 1024
HEAD_DIM = 64
A_WIDTH = 512
A_HEADS = 8
MOBA_BLOCK = 256
MOBA_TOPK = 3
REL_BUCKETS = 32
REL_MAX_DIST = 128
B_WIDTH = 256
POOL_WINDOWS = (2, 4, 8, 16)
B_GROUP = 64
C_WIDTH = 256
C_HEADS = 4
SGU_CHUNK = 128
D_PLE = 256
D_IN = 3328
EPS = 1e-6
NEG = -1e30
LOG2E = math.log2(math.e)
Q_SCALE = HEAD_DIM ** -0.5 * LOG2E

MAX_WINDOW = max(POOL_WINDOWS)
ROW_BLOCKS = 2
ROW_TILE = ROW_BLOCKS * MOBA_BLOCK
OUT_TILE = 1024
LANES = 128
MASK_HI = HEAD_DIM
MASK_LO = HEAD_DIM + 32
V_ROWS = HEAD_DIM + 16
ATTN_HEADS = 4
PAIR_UNROLLS = (4, 2, 1)
VMEM_LIMIT = 48 * 1024 * 1024

_OFF = np.cumsum((0,) + (A_WIDTH,) * 4 + (B_WIDTH,) * 2 + (C_WIDTH,) * 3)
Q0, K0, V0, GA0, XB0, GB0, UC0, VC0, GC0, _ = (int(o) for o in _OFF)

BF16 = jnp.bfloat16
F32 = jnp.float32


def _bucket_thresholds():
    max_exact = REL_BUCKETS // 2
    n = np.arange(0, 4 * MOBA_BLOCK)
    nf = np.maximum(n, 1).astype(np.float64)
    large = max_exact + (np.log(nf / max_exact) / math.log(REL_MAX_DIST / max_exact)
                         * (REL_BUCKETS - max_exact)).astype(np.int64)
    large = np.minimum(large, REL_BUCKETS - 1)
    bucket = np.where(n < max_exact, n, large)
    return [int(np.argmax(bucket >= b)) for b in range(1, REL_BUCKETS)]


_THRESHOLDS = _bucket_thresholds()


def _sigmoid(x):
    return 0.5 * jnp.tanh(0.5 * x) + 0.5


def _silu(x):
    return x * _sigmoid(x)


def _split_bf16(x):
    hi = x.astype(BF16)
    lo = (x - hi.astype(F32)).astype(BF16)
    return hi, lo


def _dot(a, b):
    return jnp.dot(a, b, preferred_element_type=F32)


def _dot_nt(a, b):
    return lax.dot_general(a, b, (((1,), (1,)), ((), ())), preferred_element_type=F32)


def _bias_kernel(rb_ref, out_ref):
    h = pl.program_id(0)
    key = lax.broadcasted_iota(jnp.int32, (MOBA_BLOCK, MOBA_BLOCK), 0)
    qry = lax.broadcasted_iota(jnp.int32, (MOBA_BLOCK, MOBA_BLOCK), 1)
    for kind, shift in ((0, 0), (1, MOBA_BLOCK)):
        dist = qry - key + shift
        val = jnp.full((MOBA_BLOCK, MOBA_BLOCK), rb_ref[0, h], F32)
        for b in range(1, REL_BUCKETS):
            val = jnp.where(dist >= _THRESHOLDS[b - 1], rb_ref[b, h], val)
        val = val * LOG2E
        if kind == 0:
            val = jnp.where(dist >= 0, val, NEG)
        out_ref[0, kind] = val


def _bias_tiles(rel_bias):
    return pl.pallas_call(
        _bias_kernel,
        grid=(A_HEADS,),
        in_specs=[pl.BlockSpec(memory_space=pltpu.SMEM)],
        out_specs=pl.BlockSpec((1, 2, MOBA_BLOCK, MOBA_BLOCK), lambda h: (h, 0, 0, 0)),
        out_shape=jax.ShapeDtypeStruct((A_HEADS, 2, MOBA_BLOCK, MOBA_BLOCK), F32),
        name="bias_tiles",
    )(rel_bias)


def _proj_kernel(rb_ref, h_ref, ng_ref, wqt_ref, wvt_ref, win_ref, poolw_ref, pscale_ref,
                 sguw_ref, sgub_ref,
                 q_ref, k_ref, vt_ref, gsa_ref, ybc_ref,
                 kmt_scr, ext_scr, *, n_blocks):
    T = ROW_TILE
    W = MOBA_BLOCK
    s = pl.program_id(1)
    first_block = s * ROW_BLOCKS

    @pl.when(s == 0)
    def _():
        kmt_scr[...] = jnp.zeros_like(kmt_scr)
        ext_scr[0:MAX_WINDOW, :] = jnp.zeros((MAX_WINDOW, B_WIDTH), F32)

    hbs, zks = [], []
    for blk in range(ROW_BLOCKS):
        h = h_ref[0, blk * W:(blk + 1) * W, :]
        hn = h * lax.rsqrt(jnp.mean(h * h, axis=-1, keepdims=True) + EPS) * ng_ref[...]
        hbs.append(hn.astype(BF16))
        zks.append(_dot(hbs[-1], win_ref[:, K0:K0 + A_WIDTH]))
    hb = jnp.concatenate(hbs, axis=0)

    def proj(c0, width):
        return _dot(hb, win_ref[:, c0:c0 + width])

    zqt = _dot_nt(wqt_ref[...], hb)

    lane = lax.broadcasted_iota(jnp.int32, (1, LANES), 1)
    for blk, zk_blk in enumerate(zks):
        j = first_block + blk
        indicator = jnp.where((lane == MASK_HI + j) | (lane == MASK_LO + j), 1.0, 0.0)
        k_mean = jnp.mean(zk_blk, axis=0, keepdims=True)
        for pair in range(A_HEADS // 2):
            zk_pair = zk_blk[:, pair * LANES:(pair + 1) * LANES]
            zk_odd = pltpu.roll(zk_pair, HEAD_DIM, 1)
            k_ref[0, 2 * pair, blk] = jnp.where(lane < HEAD_DIM, zk_pair, indicator).astype(BF16)
            k_ref[0, 2 * pair + 1, blk] = jnp.where(lane < HEAD_DIM, zk_odd, indicator).astype(BF16)
            mean_pair = k_mean[:, pair * LANES:(pair + 1) * LANES]
            kmt_scr[2 * pair, pl.ds(j, 1), :] = mean_pair
            kmt_scr[2 * pair + 1, pl.ds(j, 1), :] = pltpu.roll(mean_pair, HEAD_DIM, 1)

    q_hi, q_lo = _split_bf16(zqt)
    gates = []
    for hd in range(A_HEADS):
        rows = slice(hd * HEAD_DIM, (hd + 1) * HEAD_DIM)
        km_hi, km_lo = _split_bf16(kmt_scr[hd][:, 0:HEAD_DIM])
        gates.append(_dot(km_hi, q_hi[rows]) + _dot(km_hi, q_lo[rows]) + _dot(km_lo, q_hi[rows]))

    zvt = _dot_nt(wvt_ref[...], hb)
    zga = proj(GA0, A_WIDTH)
    xb = proj(XB0, B_WIDTH)
    zgb = proj(GB0, B_WIDTH)
    vc = proj(VC0, C_WIDTH)
    zuc = proj(UC0, C_WIDTH)
    zgc = proj(GC0, C_WIDTH)

    jrow = lax.broadcasted_iota(jnp.int32, (n_blocks, T), 0)
    own = first_block + lax.broadcasted_iota(jnp.int32, (1, T), 1) // W
    for hd in range(A_HEADS):
        g = jnp.where(jrow < own, gates[hd], NEG)
        sel = jnp.zeros((n_blocks, T), jnp.bool_)
        for _ in range(MOBA_TOPK):
            m = jnp.max(g, axis=0, keepdims=True)
            idx = jnp.min(jnp.where(g == m, jrow, n_blocks), axis=0, keepdims=True)
            pick = jrow == idx
            sel = sel | (pick & (m > NEG * 0.5))
            g = jnp.where(pick, -jnp.inf, g)
        far_bias = rb_ref[REL_BUCKETS - 1, hd] * LOG2E
        term = jnp.where(sel, jnp.where(jrow <= own - 2, far_bias, 0.0), NEG)
        term = jnp.where(jrow == own, 0.0, term)
        term_hi = term.astype(BF16).astype(F32)
        q_aug_t = jnp.concatenate(
            [zqt[hd * HEAD_DIM:(hd + 1) * HEAD_DIM], term_hi, term - term_hi], axis=0)
        q_ref[0, hd] = q_aug_t.T.astype(BF16)

    tail_row = lax.broadcasted_iota(jnp.int32, (V_ROWS - HEAD_DIM, W), 0)
    tail = jnp.where(tail_row == 0, 1.0, 0.0)
    for hd in range(A_HEADS):
        for blk in range(ROW_BLOCKS):
            vt_ref[0, hd, blk] = jnp.concatenate(
                [zvt[hd * HEAD_DIM:(hd + 1) * HEAD_DIM, blk * W:(blk + 1) * W], tail],
                axis=0).astype(BF16)

    gsa_ref[0] = _silu(zga)

    ext_scr[MAX_WINDOW:MAX_WINDOW + T, :] = xb
    lane_b = lax.broadcasted_iota(jnp.int32, (1, B_WIDTH), 1)
    win = jnp.zeros((1, B_WIDTH), F32)
    for gi, w in enumerate(POOL_WINDOWS):
        win = jnp.where(lane_b // B_GROUP == gi, float(w), win)
    halves = []
    for half, (w_small, w_big) in enumerate(zip(POOL_WINDOWS[0::2], POOL_WINDOWS[1::2])):
        cols = slice(half * LANES, (half + 1) * LANES)
        run = xb[:, cols]
        for lag in range(1, w_small):
            run = run + ext_scr[MAX_WINDOW - lag:MAX_WINDOW - lag + T, cols]
        small = run
        for lag in range(w_small, w_big):
            run = run + ext_scr[MAX_WINDOW - lag:MAX_WINDOW - lag + T, cols]
        halves.append(jnp.where(lane < B_GROUP, small, run))
    wsum = jnp.concatenate(halves, axis=1)
    pos = (s * T + lax.broadcasted_iota(jnp.int32, (T, 1), 0) + 1).astype(F32)
    pooled = wsum / jnp.minimum(pos, win) - xb
    ext_scr[0:MAX_WINDOW, :] = xb[T - MAX_WINDOW:T, :]
    mixed_b = _dot(pooled.astype(BF16), poolw_ref[...]) * pscale_ref[...]
    ybc_ref[0, :, 0:B_WIDTH] = (mixed_b * _silu(zgb)).astype(BF16)

    mu = jnp.mean(vc, axis=-1, keepdims=True)
    cen = vc - mu
    var = jnp.mean(cen * cen, axis=-1, keepdims=True)
    vn = cen * lax.rsqrt(var + EPS)
    rows = lax.broadcasted_iota(jnp.int32, (C_HEADS * SGU_CHUNK, C_WIDTH), 0)
    cols = lax.broadcasted_iota(jnp.int32, (C_HEADS * SGU_CHUNK, C_WIDTH), 1)
    head_sel = (rows // SGU_CHUNK) == (cols // HEAD_DIM)
    ug = zuc * _silu(zgc)
    for c in range(T // SGU_CHUNK):
        vn_c = vn[c * SGU_CHUNK:(c + 1) * SGU_CHUNK]
        stack = jnp.where(head_sel, jnp.concatenate([vn_c] * C_HEADS, axis=0), 0.0)
        mixed_c = _dot(sguw_ref[...], stack.astype(BF16)) + sgub_ref[...]
        ybc_ref[0, c * SGU_CHUNK:(c + 1) * SGU_CHUNK, B_WIDTH:B_WIDTH + C_WIDTH] = (
            ug[c * SGU_CHUNK:(c + 1) * SGU_CHUNK] * mixed_c).astype(BF16)


def _proj_call(layer, rel_bias, h, norm_g, w_qt, w_vt, w_in, pool_bd, pool_scale, sgu_wcat,
               sgu_btile):
    B, S, _ = h.shape
    T = ROW_TILE
    nb = S // MOBA_BLOCK
    assert nb <= MASK_LO - MASK_HI and MASK_LO + nb <= LANES
    full = lambda shape: pl.BlockSpec((None,) + shape, lambda b, s: (layer,) + (0,) * len(shape))
    return pl.pallas_call(
        functools.partial(_proj_kernel, n_blocks=nb),
        grid=(B, S // T),
        in_specs=[
            pl.BlockSpec(memory_space=pltpu.SMEM),
            pl.BlockSpec((1, T, D_MODEL), lambda b, s: (b, s, 0)),
            full((1, D_MODEL)),
            full((A_WIDTH, D_MODEL)),
            full((A_WIDTH, D_MODEL)),
            full((D_MODEL, D_IN)),
            full((B_WIDTH, B_WIDTH)),
            full((1, B_WIDTH)),
            full((SGU_CHUNK, C_HEADS * SGU_CHUNK)),
            full((SGU_CHUNK, C_WIDTH)),
        ],
        out_specs=[
            pl.BlockSpec((1, A_HEADS, T, LANES), lambda b, s: (b, 0, s, 0)),
            pl.BlockSpec((1, A_HEADS, ROW_BLOCKS, MOBA_BLOCK, LANES), lambda b, s: (b, 0, s, 0, 0)),
            pl.BlockSpec((1, A_HEADS, ROW_BLOCKS, V_ROWS, MOBA_BLOCK), lambda b, s: (b, 0, s, 0, 0)),
            pl.BlockSpec((1, T, A_WIDTH), lambda b, s: (b, s, 0)),
            pl.BlockSpec((1, T, B_WIDTH + C_WIDTH), lambda b, s: (b, s, 0)),
        ],
        out_shape=[
            jax.ShapeDtypeStruct((B, A_HEADS, S, LANES), BF16),
            jax.ShapeDtypeStruct((B, A_HEADS, nb, MOBA_BLOCK, LANES), BF16),
            jax.ShapeDtypeStruct((B, A_HEADS, nb, V_ROWS, MOBA_BLOCK), BF16),
            jax.ShapeDtypeStruct((B, S, A_WIDTH), F32),
            jax.ShapeDtypeStruct((B, S, B_WIDTH + C_WIDTH), BF16),
        ],
        scratch_shapes=[
            pltpu.VMEM((A_HEADS, nb, LANES), F32),
            pltpu.VMEM((MAX_WINDOW + T, B_WIDTH), F32),
        ],
        compiler_params=pltpu.CompilerParams(
            dimension_semantics=("arbitrary", "arbitrary"), vmem_limit_bytes=VMEM_LIMIT),
        name="proj_mix",
    )(rel_bias, h, norm_g, w_qt, w_vt, w_in, pool_bd, pool_scale, sgu_wcat, sgu_btile)


def _attn_kernel(q_ref, qnext_ref, k_ref, vt_ref, bias_ref, gsa_ref, o_ref,
                 sa_scr, sb_scr, m_scr, acc_scr):
    i = pl.program_id(2)
    W = MOBA_BLOCK
    n_pairs = jnp.maximum(i - 1, 0)
    parked = 2 * n_pairs
    before = jnp.maximum(2 * i - 1, 0)
    pen_parked = jnp.where(i >= 1, 0.0, NEG)
    pen_before = jnp.where(i >= 1, 0.0, NEG)
    heads = range(ATTN_HEADS)
    OWN, PREV = 0, 1

    BOTH = (0, 1)

    def park(hd, scr, j, bias_kinds=(None, None), halves=BOTH, queries=q_ref):
        rows = slice(halves[0] * W, (halves[-1] + 1) * W)
        st = _dot_nt(k_ref[0, hd, j], queries[0, hd, rows, :])
        for n, half in enumerate(halves):
            kind = bias_kinds[half]
            part = st[:, n * W:(n + 1) * W]
            cols = slice(half * W, (half + 1) * W)
            scr[hd, :, cols] = part if kind is None else part + bias_ref[hd, kind]

    def consume(hd, scr, j, pen=None, halves=BOTH):
        for half in halves:
            cols = slice(half * W, (half + 1) * W)
            ps, ms = [], []
            for quarter in range(2):
                qcols = slice(half * W + quarter * LANES, half * W + (quarter + 1) * LANES)
                st = scr[hd, :, qcols]
                cm = jnp.max(st, axis=0, keepdims=True)
                m_new = jnp.maximum(m_scr[hd, :, qcols], cm if pen is None else cm + pen)
                shift = m_new if pen is None else m_new - pen
                ps.append(jnp.exp2(st - shift).astype(BF16))
                ms.append(m_new)
            pv = _dot(vt_ref[0, hd, j], jnp.concatenate(ps, axis=1))
            m = m_scr[hd, :, cols]
            m_new = jnp.concatenate(ms, axis=1)
            m_scr[hd, :, cols] = m_new
            acc_scr[hd, :, cols] = jnp.exp2(m - m_new) * acc_scr[hd, :, cols] + pv

    def stage(produce=None, take=None):
        for hd in heads:
            if produce is not None:
                park(hd, *produce)
            if take is not None:
                consume(hd, *take)

    m_scr[...] = jnp.full(m_scr.shape, -jnp.inf, F32)
    acc_scr[...] = jnp.zeros(acc_scr.shape, F32)

    @pl.when(i == 0)
    def _():
        stage(produce=(sa_scr, 0))

    def pair_step(t, carry):
        stage(produce=(sb_scr, 2 * t + 1), take=(sa_scr, 2 * t))
        stage(produce=(sa_scr, 2 * t + 2), take=(sb_scr, 2 * t + 1))
        return carry

    done = 0
    for unroll in PAIR_UNROLLS:
        trips = (n_pairs - done) // unroll

        def body(u, carry, first=done, unroll=unroll):
            for r in range(unroll):
                pair_step(first + u * unroll + r, carry)
            return carry

        lax.fori_loop(0, trips, body, 0)
        done = done + trips * unroll

    second = (1,)
    stage(produce=(sb_scr, before, (PREV, None)), take=(sa_scr, parked, pen_parked))
    stage(produce=(sa_scr, 2 * i, (OWN, PREV)), take=(sb_scr, before, pen_before))
    stage(produce=(sb_scr, 2 * i + 1, (None, OWN), second), take=(sa_scr, 2 * i))
    stage(produce=(sa_scr, 0, (None, None), BOTH, qnext_ref), take=(sb_scr, 2 * i + 1, None, second))
    outs = [acc_scr[hd, 0:HEAD_DIM, :] / acc_scr[hd, HEAD_DIM:HEAD_DIM + 1, :] for hd in heads]
    o_ref[0] = (jnp.concatenate(outs, axis=0).T * gsa_ref[0]).astype(BF16)


def _attn_call(q, k_blk, vt_blk, bias_tiles, gsa):
    B, H, S, _ = q.shape
    nb = S // MOBA_BLOCK
    Tq = ROW_TILE
    G = ATTN_HEADS
    last = S // Tq - 1
    return pl.pallas_call(
        _attn_kernel,
        grid=(B, H // G, S // Tq),
        in_specs=[
            pl.BlockSpec((1, G, Tq, LANES), lambda b, hg, i: (b, hg, i, 0)),
            pl.BlockSpec((1, G, Tq, LANES), lambda b, hg, i: (b, hg, jnp.minimum(i + 1, last), 0)),
            pl.BlockSpec((1, G, nb, MOBA_BLOCK, LANES), lambda b, hg, i: (b, hg, 0, 0, 0)),
            pl.BlockSpec((1, G, nb, V_ROWS, MOBA_BLOCK), lambda b, hg, i: (b, hg, 0, 0, 0)),
            pl.BlockSpec((G, 2, MOBA_BLOCK, MOBA_BLOCK), lambda b, hg, i: (hg, 0, 0, 0)),
            pl.BlockSpec((1, Tq, G * HEAD_DIM), lambda b, hg, i: (b, i, hg)),
        ],
        out_specs=pl.BlockSpec((1, Tq, G * HEAD_DIM), lambda b, hg, i: (b, i, hg)),
        out_shape=jax.ShapeDtypeStruct((B, S, H * HEAD_DIM), BF16),
        scratch_shapes=[
            pltpu.VMEM((G, MOBA_BLOCK, Tq), F32),
            pltpu.VMEM((G, MOBA_BLOCK, Tq), F32),
            pltpu.VMEM((G, 1, Tq), F32),
            pltpu.VMEM((G, V_ROWS, Tq), F32),
        ],
        compiler_params=pltpu.CompilerParams(
            dimension_semantics=("arbitrary", "arbitrary", "arbitrary"),
            vmem_limit_bytes=VMEM_LIMIT),
        name="moba_attn",
    )(q, q, k_blk, vt_blk, bias_tiles, gsa)


def _out_kernel(h_ref, ya_ref, ybc_ref, p_ref, wout_ref, plew_ref, gatew_ref,
                fg_ref, o_ref, *, final):
    y = _dot(ya_ref[...], wout_ref[0:A_WIDTH, :]) + _dot(ybc_ref[...], wout_ref[A_WIDTH:, :])
    h1 = h_ref[...] + y
    emb = _dot(p_ref[...].astype(BF16), plew_ref[...])
    gate = _sigmoid(_dot(h1.astype(BF16), gatew_ref[...]))
    h2 = h1 + emb * gate
    if final:
        h2 = h2 * lax.rsqrt(jnp.mean(h2 * h2, axis=-1, keepdims=True) + EPS) * fg_ref[...]
    o_ref[...] = h2


def _out_call(layer, h, ya, ybc, p, w_out, ple_w, gate_w, final_g, final):
    N = h.shape[0]
    T = OUT_TILE
    row = lambda width: pl.BlockSpec((T, width), lambda i: (i, 0))
    full = lambda shape: pl.BlockSpec((None,) + shape, lambda i: (layer,) + (0,) * len(shape))
    return pl.pallas_call(
        functools.partial(_out_kernel, final=final),
        grid=(N // T,),
        in_specs=[
            row(D_MODEL), row(A_WIDTH), row(B_WIDTH + C_WIDTH),
            pl.BlockSpec((None, T, D_PLE), lambda i: (layer, i, 0)),
            full((D_MODEL, D_MODEL)), full((D_PLE, D_MODEL)), full((D_MODEL, D_MODEL)),
            pl.BlockSpec((1, D_MODEL), lambda i: (0, 0)),
        ],
        out_specs=row(D_MODEL),
        out_shape=jax.ShapeDtypeStruct((N, D_MODEL), F32),
        compiler_params=pltpu.CompilerParams(
            dimension_semantics=("arbitrary",), vmem_limit_bytes=VMEM_LIMIT),
        name="out_ple",
    )(h, ya, ybc, p, w_out, ple_w, gate_w, final_g)


def _block_diag(w):
    G, c, d = w.shape
    eye = jnp.eye(G, dtype=w.dtype)
    return (eye[:, None, :, None] * w[:, :, None, :]).reshape(G * c, G * d)


def kernel(x, p, norm_g, w_in, w_out, rel_bias, pool_w, pool_scale, sgu_w, sgu_b,
           ple_w, ple_gate_w, final_g):
    B, S, D = x.shape
    depth = w_in.shape[0]
    N = B * S
    bias_tiles = _bias_tiles(rel_bias)
    w_in_b = w_in.astype(BF16)
    w_qt = jnp.transpose(w_in[:, :, Q0:Q0 + A_WIDTH] * Q_SCALE, (0, 2, 1)).astype(BF16)
    w_vt = jnp.transpose(w_in_b[:, :, V0:V0 + A_WIDTH], (0, 2, 1))
    pool_bd = jax.vmap(_block_diag)(pool_w).astype(BF16)
    sgu_wcat = jnp.transpose(jnp.tril(sgu_w), (0, 2, 1, 3)).reshape(
        depth, SGU_CHUNK, C_HEADS * SGU_CHUNK).astype(BF16)
    sgu_btile = jnp.repeat(jnp.transpose(sgu_b, (0, 2, 1)), HEAD_DIM, axis=2)
    w_out_b, ple_w_b, gate_w_b = (w.astype(BF16) for w in (w_out, ple_w, ple_gate_w))
    p_rows = p.reshape(depth, N, D_PLE)
    h = x
    for i in range(depth):
        q, k_blk, vt_blk, gsa, ybc = _proj_call(
            i, rel_bias, h, norm_g[:, None, :], w_qt, w_vt, w_in_b, pool_bd,
            pool_scale[:, None, :], sgu_wcat, sgu_btile)
        ya = _attn_call(q, k_blk, vt_blk, bias_tiles, gsa)
        h = _out_call(
            i, h.reshape(N, D), ya.reshape(N, A_WIDTH), ybc.reshape(N, B_WIDTH + C_WIDTH),
            p_rows, w_out_b, ple_w_b, gate_w_b, final_g[None, :],
            final=(i == depth - 1)).reshape(B, S, D)
    return h
```

```python
import functools
import math

import numpy as np
import jax
import jax.numpy as jnp
from jax import lax
from jax.experimental import pallas as pl
from jax.experimental.pallas import tpu as pltpu

D_MODEL = 1024
HEAD_DIM = 64
A_WIDTH = 512
A_HEADS = 8
MOBA_BLOCK = 256
MOBA_TOPK = 3
REL_BUCKETS = 32
REL_MAX_DIST = 128
B_WIDTH = 256
POOL_WINDOWS = (2, 4, 8, 16)
B_GROUP = 64
C_WIDTH = 256
C_HEADS = 4
SGU_CHUNK = 128
D_PLE = 256
D_IN = 3328
EPS = 1e-6
NEG = -1e30
LOG2E = math.log2(math.e)
Q_SCALE = HEAD_DIM ** -0.5 * LOG2E

MAX_WINDOW = max(POOL_WINDOWS)
ROW_BLOCKS = 2
ROW_TILE = ROW_BLOCKS * MOBA_BLOCK
OUT_TILE = 1024
LANES = 128
MASK_HI = HEAD_DIM
MASK_LO = HEAD_DIM + 32
V_ROWS = HEAD_DIM + 16
ATTN_HEADS = 4
PAIR_UNROLLS = (4, 2, 1)
VMEM_LIMIT = 48 * 1024 * 1024

_OFF = np.cumsum((0,) + (A_WIDTH,) * 4 + (B_WIDTH,) * 2 + (C_WIDTH,) * 3)
Q0, K0, V0, GA0, XB0, GB0, UC0, VC0, GC0, _ = (int(o) for o in _OFF)

BF16 = jnp.bfloat16
F32 = jnp.float32


def _bucket_thresholds():
    max_exact = REL_BUCKETS // 2
    n = np.arange(0, 4 * MOBA_BLOCK)
    nf = np.maximum(n, 1).astype(np.float64)
    large = max_exact + (np.log(nf / max_exact) / math.log(REL_MAX_DIST / max_exact)
                         * (REL_BUCKETS - max_exact)).astype(np.int64)
    large = np.minimum(large, REL_BUCKETS - 1)
    bucket = np.where(n < max_exact, n, large)
    return [int(np.argmax(bucket >= b)) for b in range(1, REL_BUCKETS)]


_THRESHOLDS = _bucket_thresholds()


def _sigmoid(x):
    return 0.5 * jnp.tanh(0.5 * x) + 0.5


def _silu(x):
    return x * _sigmoid(x)


def _split_bf16(x):
    hi = x.astype(BF16)
    lo = (x - hi.astype(F32)).astype(BF16)
    return hi, lo


def _dot(a, b):
    return jnp.dot(a, b, preferred_element_type=F32)


def _dot_tn(a, b):
    return lax.dot_general(a, b, (((0,), (1,)), ((), ())), preferred_element_type=F32)


def _dot_nt(a, b):
    return lax.dot_general(a, b, (((1,), (1,)), ((), ())), preferred_element_type=F32)


def _bias_kernel(rb_ref, out_ref):
    h = pl.program_id(0)
    key = lax.broadcasted_iota(jnp.int32, (MOBA_BLOCK, MOBA_BLOCK), 0)
    qry = lax.broadcasted_iota(jnp.int32, (MOBA_BLOCK, MOBA_BLOCK), 1)
    for kind, shift in ((0, 0), (1, MOBA_BLOCK)):
        dist = qry - key + shift
        val = jnp.full((MOBA_BLOCK, MOBA_BLOCK), rb_ref[0, h], F32)
        for b in range(1, REL_BUCKETS):
            val = jnp.where(dist >= _THRESHOLDS[b - 1], rb_ref[b, h], val)
        val = val * LOG2E
        if kind == 0:
            val = jnp.where(dist >= 0, val, NEG)
        out_ref[0, kind] = val


def _bias_tiles(rel_bias):
    return pl.pallas_call(
        _bias_kernel,
        grid=(A_HEADS,),
        in_specs=[pl.BlockSpec(memory_space=pltpu.SMEM)],
        out_specs=pl.BlockSpec((1, 2, MOBA_BLOCK, MOBA_BLOCK), lambda h: (h, 0, 0, 0)),
        out_shape=jax.ShapeDtypeStruct((A_HEADS, 2, MOBA_BLOCK, MOBA_BLOCK), F32),
        name="bias_tiles",
    )(rel_bias)


def _proj_kernel(rb_ref, h_ref, ng_ref, win_ref, poolw_ref, pscale_ref,
                 sguw_ref, sgub_ref,
                 q_ref, k_ref, vt_ref, gsa_ref, ybc_ref,
                 kmt_scr, ext_scr, *, n_blocks):
    T = ROW_TILE
    W = MOBA_BLOCK
    s = pl.program_id(1)
    first_block = s * ROW_BLOCKS

    @pl.when(s == 0)
    def _():
        kmt_scr[...] = jnp.zeros_like(kmt_scr)
        ext_scr[0:MAX_WINDOW, :] = jnp.zeros((MAX_WINDOW, B_WIDTH), F32)

    hbs, zks = [], []
    for blk in range(ROW_BLOCKS):
        h = h_ref[0, blk * W:(blk + 1) * W, :]
        hn = h * lax.rsqrt(jnp.mean(h * h, axis=-1, keepdims=True) + EPS) * ng_ref[...]
        hbs.append(hn.astype(BF16))
        zks.append(_dot(hbs[-1], win_ref[:, K0:K0 + A_WIDTH]))
    hb = jnp.concatenate(hbs, axis=0)

    def proj(c0, width):
        return _dot(hb, win_ref[:, c0:c0 + width])

    zqt = _dot_tn(win_ref[:, Q0:Q0 + A_WIDTH], hb)

    lane = lax.broadcasted_iota(jnp.int32, (1, LANES), 1)
    for blk, zk_blk in enumerate(zks):
        j = first_block + blk
        indicator = jnp.where((lane == MASK_HI + j) | (lane == MASK_LO + j), 1.0, 0.0)
        k_mean = jnp.mean(zk_blk, axis=0, keepdims=True)
        for pair in range(A_HEADS // 2):
            zk_pair = zk_blk[:, pair * LANES:(pair + 1) * LANES]
            zk_odd = pltpu.roll(zk_pair, HEAD_DIM, 1)
            k_ref[0, 2 * pair, blk] = jnp.where(lane < HEAD_DIM, zk_pair, indicator).astype(BF16)
            k_ref[0, 2 * pair + 1, blk] = jnp.where(lane < HEAD_DIM, zk_odd, indicator).astype(BF16)
            mean_pair = k_mean[:, pair * LANES:(pair + 1) * LANES]
            kmt_scr[2 * pair, pl.ds(j, 1), :] = mean_pair
            kmt_scr[2 * pair + 1, pl.ds(j, 1), :] = pltpu.roll(mean_pair, HEAD_DIM, 1)

    q_hi, q_lo = _split_bf16(zqt)
    gates = []
    for hd in range(A_HEADS):
        rows = slice(hd * HEAD_DIM, (hd + 1) * HEAD_DIM)
        km_hi, km_lo = _split_bf16(kmt_scr[hd][:, 0:HEAD_DIM])
        gates.append(_dot(km_hi, q_hi[rows]) + _dot(km_hi, q_lo[rows]) + _dot(km_lo, q_hi[rows]))

    vc = proj(VC0, C_WIDTH)
    zuc = proj(UC0, C_WIDTH)
    zgc = proj(GC0, C_WIDTH)
    xb = proj(XB0, B_WIDTH)
    zgb = proj(GB0, B_WIDTH)
    zga = proj(GA0, A_WIDTH)
    zvt = _dot_tn(win_ref[:, V0:V0 + A_WIDTH], hb)

    jrow = lax.broadcasted_iota(jnp.int32, (n_blocks, T), 0)
    own = first_block + lax.broadcasted_iota(jnp.int32, (1, T), 1) // W
    for hd in range(A_HEADS):
        g = jnp.where(jrow < own, gates[hd], NEG)
        sel = jnp.zeros((n_blocks, T), jnp.bool_)
        for _ in range(MOBA_TOPK):
            m = jnp.max(g, axis=0, keepdims=True)
            idx = jnp.min(jnp.where(g == m, jrow, n_blocks), axis=0, keepdims=True)
            pick = jrow == idx
            sel = sel | (pick & (m > NEG * 0.5))
            g = jnp.where(pick, -jnp.inf, g)
        far_bias = rb_ref[REL_BUCKETS - 1, hd] * LOG2E
        term = jnp.where(sel, jnp.where(jrow <= own - 2, far_bias, 0.0), NEG)
        term = jnp.where(jrow == own, 0.0, term)
        term_hi = term.astype(BF16).astype(F32)
        q_aug_t = jnp.concatenate(
            [zqt[hd * HEAD_DIM:(hd + 1) * HEAD_DIM], term_hi, term - term_hi], axis=0)
        q_ref[0, hd] = q_aug_t.T.astype(BF16)

    tail_row = lax.broadcasted_iota(jnp.int32, (V_ROWS - HEAD_DIM, W), 0)
    tail = jnp.where(tail_row == 0, 1.0, 0.0)
    for hd in range(A_HEADS):
        for blk in range(ROW_BLOCKS):
            vt_ref[0, hd, blk] = jnp.concatenate(
                [zvt[hd * HEAD_DIM:(hd + 1) * HEAD_DIM, blk * W:(blk + 1) * W], tail],
                axis=0).astype(BF16)

    gsa_ref[0] = _silu(zga)

    ext_scr[MAX_WINDOW:MAX_WINDOW + T, :] = xb
    lane_b = lax.broadcasted_iota(jnp.int32, (1, B_WIDTH), 1)
    win = jnp.zeros((1, B_WIDTH), F32)
    for gi, w in enumerate(POOL_WINDOWS):
        win = jnp.where(lane_b // B_GROUP == gi, float(w), win)
    halves = []
    for half, (w_small, w_big) in enumerate(zip(POOL_WINDOWS[0::2], POOL_WINDOWS[1::2])):
        cols = slice(half * LANES, (half + 1) * LANES)
        run = xb[:, cols]
        for lag in range(1, w_small):
            run = run + ext_scr[MAX_WINDOW - lag:MAX_WINDOW - lag + T, cols]
        small = run
        for lag in range(w_small, w_big):
            run = run + ext_scr[MAX_WINDOW - lag:MAX_WINDOW - lag + T, cols]
        halves.append(jnp.where(lane < B_GROUP, small, run))
    wsum = jnp.concatenate(halves, axis=1)
    pos = (s * T + lax.broadcasted_iota(jnp.int32, (T, 1), 0) + 1).astype(F32)
    pooled = wsum / jnp.minimum(pos, win) - xb
    ext_scr[0:MAX_WINDOW, :] = xb[T - MAX_WINDOW:T, :]
    mixed_b = _dot(pooled.astype(BF16), poolw_ref[...]) * pscale_ref[...]
    ybc_ref[0, :, 0:B_WIDTH] = (mixed_b * _silu(zgb)).astype(BF16)

    mu = jnp.mean(vc, axis=-1, keepdims=True)
    cen = vc - mu
    var = jnp.mean(cen * cen, axis=-1, keepdims=True)
    vn = cen * lax.rsqrt(var + EPS)
    rows = lax.broadcasted_iota(jnp.int32, (C_HEADS * SGU_CHUNK, C_WIDTH), 0)
    cols = lax.broadcasted_iota(jnp.int32, (C_HEADS * SGU_CHUNK, C_WIDTH), 1)
    head_sel = (rows // SGU_CHUNK) == (cols // HEAD_DIM)
    ug = zuc * _silu(zgc)
    for c in range(T // SGU_CHUNK):
        vn_c = vn[c * SGU_CHUNK:(c + 1) * SGU_CHUNK]
        stack = jnp.where(head_sel, jnp.concatenate([vn_c] * C_HEADS, axis=0), 0.0)
        mixed_c = _dot(sguw_ref[...], stack.astype(BF16)) + sgub_ref[...]
        ybc_ref[0, c * SGU_CHUNK:(c + 1) * SGU_CHUNK, B_WIDTH:B_WIDTH + C_WIDTH] = (
            ug[c * SGU_CHUNK:(c + 1) * SGU_CHUNK] * mixed_c).astype(BF16)


def _proj_call(layer, rel_bias, h, norm_g, w_in, pool_bd, pool_scale, sgu_wcat, sgu_btile):
    B, S, _ = h.shape
    T = ROW_TILE
    nb = S // MOBA_BLOCK
    assert nb <= MASK_LO - MASK_HI and MASK_LO + nb <= LANES
    full = lambda shape: pl.BlockSpec((None,) + shape, lambda b, s: (layer,) + (0,) * len(shape))
    return pl.pallas_call(
        functools.partial(_proj_kernel, n_blocks=nb),
        grid=(B, S // T),
        in_specs=[
            pl.BlockSpec(memory_space=pltpu.SMEM),
            pl.BlockSpec((1, T, D_MODEL), lambda b, s: (b, s, 0)),
            full((1, D_MODEL)),
            full((D_MODEL, D_IN)),
            full((B_WIDTH, B_WIDTH)),
            full((1, B_WIDTH)),
            full((SGU_CHUNK, C_HEADS * SGU_CHUNK)),
            full((SGU_CHUNK, C_WIDTH)),
        ],
        out_specs=[
            pl.BlockSpec((1, A_HEADS, T, LANES), lambda b, s: (b, 0, s, 0)),
            pl.BlockSpec((1, A_HEADS, ROW_BLOCKS, MOBA_BLOCK, LANES), lambda b, s: (b, 0, s, 0, 0)),
            pl.BlockSpec((1, A_HEADS, ROW_BLOCKS, V_ROWS, MOBA_BLOCK), lambda b, s: (b, 0, s, 0, 0)),
            pl.BlockSpec((1, T, A_WIDTH), lambda b, s: (b, s, 0)),
            pl.BlockSpec((1, T, B_WIDTH + C_WIDTH), lambda b, s: (b, s, 0)),
        ],
        out_shape=[
            jax.ShapeDtypeStruct((B, A_HEADS, S, LANES), BF16),
            jax.ShapeDtypeStruct((B, A_HEADS, nb, MOBA_BLOCK, LANES), BF16),
            jax.ShapeDtypeStruct((B, A_HEADS, nb, V_ROWS, MOBA_BLOCK), BF16),
            jax.ShapeDtypeStruct((B, S, A_WIDTH), F32),
            jax.ShapeDtypeStruct((B, S, B_WIDTH + C_WIDTH), BF16),
        ],
        scratch_shapes=[
            pltpu.VMEM((A_HEADS, nb, LANES), F32),
            pltpu.VMEM((MAX_WINDOW + T, B_WIDTH), F32),
        ],
        compiler_params=pltpu.CompilerParams(
            dimension_semantics=("arbitrary", "arbitrary"), vmem_limit_bytes=VMEM_LIMIT),
        name="proj_mix",
    )(rel_bias, h, norm_g, w_in, pool_bd, pool_scale, sgu_wcat, sgu_btile)


def _attn_kernel(q_ref, qnext_ref, k_ref, vt_ref, bias_ref, gsa_ref, o_ref,
                 sa_scr, sb_scr, m_scr, acc_scr):
    i = pl.program_id(2)
    W = MOBA_BLOCK
    n_pairs = jnp.maximum(i - 1, 0)
    parked = 2 * n_pairs
    before = jnp.maximum(2 * i - 1, 0)
    pen_parked = jnp.where(i >= 1, 0.0, NEG)
    pen_before = jnp.where(i >= 1, 0.0, NEG)
    heads = range(ATTN_HEADS)
    OWN, PREV = 0, 1

    BOTH = (0, 1)

    def park(hd, scr, j, bias_kinds=(None, None), halves=BOTH, queries=q_ref):
        rows = slice(halves[0] * W, (halves[-1] + 1) * W)
        st = _dot_nt(k_ref[0, hd, j], queries[0, hd, rows, :])
        for n, half in enumerate(halves):
            kind = bias_kinds[half]
            part = st[:, n * W:(n + 1) * W]
            cols = slice(half * W, (half + 1) * W)
            scr[hd, :, cols] = part if kind is None else part + bias_ref[hd, kind]

    def consume(hd, scr, j, pen=None, halves=BOTH):
        for half in halves:
            cols = slice(half * W, (half + 1) * W)
            ps, ms = [], []
            for quarter in range(2):
                qcols = slice(half * W + quarter * LANES, half * W + (quarter + 1) * LANES)
                st = scr[hd, :, qcols]
                cm = jnp.max(st, axis=0, keepdims=True)
                m_new = jnp.maximum(m_scr[hd, :, qcols], cm if pen is None else cm + pen)
                shift = m_new if pen is None else m_new - pen
                ps.append(jnp.exp2(st - shift).astype(BF16))
                ms.append(m_new)
            pv = _dot(vt_ref[0, hd, j], jnp.concatenate(ps, axis=1))
            m = m_scr[hd, :, cols]
            m_new = jnp.concatenate(ms, axis=1)
            m_scr[hd, :, cols] = m_new
            acc_scr[hd, :, cols] = jnp.exp2(m - m_new) * acc_scr[hd, :, cols] + pv

    def stage(produce=None, take=None):
        for hd in heads:
            if produce is not None:
                park(hd, *produce)
            if take is not None:
                consume(hd, *take)

    m_scr[...] = jnp.full(m_scr.shape, -jnp.inf, F32)
    acc_scr[...] = jnp.zeros(acc_scr.shape, F32)

    @pl.when(i == 0)
    def _():
        stage(produce=(sa_scr, 0))

    def pair_step(t, carry):
        stage(produce=(sb_scr, 2 * t + 1), take=(sa_scr, 2 * t))
        stage(produce=(sa_scr, 2 * t + 2), take=(sb_scr, 2 * t + 1))
        return carry

    done = 0
    for unroll in PAIR_UNROLLS:
        trips = (n_pairs - done) // unroll

        def body(u, carry, first=done, unroll=unroll):
            for r in range(unroll):
                pair_step(first + u * unroll + r, carry)
            return carry

        lax.fori_loop(0, trips, body, 0)
        done = done + trips * unroll

    second = (1,)
    stage(produce=(sb_scr, before, (PREV, None)), take=(sa_scr, parked, pen_parked))
    stage(produce=(sa_scr, 2 * i, (OWN, PREV)), take=(sb_scr, before, pen_before))
    stage(produce=(sb_scr, 2 * i + 1, (None, OWN), second), take=(sa_scr, 2 * i))
    stage(produce=(sa_scr, 0, (None, None), BOTH, qnext_ref), take=(sb_scr, 2 * i + 1, None, second))
    outs = [acc_scr[hd, 0:HEAD_DIM, :] / acc_scr[hd, HEAD_DIM:HEAD_DIM + 1, :] for hd in heads]
    o_ref[0] = (jnp.concatenate(outs, axis=0).T * gsa_ref[0]).astype(BF16)


def _attn_call(q, k_blk, vt_blk, bias_tiles, gsa):
    B, H, S, _ = q.shape
    nb = S // MOBA_BLOCK
    Tq = ROW_TILE
    G = ATTN_HEADS
    last = S // Tq - 1
    return pl.pallas_call(
        _attn_kernel,
        grid=(B, H // G, S // Tq),
        in_specs=[
            pl.BlockSpec((1, G, Tq, LANES), lambda b, hg, i: (b, hg, i, 0)),
            pl.BlockSpec((1, G, Tq, LANES), lambda b, hg, i: (b, hg, jnp.minimum(i + 1, last), 0)),
            pl.BlockSpec((1, G, nb, MOBA_BLOCK, LANES), lambda b, hg, i: (b, hg, 0, 0, 0)),
            pl.BlockSpec((1, G, nb, V_ROWS, MOBA_BLOCK), lambda b, hg, i: (b, hg, 0, 0, 0)),
            pl.BlockSpec((G, 2, MOBA_BLOCK, MOBA_BLOCK), lambda b, hg, i: (hg, 0, 0, 0)),
            pl.BlockSpec((1, Tq, G * HEAD_DIM), lambda b, hg, i: (b, i, hg)),
        ],
        out_specs=pl.BlockSpec((1, Tq, G * HEAD_DIM), lambda b, hg, i: (b, i, hg)),
        out_shape=jax.ShapeDtypeStruct((B, S, H * HEAD_DIM), BF16),
        scratch_shapes=[
            pltpu.VMEM((G, MOBA_BLOCK, Tq), F32),
            pltpu.VMEM((G, MOBA_BLOCK, Tq), F32),
            pltpu.VMEM((G, 1, Tq), F32),
            pltpu.VMEM((G, V_ROWS, Tq), F32),
        ],
        compiler_params=pltpu.CompilerParams(
            dimension_semantics=("arbitrary", "arbitrary", "arbitrary"),
            vmem_limit_bytes=VMEM_LIMIT),
        name="moba_attn",
    )(q, q, k_blk, vt_blk, bias_tiles, gsa)


def _out_kernel(h_ref, ya_ref, ybc_ref, p_ref, wout_ref, plew_ref, gatew_ref,
                fg_ref, o_ref, *, final):
    y = _dot(ya_ref[...], wout_ref[0:A_WIDTH, :]) + _dot(ybc_ref[...], wout_ref[A_WIDTH:, :])
    h1 = h_ref[...] + y
    emb = _dot(p_ref[...].astype(BF16), plew_ref[...])
    gate = _sigmoid(_dot(h1.astype(BF16), gatew_ref[...]))
    h2 = h1 + emb * gate
    if final:
        h2 = h2 * lax.rsqrt(jnp.mean(h2 * h2, axis=-1, keepdims=True) + EPS) * fg_ref[...]
    o_ref[...] = h2


def _out_call(layer, h, ya, ybc, p, w_out, ple_w, gate_w, final_g, final):
    N = h.shape[0]
    T = OUT_TILE
    row = lambda width: pl.BlockSpec((T, width), lambda i: (i, 0))
    full = lambda shape: pl.BlockSpec((None,) + shape, lambda i: (layer,) + (0,) * len(shape))
    return pl.pallas_call(
        functools.partial(_out_kernel, final=final),
        grid=(N // T,),
        in_specs=[
            row(D_MODEL), row(A_WIDTH), row(B_WIDTH + C_WIDTH),
            pl.BlockSpec((None, T, D_PLE), lambda i: (layer, i, 0)),
            full((D_MODEL, D_MODEL)), full((D_PLE, D_MODEL)), full((D_MODEL, D_MODEL)),
            pl.BlockSpec((1, D_MODEL), lambda i: (0, 0)),
        ],
        out_specs=row(D_MODEL),
        out_shape=jax.ShapeDtypeStruct((N, D_MODEL), F32),
        compiler_params=pltpu.CompilerParams(
            dimension_semantics=("arbitrary",), vmem_limit_bytes=VMEM_LIMIT),
        name="out_ple",
    )(h, ya, ybc, p, w_out, ple_w, gate_w, final_g)


def _block_diag(w):
    G, c, d = w.shape
    eye = jnp.eye(G, dtype=w.dtype)
    return (eye[:, None, :, None] * w[:, :, None, :]).reshape(G * c, G * d)


def kernel(x, p, norm_g, w_in, w_out, rel_bias, pool_w, pool_scale, sgu_w, sgu_b,
           ple_w, ple_gate_w, final_g):
    B, S, D = x.shape
    depth = w_in.shape[0]
    N = B * S
    bias_tiles = _bias_tiles(rel_bias)
    col = jnp.arange(D_IN)
    q_cols = (col >= Q0) & (col < Q0 + A_WIDTH)
    w_in_b = (w_in * jnp.where(q_cols, Q_SCALE, 1.0)).astype(BF16)
    pool_bd = jax.vmap(_block_diag)(pool_w).astype(BF16)
    sgu_wcat = jnp.transpose(jnp.tril(sgu_w), (0, 2, 1, 3)).reshape(
        depth, SGU_CHUNK, C_HEADS * SGU_CHUNK).astype(BF16)
    sgu_btile = jnp.repeat(jnp.transpose(sgu_b, (0, 2, 1)), HEAD_DIM, axis=2)
    w_out_b, ple_w_b, gate_w_b = (w.astype(BF16) for w in (w_out, ple_w, ple_gate_w))
    p_rows = p.reshape(depth, N, D_PLE)
    h = x
    for i in range(depth):
        q, k_blk, vt_blk, gsa, ybc = _proj_call(
            i, rel_bias, h, norm_g[:, None, :], w_in_b, pool_bd,
            pool_scale[:, None, :], sgu_wcat, sgu_btile)
        ya = _attn_call(q, k_blk, vt_blk, bias_tiles, gsa)
        h = _out_call(
            i, h.reshape(N, D), ya.reshape(N, A_WIDTH), ybc.reshape(N, B_WIDTH + C_WIDTH),
            p_rows, w_out_b, ple_w_b, gate_w_b, final_g[None, :],
            final=(i == depth - 1)).reshape(B, S, D)
    return h
```

```python
import functools
import math

import numpy as np
import jax
import jax.numpy as jnp
from jax import lax
from jax.experimental import pallas as pl
from jax.experimental.pallas import tpu as pltpu

D_MODEL = 1024
HEAD_DIM = 64
A_WIDTH = 512
A_HEADS = 8
MOBA_BLOCK = 256
MOBA_TOPK = 3
REL_BUCKETS = 32
REL_MAX_DIST = 128
B_WIDTH = 256
POOL_WINDOWS = (2, 4, 8, 16)
B_GROUP = 64
C_WIDTH = 256
C_HEADS = 4
SGU_CHUNK = 128
D_PLE = 256
D_IN = 3328
EPS = 1e-6
NEG = -1e30
LOG2E = math.log2(math.e)
Q_SCALE = HEAD_DIM ** -0.5 * LOG2E

MAX_WINDOW = max(POOL_WINDOWS)
ROW_BLOCKS = 2
ROW_TILE = ROW_BLOCKS * MOBA_BLOCK
OUT_TILE = 1024
LANES = 128
BF16_SUBLANES = 16
MAX_BLOCKS = (LANES - HEAD_DIM) // 2
MASK_HI = HEAD_DIM
MASK_LO = MASK_HI + MAX_BLOCKS
V_ROWS = HEAD_DIM + BF16_SUBLANES
ATTN_HEADS = 4
PAIR_UNROLLS = (4, 2, 1)
VMEM_LIMIT = 48 * 1024 * 1024

_OFF = np.cumsum((0,) + (A_WIDTH,) * 4 + (B_WIDTH,) * 2 + (C_WIDTH,) * 3)
Q0, K0, V0, GA0, XB0, GB0, UC0, VC0, GC0, _ = (int(o) for o in _OFF)

BF16 = jnp.bfloat16
F32 = jnp.float32


def _bucket_thresholds():
    max_exact = REL_BUCKETS // 2
    n = np.arange(0, 4 * MOBA_BLOCK)
    nf = np.maximum(n, 1).astype(np.float64)
    large = max_exact + (np.log(nf / max_exact) / math.log(REL_MAX_DIST / max_exact)
                         * (REL_BUCKETS - max_exact)).astype(np.int64)
    large = np.minimum(large, REL_BUCKETS - 1)
    bucket = np.where(n < max_exact, n, large)
    return [int(np.argmax(bucket >= b)) for b in range(1, REL_BUCKETS)]


_THRESHOLDS = _bucket_thresholds()


def _sigmoid(x):
    return 0.5 * jnp.tanh(0.5 * x) + 0.5


def _silu(x):
    return x * _sigmoid(x)


def _split_bf16(x):
    hi = x.astype(BF16)
    lo = (x - hi.astype(F32)).astype(BF16)
    return hi, lo


def _dot(a, b):
    return jnp.dot(a, b, preferred_element_type=F32)


def _dot_tn(a, b):
    return lax.dot_general(a, b, (((0,), (1,)), ((), ())), preferred_element_type=F32)


def _dot_nt(a, b):
    return lax.dot_general(a, b, (((1,), (1,)), ((), ())), preferred_element_type=F32)


def _bias_kernel(rb_ref, out_ref):
    h = pl.program_id(0)
    key = lax.broadcasted_iota(jnp.int32, (MOBA_BLOCK, MOBA_BLOCK), 0)
    qry = lax.broadcasted_iota(jnp.int32, (MOBA_BLOCK, MOBA_BLOCK), 1)
    for kind, shift in ((0, 0), (1, MOBA_BLOCK)):
        dist = qry - key + shift
        val = jnp.full((MOBA_BLOCK, MOBA_BLOCK), rb_ref[0, h], F32)
        for b in range(1, REL_BUCKETS):
            val = jnp.where(dist >= _THRESHOLDS[b - 1], rb_ref[b, h], val)
        val = val * LOG2E
        if kind == 0:
            val = jnp.where(dist >= 0, val, NEG)
        out_ref[0, kind] = val


def _bias_tiles(rel_bias):
    return pl.pallas_call(
        _bias_kernel,
        grid=(A_HEADS,),
        in_specs=[pl.BlockSpec(memory_space=pltpu.SMEM)],
        out_specs=pl.BlockSpec((1, 2, MOBA_BLOCK, MOBA_BLOCK), lambda h: (h, 0, 0, 0)),
        out_shape=jax.ShapeDtypeStruct((A_HEADS, 2, MOBA_BLOCK, MOBA_BLOCK), F32),
        name="bias_tiles",
    )(rel_bias)


def _proj_kernel(rb_ref, h_ref, ng_ref, win_ref, poolw_ref, pscale_ref,
                 sguw_ref, sgub_ref,
                 q_ref, k_ref, vt_ref, gsa_ref, ybc_ref,
                 kmt_scr, ext_scr, *, n_blocks):
    T = ROW_TILE
    W = MOBA_BLOCK
    s = pl.program_id(1)
    first_block = s * ROW_BLOCKS

    @pl.when(s == 0)
    def _():
        kmt_scr[...] = jnp.zeros_like(kmt_scr)
        ext_scr[0:MAX_WINDOW, :] = jnp.zeros((MAX_WINDOW, B_WIDTH), F32)

    hbs, zks = [], []
    for blk in range(ROW_BLOCKS):
        h = h_ref[0, blk * W:(blk + 1) * W, :]
        hn = h * lax.rsqrt(jnp.mean(h * h, axis=-1, keepdims=True) + EPS) * ng_ref[...]
        hbs.append(hn.astype(BF16))
        zks.append(_dot(hbs[-1], win_ref[:, K0:K0 + A_WIDTH]))
    hb = jnp.concatenate(hbs, axis=0)

    def proj(c0, width):
        return _dot(hb, win_ref[:, c0:c0 + width])

    zqt = _dot_tn(win_ref[:, Q0:Q0 + A_WIDTH], hb)

    lane = lax.broadcasted_iota(jnp.int32, (1, LANES), 1)
    for blk, zk_blk in enumerate(zks):
        j = first_block + blk
        indicator = jnp.where((lane == MASK_HI + j) | (lane == MASK_LO + j), 1.0, 0.0)
        k_mean = jnp.mean(zk_blk, axis=0, keepdims=True)
        for pair in range(A_HEADS // 2):
            zk_pair = zk_blk[:, pair * LANES:(pair + 1) * LANES]
            zk_odd = pltpu.roll(zk_pair, HEAD_DIM, 1)
            k_ref[0, 2 * pair, blk] = jnp.where(lane < HEAD_DIM, zk_pair, indicator).astype(BF16)
            k_ref[0, 2 * pair + 1, blk] = jnp.where(lane < HEAD_DIM, zk_odd, indicator).astype(BF16)
            mean_pair = k_mean[:, pair * LANES:(pair + 1) * LANES]
            kmt_scr[2 * pair, pl.ds(j, 1), :] = mean_pair
            kmt_scr[2 * pair + 1, pl.ds(j, 1), :] = pltpu.roll(mean_pair, HEAD_DIM, 1)

    q_hi, q_lo = _split_bf16(zqt)
    gates = []
    for hd in range(A_HEADS):
        rows = slice(hd * HEAD_DIM, (hd + 1) * HEAD_DIM)
        km_hi, km_lo = _split_bf16(kmt_scr[hd][:, 0:HEAD_DIM])
        gates.append(_dot(km_hi, q_hi[rows]) + _dot(km_hi, q_lo[rows]) + _dot(km_lo, q_hi[rows]))

    vc = proj(VC0, C_WIDTH)
    zuc = proj(UC0, C_WIDTH)
    zgc = proj(GC0, C_WIDTH)
    xb = proj(XB0, B_WIDTH)
    zgb = proj(GB0, B_WIDTH)
    zga = proj(GA0, A_WIDTH)
    zvt = _dot_tn(win_ref[:, V0:V0 + A_WIDTH], hb)

    jrow = lax.broadcasted_iota(jnp.int32, (n_blocks, T), 0)
    own = first_block + lax.broadcasted_iota(jnp.int32, (1, T), 1) // W
    for hd in range(A_HEADS):
        g = jnp.where(jrow < own, gates[hd], NEG)
        sel = jnp.zeros((n_blocks, T), jnp.bool_)
        for _ in range(MOBA_TOPK):
            m = jnp.max(g, axis=0, keepdims=True)
            idx = jnp.min(jnp.where(g == m, jrow, n_blocks), axis=0, keepdims=True)
            pick = jrow == idx
            sel = sel | (pick & (m > NEG * 0.5))
            g = jnp.where(pick, -jnp.inf, g)
        far_bias = rb_ref[REL_BUCKETS - 1, hd] * LOG2E
        term = jnp.where(sel, jnp.where(jrow <= own - 2, far_bias, 0.0), NEG)
        term = jnp.where(jrow == own, 0.0, term)
        term_hi = term.astype(BF16).astype(F32)
        q_aug_t = jnp.concatenate(
            [zqt[hd * HEAD_DIM:(hd + 1) * HEAD_DIM], term_hi, term - term_hi], axis=0)
        q_ref[0, hd] = q_aug_t.T.astype(BF16)

    tail_row = lax.broadcasted_iota(jnp.int32, (V_ROWS - HEAD_DIM, W), 0)
    tail = jnp.where(tail_row == 0, 1.0, 0.0)
    for hd in range(A_HEADS):
        for blk in range(ROW_BLOCKS):
            vt_ref[0, hd, blk] = jnp.concatenate(
                [zvt[hd * HEAD_DIM:(hd + 1) * HEAD_DIM, blk * W:(blk + 1) * W], tail],
                axis=0).astype(BF16)

    gsa_ref[0] = _silu(zga)

    ext_scr[MAX_WINDOW:MAX_WINDOW + T, :] = xb
    lane_b = lax.broadcasted_iota(jnp.int32, (1, B_WIDTH), 1)
    win = jnp.zeros((1, B_WIDTH), F32)
    for gi, w in enumerate(POOL_WINDOWS):
        win = jnp.where(lane_b // B_GROUP == gi, float(w), win)
    halves = []
    for half, (w_small, w_big) in enumerate(zip(POOL_WINDOWS[0::2], POOL_WINDOWS[1::2])):
        cols = slice(half * LANES, (half + 1) * LANES)
        run = xb[:, cols]
        for lag in range(1, w_small):
            run = run + ext_scr[MAX_WINDOW - lag:MAX_WINDOW - lag + T, cols]
        small = run
        for lag in range(w_small, w_big):
            run = run + ext_scr[MAX_WINDOW - lag:MAX_WINDOW - lag + T, cols]
        halves.append(jnp.where(lane < B_GROUP, small, run))
    wsum = jnp.concatenate(halves, axis=1)
    pos = (s * T + lax.broadcasted_iota(jnp.int32, (T, 1), 0) + 1).astype(F32)
    pooled = wsum / jnp.minimum(pos, win) - xb
    ext_scr[0:MAX_WINDOW, :] = xb[T - MAX_WINDOW:T, :]
    mixed_b = _dot(pooled.astype(BF16), poolw_ref[...]) * pscale_ref[...]
    ybc_ref[0, :, 0:B_WIDTH] = (mixed_b * _silu(zgb)).astype(BF16)

    mu = jnp.mean(vc, axis=-1, keepdims=True)
    cen = vc - mu
    var = jnp.mean(cen * cen, axis=-1, keepdims=True)
    vn = cen * lax.rsqrt(var + EPS)
    rows = lax.broadcasted_iota(jnp.int32, (C_HEADS * SGU_CHUNK, C_WIDTH), 0)
    cols = lax.broadcasted_iota(jnp.int32, (C_HEADS * SGU_CHUNK, C_WIDTH), 1)
    head_sel = (rows // SGU_CHUNK) == (cols // HEAD_DIM)
    ug = zuc * _silu(zgc)
    for c in range(T // SGU_CHUNK):
        vn_c = vn[c * SGU_CHUNK:(c + 1) * SGU_CHUNK]
        stack = jnp.where(head_sel, jnp.concatenate([vn_c] * C_HEADS, axis=0), 0.0)
        mixed_c = _dot(sguw_ref[...], stack.astype(BF16)) + sgub_ref[...]
        ybc_ref[0, c * SGU_CHUNK:(c + 1) * SGU_CHUNK, B_WIDTH:B_WIDTH + C_WIDTH] = (
            ug[c * SGU_CHUNK:(c + 1) * SGU_CHUNK] * mixed_c).astype(BF16)


def _proj_call(layer, rel_bias, h, norm_g, w_in, pool_bd, pool_scale, sgu_wcat, sgu_btile):
    B, S, _ = h.shape
    T = ROW_TILE
    nb = S // MOBA_BLOCK
    assert nb <= MAX_BLOCKS and S % T == 0
    full = lambda shape: pl.BlockSpec((None,) + shape, lambda b, s: (layer,) + (0,) * len(shape))
    return pl.pallas_call(
        functools.partial(_proj_kernel, n_blocks=nb),
        grid=(B, S // T),
        in_specs=[
            pl.BlockSpec(memory_space=pltpu.SMEM),
            pl.BlockSpec((1, T, D_MODEL), lambda b, s: (b, s, 0)),
            full((1, D_MODEL)),
            full((D_MODEL, D_IN)),
            full((B_WIDTH, B_WIDTH)),
            full((1, B_WIDTH)),
            full((SGU_CHUNK, C_HEADS * SGU_CHUNK)),
            full((SGU_CHUNK, C_WIDTH)),
        ],
        out_specs=[
            pl.BlockSpec((1, A_HEADS, T, LANES), lambda b, s: (b, 0, s, 0)),
            pl.BlockSpec((1, A_HEADS, ROW_BLOCKS, MOBA_BLOCK, LANES), lambda b, s: (b, 0, s, 0, 0)),
            pl.BlockSpec((1, A_HEADS, ROW_BLOCKS, V_ROWS, MOBA_BLOCK), lambda b, s: (b, 0, s, 0, 0)),
            pl.BlockSpec((1, T, A_WIDTH), lambda b, s: (b, s, 0)),
            pl.BlockSpec((1, T, B_WIDTH + C_WIDTH), lambda b, s: (b, s, 0)),
        ],
        out_shape=[
            jax.ShapeDtypeStruct((B, A_HEADS, S, LANES), BF16),
            jax.ShapeDtypeStruct((B, A_HEADS, nb, MOBA_BLOCK, LANES), BF16),
            jax.ShapeDtypeStruct((B, A_HEADS, nb, V_ROWS, MOBA_BLOCK), BF16),
            jax.ShapeDtypeStruct((B, S, A_WIDTH), F32),
            jax.ShapeDtypeStruct((B, S, B_WIDTH + C_WIDTH), BF16),
        ],
        scratch_shapes=[
            pltpu.VMEM((A_HEADS, nb, LANES), F32),
            pltpu.VMEM((MAX_WINDOW + T, B_WIDTH), F32),
        ],
        compiler_params=pltpu.CompilerParams(
            dimension_semantics=("arbitrary", "arbitrary"), vmem_limit_bytes=VMEM_LIMIT),
        name="proj_mix",
    )(rel_bias, h, norm_g, w_in, pool_bd, pool_scale, sgu_wcat, sgu_btile)


def _attn_kernel(q_ref, qnext_ref, k_ref, vt_ref, bias_ref, gsa_ref, o_ref,
                 sa_scr, sb_scr, m_scr, acc_scr):
    i = pl.program_id(2)
    W = MOBA_BLOCK
    n_pairs = jnp.maximum(i - 1, 0)
    parked = 2 * n_pairs
    before = jnp.maximum(2 * i - 1, 0)
    pen_parked = jnp.where(i >= 1, 0.0, NEG)
    pen_before = jnp.where(i >= 1, 0.0, NEG)
    heads = range(ATTN_HEADS)
    OWN, PREV = 0, 1

    BOTH = (0, 1)

    def park(hd, scr, j, bias_kinds=(None, None), halves=BOTH, queries=q_ref):
        rows = slice(halves[0] * W, (halves[-1] + 1) * W)
        st = _dot_nt(k_ref[0, hd, j], queries[0, hd, rows, :])
        for n, half in enumerate(halves):
            kind = bias_kinds[half]
            part = st[:, n * W:(n + 1) * W]
            cols = slice(half * W, (half + 1) * W)
            scr[hd, :, cols] = part if kind is None else part + bias_ref[hd, kind]

    def consume(hd, scr, j, pen=None, halves=BOTH):
        for half in halves:
            cols = slice(half * W, (half + 1) * W)
            ps, ms = [], []
            for quarter in range(2):
                qcols = slice(half * W + quarter * LANES, half * W + (quarter + 1) * LANES)
                st = scr[hd, :, qcols]
                cm = jnp.max(st, axis=0, keepdims=True)
                m_new = jnp.maximum(m_scr[hd, :, qcols], cm if pen is None else cm + pen)
                shift = m_new if pen is None else m_new - pen
                ps.append(jnp.exp2(st - shift).astype(BF16))
                ms.append(m_new)
            pv = _dot(vt_ref[0, hd, j], jnp.concatenate(ps, axis=1))
            m = m_scr[hd, :, cols]
            m_new = jnp.concatenate(ms, axis=1)
            m_scr[hd, :, cols] = m_new
            acc_scr[hd, :, cols] = jnp.exp2(m - m_new) * acc_scr[hd, :, cols] + pv

    def stage(produce=None, take=None):
        for hd in heads:
            if produce is not None:
                park(hd, *produce)
            if take is not None:
                consume(hd, *take)

    m_scr[...] = jnp.full(m_scr.shape, -jnp.inf, F32)
    acc_scr[...] = jnp.zeros(acc_scr.shape, F32)

    @pl.when(i == 0)
    def _():
        stage(produce=(sa_scr, 0))

    def pair_step(t, carry):
        stage(produce=(sb_scr, 2 * t + 1), take=(sa_scr, 2 * t))
        stage(produce=(sa_scr, 2 * t + 2), take=(sb_scr, 2 * t + 1))
        return carry

    done = 0
    for unroll in PAIR_UNROLLS:
        trips = (n_pairs - done) // unroll

        def body(u, carry, first=done, unroll=unroll):
            for r in range(unroll):
                pair_step(first + u * unroll + r, carry)
            return carry

        lax.fori_loop(0, trips, body, 0)
        done = done + trips * unroll

    second = (1,)
    stage(produce=(sb_scr, before, (PREV, None)), take=(sa_scr, parked, pen_parked))
    stage(produce=(sa_scr, 2 * i, (OWN, PREV)), take=(sb_scr, before, pen_before))
    stage(produce=(sb_scr, 2 * i + 1, (None, OWN), second), take=(sa_scr, 2 * i))
    stage(produce=(sa_scr, 0, (None, None), BOTH, qnext_ref), take=(sb_scr, 2 * i + 1, None, second))
    outs = [acc_scr[hd, 0:HEAD_DIM, :] / acc_scr[hd, HEAD_DIM:HEAD_DIM + 1, :] for hd in heads]
    o_ref[0] = (jnp.concatenate(outs, axis=0).T * gsa_ref[0]).astype(BF16)


def _attn_call(q, k_blk, vt_blk, bias_tiles, gsa):
    B, H, S, _ = q.shape
    nb = S // MOBA_BLOCK
    Tq = ROW_TILE
    G = ATTN_HEADS
    last = S // Tq - 1
    return pl.pallas_call(
        _attn_kernel,
        grid=(B, H // G, S // Tq),
        in_specs=[
            pl.BlockSpec((1, G, Tq, LANES), lambda b, hg, i: (b, hg, i, 0)),
            pl.BlockSpec((1, G, Tq, LANES), lambda b, hg, i: (b, hg, jnp.minimum(i + 1, last), 0)),
            pl.BlockSpec((1, G, nb, MOBA_BLOCK, LANES), lambda b, hg, i: (b, hg, 0, 0, 0)),
            pl.BlockSpec((1, G, nb, V_ROWS, MOBA_BLOCK), lambda b, hg, i: (b, hg, 0, 0, 0)),
            pl.BlockSpec((G, 2, MOBA_BLOCK, MOBA_BLOCK), lambda b, hg, i: (hg, 0, 0, 0)),
            pl.BlockSpec((1, Tq, G * HEAD_DIM), lambda b, hg, i: (b, i, hg)),
        ],
        out_specs=pl.BlockSpec((1, Tq, G * HEAD_DIM), lambda b, hg, i: (b, i, hg)),
        out_shape=jax.ShapeDtypeStruct((B, S, H * HEAD_DIM), BF16),
        scratch_shapes=[
            pltpu.VMEM((G, MOBA_BLOCK, Tq), F32),
            pltpu.VMEM((G, MOBA_BLOCK, Tq), F32),
            pltpu.VMEM((G, 1, Tq), F32),
            pltpu.VMEM((G, V_ROWS, Tq), F32),
        ],
        compiler_params=pltpu.CompilerParams(
            dimension_semantics=("arbitrary", "arbitrary", "arbitrary"),
            vmem_limit_bytes=VMEM_LIMIT),
        name="moba_attn",
    )(q, q, k_blk, vt_blk, bias_tiles, gsa)


def _out_kernel(h_ref, ya_ref, ybc_ref, p_ref, wout_ref, plew_ref, gatew_ref,
                fg_ref, o_ref, *, final):
    y = _dot(ya_ref[...], wout_ref[0:A_WIDTH, :]) + _dot(ybc_ref[...], wout_ref[A_WIDTH:, :])
    h1 = h_ref[...] + y
    emb = _dot(p_ref[...].astype(BF16), plew_ref[...])
    gate = _sigmoid(_dot(h1.astype(BF16), gatew_ref[...]))
    h2 = h1 + emb * gate
    if final:
        h2 = h2 * lax.rsqrt(jnp.mean(h2 * h2, axis=-1, keepdims=True) + EPS) * fg_ref[...]
    o_ref[...] = h2


def _out_call(layer, h, ya, ybc, p, w_out, ple_w, gate_w, final_g, final):
    N = h.shape[0]
    T = OUT_TILE
    row = lambda width: pl.BlockSpec((T, width), lambda i: (i, 0))
    full = lambda shape: pl.BlockSpec((None,) + shape, lambda i: (layer,) + (0,) * len(shape))
    return pl.pallas_call(
        functools.partial(_out_kernel, final=final),
        grid=(N // T,),
        in_specs=[
            row(D_MODEL), row(A_WIDTH), row(B_WIDTH + C_WIDTH),
            pl.BlockSpec((None, T, D_PLE), lambda i: (layer, i, 0)),
            full((D_MODEL, D_MODEL)), full((D_PLE, D_MODEL)), full((D_MODEL, D_MODEL)),
            pl.BlockSpec((1, D_MODEL), lambda i: (0, 0)),
        ],
        out_specs=row(D_MODEL),
        out_shape=jax.ShapeDtypeStruct((N, D_MODEL), F32),
        compiler_params=pltpu.CompilerParams(
            dimension_semantics=("arbitrary",), vmem_limit_bytes=VMEM_LIMIT),
        name="out_ple",
    )(h, ya, ybc, p, w_out, ple_w, gate_w, final_g)


def _block_diag(w):
    G, c, d = w.shape
    eye = jnp.eye(G, dtype=w.dtype)
    return (eye[:, None, :, None] * w[:, :, None, :]).reshape(G * c, G * d)


def kernel(x, p, norm_g, w_in, w_out, rel_bias, pool_w, pool_scale, sgu_w, sgu_b,
           ple_w, ple_gate_w, final_g):
    B, S, D = x.shape
    depth = w_in.shape[0]
    N = B * S
    bias_tiles = _bias_tiles(rel_bias)
    col = jnp.arange(D_IN)
    q_cols = (col >= Q0) & (col < Q0 + A_WIDTH)
    w_in_b = (w_in * jnp.where(q_cols, Q_SCALE, 1.0)).astype(BF16)
    pool_bd = jax.vmap(_block_diag)(pool_w).astype(BF16)
    sgu_wcat = jnp.transpose(jnp.tril(sgu_w), (0, 2, 1, 3)).reshape(
        depth, SGU_CHUNK, C_HEADS * SGU_CHUNK).astype(BF16)
    sgu_btile = jnp.repeat(jnp.transpose(sgu_b, (0, 2, 1)), HEAD_DIM, axis=2)
    w_out_b, ple_w_b, gate_w_b = (w.astype(BF16) for w in (w_out, ple_w, ple_gate_w))
    p_rows = p.reshape(depth, N, D_PLE)
    h = x
    for i in range(depth):
        q, k_blk, vt_blk, gsa, ybc = _proj_call(
            i, rel_bias, h, norm_g[:, None, :], w_in_b, pool_bd,
            pool_scale[:, None, :], sgu_wcat, sgu_btile)
        ya = _attn_call(q, k_blk, vt_blk, bias_tiles, gsa)
        h = _out_call(
            i, h.reshape(N, D), ya.reshape(N, A_WIDTH), ybc.reshape(N, B_WIDTH + C_WIDTH),
            p_rows, w_out_b, ple_w_b, gate_w_b, final_g[None, :],
            final=(i == depth - 1)).reshape(B, S, D)
    return h
```

```python
import functools
import math

import numpy as np
import jax
import jax.numpy as jnp
from jax import lax
from jax.experimental import pallas as pl
from jax.experimental.pallas import tpu as pltpu

D_MODEL = 1024
HEAD_DIM = 64
A_WIDTH = 512
A_HEADS = 8
MOBA_BLOCK = 256
MOBA_TOPK = 3
REL_BUCKETS = 32
REL_MAX_DIST = 128
B_WIDTH = 256
POOL_WINDOWS = (2, 4, 8, 16)
B_GROUP = 64
C_WIDTH = 256
C_HEADS = 4
SGU_CHUNK = 128
D_PLE = 256
D_IN = 3328
EPS = 1e-6
NEG = -1e30
LOG2E = math.log2(math.e)
Q_SCALE = HEAD_DIM ** -0.5 * LOG2E

MAX_WINDOW = max(POOL_WINDOWS)
ROW_BLOCKS = 2
ROW_TILE = ROW_BLOCKS * MOBA_BLOCK
OUT_TILE = 1024
LANES = 128
BF16_SUBLANES = 16
MAX_BLOCKS = (LANES - HEAD_DIM) // 2
MASK_HI = HEAD_DIM
MASK_LO = MASK_HI + MAX_BLOCKS
V_ROWS = HEAD_DIM + BF16_SUBLANES
ATTN_HEADS = 4
PAIR_UNROLLS = (8, 4, 2, 1)
VMEM_LIMIT = 48 * 1024 * 1024

_OFF = np.cumsum((0,) + (A_WIDTH,) * 4 + (B_WIDTH,) * 2 + (C_WIDTH,) * 3)
Q0, K0, V0, GA0, XB0, GB0, UC0, VC0, GC0, _ = (int(o) for o in _OFF)

BF16 = jnp.bfloat16
F32 = jnp.float32


def _bucket_thresholds():
    max_exact = REL_BUCKETS // 2
    n = np.arange(0, 4 * MOBA_BLOCK)
    nf = np.maximum(n, 1).astype(np.float64)
    large = max_exact + (np.log(nf / max_exact) / math.log(REL_MAX_DIST / max_exact)
                         * (REL_BUCKETS - max_exact)).astype(np.int64)
    large = np.minimum(large, REL_BUCKETS - 1)
    bucket = np.where(n < max_exact, n, large)
    return [int(np.argmax(bucket >= b)) for b in range(1, REL_BUCKETS)]


_THRESHOLDS = _bucket_thresholds()


def _sigmoid(x):
    return 0.5 * jnp.tanh(0.5 * x) + 0.5


def _silu(x):
    return x * _sigmoid(x)


def _split_bf16(x):
    hi = x.astype(BF16)
    lo = (x - hi.astype(F32)).astype(BF16)
    return hi, lo


def _dot(a, b):
    return jnp.dot(a, b, preferred_element_type=F32)


def _dot_tn(a, b):
    return lax.dot_general(a, b, (((0,), (1,)), ((), ())), preferred_element_type=F32)


def _dot_nt(a, b):
    return lax.dot_general(a, b, (((1,), (1,)), ((), ())), preferred_element_type=F32)


def _bias_kernel(rb_ref, out_ref):
    h = pl.program_id(0)
    key = lax.broadcasted_iota(jnp.int32, (MOBA_BLOCK, MOBA_BLOCK), 0)
    qry = lax.broadcasted_iota(jnp.int32, (MOBA_BLOCK, MOBA_BLOCK), 1)
    for kind, shift in ((0, 0), (1, MOBA_BLOCK)):
        dist = qry - key + shift
        val = jnp.full((MOBA_BLOCK, MOBA_BLOCK), rb_ref[0, h], F32)
        for b in range(1, REL_BUCKETS):
            val = jnp.where(dist >= _THRESHOLDS[b - 1], rb_ref[b, h], val)
        val = val * LOG2E
        if kind == 0:
            val = jnp.where(dist >= 0, val, NEG)
        out_ref[0, kind] = val


def _bias_tiles(rel_bias):
    return pl.pallas_call(
        _bias_kernel,
        grid=(A_HEADS,),
        in_specs=[pl.BlockSpec(memory_space=pltpu.SMEM)],
        out_specs=pl.BlockSpec((1, 2, MOBA_BLOCK, MOBA_BLOCK), lambda h: (h, 0, 0, 0)),
        out_shape=jax.ShapeDtypeStruct((A_HEADS, 2, MOBA_BLOCK, MOBA_BLOCK), F32),
        name="bias_tiles",
    )(rel_bias)


def _proj_kernel(rb_ref, h_ref, ng_ref, win_ref, poolw_ref, pscale_ref,
                 sguw_ref, sgub_ref,
                 q_ref, k_ref, vt_ref, gsa_ref, ybc_ref,
                 kmt_scr, ext_scr, *, n_blocks):
    T = ROW_TILE
    W = MOBA_BLOCK
    s = pl.program_id(1)
    first_block = s * ROW_BLOCKS

    @pl.when(s == 0)
    def _():
        kmt_scr[...] = jnp.zeros_like(kmt_scr)
        ext_scr[0:MAX_WINDOW, :] = jnp.zeros((MAX_WINDOW, B_WIDTH), F32)

    hbs, zks = [], []
    for blk in range(ROW_BLOCKS):
        h = h_ref[0, blk * W:(blk + 1) * W, :]
        hn = h * lax.rsqrt(jnp.mean(h * h, axis=-1, keepdims=True) + EPS) * ng_ref[...]
        hbs.append(hn.astype(BF16))
        zks.append(_dot(hbs[-1], win_ref[:, K0:K0 + A_WIDTH]))
    hb = jnp.concatenate(hbs, axis=0)

    def proj(c0, width):
        return _dot(hb, win_ref[:, c0:c0 + width])

    zqt = _dot_tn(win_ref[:, Q0:Q0 + A_WIDTH], hb)

    lane = lax.broadcasted_iota(jnp.int32, (1, LANES), 1)
    for blk, zk_blk in enumerate(zks):
        j = first_block + blk
        indicator = jnp.where((lane == MASK_HI + j) | (lane == MASK_LO + j), 1.0, 0.0)
        k_mean = jnp.mean(zk_blk, axis=0, keepdims=True)
        for pair in range(A_HEADS // 2):
            zk_pair = zk_blk[:, pair * LANES:(pair + 1) * LANES]
            zk_odd = pltpu.roll(zk_pair, HEAD_DIM, 1)
            k_ref[0, 2 * pair, blk] = jnp.where(lane < HEAD_DIM, zk_pair, indicator).astype(BF16)
            k_ref[0, 2 * pair + 1, blk] = jnp.where(lane < HEAD_DIM, zk_odd, indicator).astype(BF16)
            mean_pair = k_mean[:, pair * LANES:(pair + 1) * LANES]
            kmt_scr[2 * pair, pl.ds(j, 1), :] = mean_pair
            kmt_scr[2 * pair + 1, pl.ds(j, 1), :] = pltpu.roll(mean_pair, HEAD_DIM, 1)

    q_hi, q_lo = _split_bf16(zqt)
    gates = []
    for hd in range(A_HEADS):
        rows = slice(hd * HEAD_DIM, (hd + 1) * HEAD_DIM)
        km_hi, km_lo = _split_bf16(kmt_scr[hd][:, 0:HEAD_DIM])
        gates.append(_dot(km_hi, q_hi[rows]) + _dot(km_hi, q_lo[rows]) + _dot(km_lo, q_hi[rows]))

    vc = proj(VC0, C_WIDTH)
    zuc = proj(UC0, C_WIDTH)
    zgc = proj(GC0, C_WIDTH)
    xb = proj(XB0, B_WIDTH)
    zgb = proj(GB0, B_WIDTH)
    zga = proj(GA0, A_WIDTH)
    zvt = _dot_tn(win_ref[:, V0:V0 + A_WIDTH], hb)

    jrow = lax.broadcasted_iota(jnp.int32, (n_blocks, T), 0)
    own = first_block + lax.broadcasted_iota(jnp.int32, (1, T), 1) // W
    for hd in range(A_HEADS):
        g = jnp.where(jrow < own, gates[hd], NEG)
        sel = jnp.zeros((n_blocks, T), jnp.bool_)
        for _ in range(MOBA_TOPK):
            m = jnp.max(g, axis=0, keepdims=True)
            idx = jnp.min(jnp.where(g == m, jrow, n_blocks), axis=0, keepdims=True)
            pick = jrow == idx
            sel = sel | (pick & (m > NEG * 0.5))
            g = jnp.where(pick, -jnp.inf, g)
        far_bias = rb_ref[REL_BUCKETS - 1, hd] * LOG2E
        term = jnp.where(sel, jnp.where(jrow <= own - 2, far_bias, 0.0), NEG)
        term = jnp.where(jrow == own, 0.0, term)
        term_hi = term.astype(BF16).astype(F32)
        q_aug_t = jnp.concatenate(
            [zqt[hd * HEAD_DIM:(hd + 1) * HEAD_DIM], term_hi, term - term_hi], axis=0)
        q_ref[0, hd] = q_aug_t.T.astype(BF16)

    tail_row = lax.broadcasted_iota(jnp.int32, (V_ROWS - HEAD_DIM, W), 0)
    tail = jnp.where(tail_row == 0, 1.0, 0.0)
    for hd in range(A_HEADS):
        for blk in range(ROW_BLOCKS):
            vt_ref[0, hd, blk] = jnp.concatenate(
                [zvt[hd * HEAD_DIM:(hd + 1) * HEAD_DIM, blk * W:(blk + 1) * W], tail],
                axis=0).astype(BF16)

    gsa_ref[0] = _silu(zga)

    ext_scr[MAX_WINDOW:MAX_WINDOW + T, :] = xb
    lane_b = lax.broadcasted_iota(jnp.int32, (1, B_WIDTH), 1)
    win = jnp.zeros((1, B_WIDTH), F32)
    for gi, w in enumerate(POOL_WINDOWS):
        win = jnp.where(lane_b // B_GROUP == gi, float(w), win)
    halves = []
    for half, (w_small, w_big) in enumerate(zip(POOL_WINDOWS[0::2], POOL_WINDOWS[1::2])):
        cols = slice(half * LANES, (half + 1) * LANES)
        run = xb[:, cols]
        for lag in range(1, w_small):
            run = run + ext_scr[MAX_WINDOW - lag:MAX_WINDOW - lag + T, cols]
        small = run
        for lag in range(w_small, w_big):
            run = run + ext_scr[MAX_WINDOW - lag:MAX_WINDOW - lag + T, cols]
        halves.append(jnp.where(lane < B_GROUP, small, run))
    wsum = jnp.concatenate(halves, axis=1)
    pos = (s * T + lax.broadcasted_iota(jnp.int32, (T, 1), 0) + 1).astype(F32)
    pooled = wsum / jnp.minimum(pos, win) - xb
    ext_scr[0:MAX_WINDOW, :] = xb[T - MAX_WINDOW:T, :]
    mixed_b = _dot(pooled.astype(BF16), poolw_ref[...]) * pscale_ref[...]
    ybc_ref[0, :, 0:B_WIDTH] = (mixed_b * _silu(zgb)).astype(BF16)

    mu = jnp.mean(vc, axis=-1, keepdims=True)
    cen = vc - mu
    var = jnp.mean(cen * cen, axis=-1, keepdims=True)
    vn = cen * lax.rsqrt(var + EPS)
    rows = lax.broadcasted_iota(jnp.int32, (C_HEADS * SGU_CHUNK, C_WIDTH), 0)
    cols = lax.broadcasted_iota(jnp.int32, (C_HEADS * SGU_CHUNK, C_WIDTH), 1)
    head_sel = (rows // SGU_CHUNK) == (cols // HEAD_DIM)
    ug = zuc * _silu(zgc)
    for c in range(T // SGU_CHUNK):
        vn_c = vn[c * SGU_CHUNK:(c + 1) * SGU_CHUNK]
        stack = jnp.where(head_sel, jnp.concatenate([vn_c] * C_HEADS, axis=0), 0.0)
        mixed_c = _dot(sguw_ref[...], stack.astype(BF16)) + sgub_ref[...]
        ybc_ref[0, c * SGU_CHUNK:(c + 1) * SGU_CHUNK, B_WIDTH:B_WIDTH + C_WIDTH] = (
            ug[c * SGU_CHUNK:(c + 1) * SGU_CHUNK] * mixed_c).astype(BF16)


def _proj_call(layer, rel_bias, h, norm_g, w_in, pool_bd, pool_scale, sgu_wcat, sgu_btile):
    B, S, _ = h.shape
    T = ROW_TILE
    nb = S // MOBA_BLOCK
    assert nb <= MAX_BLOCKS and S % T == 0
    full = lambda shape: pl.BlockSpec((None,) + shape, lambda b, s: (layer,) + (0,) * len(shape))
    return pl.pallas_call(
        functools.partial(_proj_kernel, n_blocks=nb),
        grid=(B, S // T),
        in_specs=[
            pl.BlockSpec(memory_space=pltpu.SMEM),
            pl.BlockSpec((1, T, D_MODEL), lambda b, s: (b, s, 0)),
            full((1, D_MODEL)),
            full((D_MODEL, D_IN)),
            full((B_WIDTH, B_WIDTH)),
            full((1, B_WIDTH)),
            full((SGU_CHUNK, C_HEADS * SGU_CHUNK)),
            full((SGU_CHUNK, C_WIDTH)),
        ],
        out_specs=[
            pl.BlockSpec((1, A_HEADS, T, LANES), lambda b, s: (b, 0, s, 0)),
            pl.BlockSpec((1, A_HEADS, ROW_BLOCKS, MOBA_BLOCK, LANES), lambda b, s: (b, 0, s, 0, 0)),
            pl.BlockSpec((1, A_HEADS, ROW_BLOCKS, V_ROWS, MOBA_BLOCK), lambda b, s: (b, 0, s, 0, 0)),
            pl.BlockSpec((1, T, A_WIDTH), lambda b, s: (b, s, 0)),
            pl.BlockSpec((1, T, B_WIDTH + C_WIDTH), lambda b, s: (b, s, 0)),
        ],
        out_shape=[
            jax.ShapeDtypeStruct((B, A_HEADS, S, LANES), BF16),
            jax.ShapeDtypeStruct((B, A_HEADS, nb, MOBA_BLOCK, LANES), BF16),
            jax.ShapeDtypeStruct((B, A_HEADS, nb, V_ROWS, MOBA_BLOCK), BF16),
            jax.ShapeDtypeStruct((B, S, A_WIDTH), F32),
            jax.ShapeDtypeStruct((B, S, B_WIDTH + C_WIDTH), BF16),
        ],
        scratch_shapes=[
            pltpu.VMEM((A_HEADS, nb, LANES), F32),
            pltpu.VMEM((MAX_WINDOW + T, B_WIDTH), F32),
        ],
        compiler_params=pltpu.CompilerParams(
            dimension_semantics=("arbitrary", "arbitrary"), vmem_limit_bytes=VMEM_LIMIT),
        name="proj_mix",
    )(rel_bias, h, norm_g, w_in, pool_bd, pool_scale, sgu_wcat, sgu_btile)


def _attn_kernel(q_ref, qnext_ref, k_ref, vt_ref, bias_ref, gsa_ref, o_ref,
                 sa_scr, sb_scr, m_scr, acc_scr):
    i = pl.program_id(2)
    W = MOBA_BLOCK
    n_pairs = jnp.maximum(i - 1, 0)
    parked = 2 * n_pairs
    before = jnp.maximum(2 * i - 1, 0)
    pen_parked = jnp.where(i >= 1, 0.0, NEG)
    pen_before = jnp.where(i >= 1, 0.0, NEG)
    heads = range(ATTN_HEADS)
    OWN, PREV = 0, 1

    BOTH = (0, 1)

    def park(hd, scr, j, bias_kinds=(None, None), halves=BOTH, queries=q_ref):
        rows = slice(halves[0] * W, (halves[-1] + 1) * W)
        st = _dot_nt(k_ref[0, hd, j], queries[0, hd, rows, :])
        for n, half in enumerate(halves):
            kind = bias_kinds[half]
            part = st[:, n * W:(n + 1) * W]
            cols = slice(half * W, (half + 1) * W)
            scr[hd, :, cols] = part if kind is None else part + bias_ref[hd, kind]

    def consume(hd, scr, j, pen=None, halves=BOTH):
        for half in halves:
            cols = slice(half * W, (half + 1) * W)
            ps, ms = [], []
            for quarter in range(2):
                qcols = slice(half * W + quarter * LANES, half * W + (quarter + 1) * LANES)
                st = scr[hd, :, qcols]
                cm = jnp.max(st, axis=0, keepdims=True)
                m_new = jnp.maximum(m_scr[hd, :, qcols], cm if pen is None else cm + pen)
                shift = m_new if pen is None else m_new - pen
                ps.append(jnp.exp2(st - shift).astype(BF16))
                ms.append(m_new)
            pv = _dot(vt_ref[0, hd, j], jnp.concatenate(ps, axis=1))
            m = m_scr[hd, :, cols]
            m_new = jnp.concatenate(ms, axis=1)
            m_scr[hd, :, cols] = m_new
            acc_scr[hd, :, cols] = jnp.exp2(m - m_new) * acc_scr[hd, :, cols] + pv

    def stage(produce=None, take=None):
        for hd in heads:
            if produce is not None:
                park(hd, *produce)
            if take is not None:
                consume(hd, *take)

    m_scr[...] = jnp.full(m_scr.shape, -jnp.inf, F32)
    acc_scr[...] = jnp.zeros(acc_scr.shape, F32)

    @pl.when(i == 0)
    def _():
        stage(produce=(sa_scr, 0))

    def pair_step(t, carry):
        stage(produce=(sb_scr, 2 * t + 1), take=(sa_scr, 2 * t))
        stage(produce=(sa_scr, 2 * t + 2), take=(sb_scr, 2 * t + 1))
        return carry

    done = 0
    for unroll in PAIR_UNROLLS:
        trips = (n_pairs - done) // unroll

        def body(u, carry, first=done, unroll=unroll):
            for r in range(unroll):
                pair_step(first + u * unroll + r, carry)
            return carry

        lax.fori_loop(0, trips, body, 0)
        done = done + trips * unroll

    second = (1,)
    stage(produce=(sb_scr, before, (PREV, None)), take=(sa_scr, parked, pen_parked))
    stage(produce=(sa_scr, 2 * i, (OWN, PREV)), take=(sb_scr, before, pen_before))
    stage(produce=(sb_scr, 2 * i + 1, (None, OWN), second), take=(sa_scr, 2 * i))
    stage(produce=(sa_scr, 0, (None, None), BOTH, qnext_ref), take=(sb_scr, 2 * i + 1, None, second))
    outs = [acc_scr[hd, 0:HEAD_DIM, :] / acc_scr[hd, HEAD_DIM:HEAD_DIM + 1, :] for hd in heads]
    o_ref[0] = (jnp.concatenate(outs, axis=0).T * gsa_ref[0]).astype(BF16)


def _attn_call(q, k_blk, vt_blk, bias_tiles, gsa):
    B, H, S, _ = q.shape
    nb = S // MOBA_BLOCK
    Tq = ROW_TILE
    G = ATTN_HEADS
    last = S // Tq - 1
    return pl.pallas_call(
        _attn_kernel,
        grid=(B, H // G, S // Tq),
        in_specs=[
            pl.BlockSpec((1, G, Tq, LANES), lambda b, hg, i: (b, hg, i, 0)),
            pl.BlockSpec((1, G, Tq, LANES), lambda b, hg, i: (b, hg, jnp.minimum(i + 1, last), 0)),
            pl.BlockSpec((1, G, nb, MOBA_BLOCK, LANES), lambda b, hg, i: (b, hg, 0, 0, 0)),
            pl.BlockSpec((1, G, nb, V_ROWS, MOBA_BLOCK), lambda b, hg, i: (b, hg, 0, 0, 0)),
            pl.BlockSpec((G, 2, MOBA_BLOCK, MOBA_BLOCK), lambda b, hg, i: (hg, 0, 0, 0)),
            pl.BlockSpec((1, Tq, G * HEAD_DIM), lambda b, hg, i: (b, i, hg)),
        ],
        out_specs=pl.BlockSpec((1, Tq, G * HEAD_DIM), lambda b, hg, i: (b, i, hg)),
        out_shape=jax.ShapeDtypeStruct((B, S, H * HEAD_DIM), BF16),
        scratch_shapes=[
            pltpu.VMEM((G, MOBA_BLOCK, Tq), F32),
            pltpu.VMEM((G, MOBA_BLOCK, Tq), F32),
            pltpu.VMEM((G, 1, Tq), F32),
            pltpu.VMEM((G, V_ROWS, Tq), F32),
        ],
        compiler_params=pltpu.CompilerParams(
            dimension_semantics=("arbitrary", "arbitrary", "arbitrary"),
            vmem_limit_bytes=VMEM_LIMIT),
        name="moba_attn",
    )(q, q, k_blk, vt_blk, bias_tiles, gsa)


def _out_kernel(h_ref, ya_ref, ybc_ref, p_ref, wout_ref, plew_ref, gatew_ref,
                fg_ref, o_ref, *, final):
    y = _dot(ya_ref[...], wout_ref[0:A_WIDTH, :]) + _dot(ybc_ref[...], wout_ref[A_WIDTH:, :])
    h1 = h_ref[...] + y
    emb = _dot(p_ref[...].astype(BF16), plew_ref[...])
    gate = _sigmoid(_dot(h1.astype(BF16), gatew_ref[...]))
    h2 = h1 + emb * gate
    if final:
        h2 = h2 * lax.rsqrt(jnp.mean(h2 * h2, axis=-1, keepdims=True) + EPS) * fg_ref[...]
    o_ref[...] = h2


def _out_call(layer, h, ya, ybc, p, w_out, ple_w, gate_w, final_g, final):
    N = h.shape[0]
    T = OUT_TILE
    row = lambda width: pl.BlockSpec((T, width), lambda i: (i, 0))
    full = lambda shape: pl.BlockSpec((None,) + shape, lambda i: (layer,) + (0,) * len(shape))
    return pl.pallas_call(
        functools.partial(_out_kernel, final=final),
        grid=(N // T,),
        in_specs=[
            row(D_MODEL), row(A_WIDTH), row(B_WIDTH + C_WIDTH),
            pl.BlockSpec((None, T, D_PLE), lambda i: (layer, i, 0)),
            full((D_MODEL, D_MODEL)), full((D_PLE, D_MODEL)), full((D_MODEL, D_MODEL)),
            pl.BlockSpec((1, D_MODEL), lambda i: (0, 0)),
        ],
        out_specs=row(D_MODEL),
        out_shape=jax.ShapeDtypeStruct((N, D_MODEL), F32),
        compiler_params=pltpu.CompilerParams(
            dimension_semantics=("arbitrary",), vmem_limit_bytes=VMEM_LIMIT),
        name="out_ple",
    )(h, ya, ybc, p, w_out, ple_w, gate_w, final_g)


def _block_diag(w):
    G, c, d = w.shape
    eye = jnp.eye(G, dtype=w.dtype)
    return (eye[:, None, :, None] * w[:, :, None, :]).reshape(G * c, G * d)


def kernel(x, p, norm_g, w_in, w_out, rel_bias, pool_w, pool_scale, sgu_w, sgu_b,
           ple_w, ple_gate_w, final_g):
    B, S, D = x.shape
    depth = w_in.shape[0]
    N = B * S
    bias_tiles = _bias_tiles(rel_bias)
    col = jnp.arange(D_IN)
    q_cols = (col >= Q0) & (col < Q0 + A_WIDTH)
    w_in_b = (w_in * jnp.where(q_cols, Q_SCALE, 1.0)).astype(BF16)
    pool_bd = jax.vmap(_block_diag)(pool_w).astype(BF16)
    sgu_wcat = jnp.transpose(jnp.tril(sgu_w), (0, 2, 1, 3)).reshape(
        depth, SGU_CHUNK, C_HEADS * SGU_CHUNK).astype(BF16)
    sgu_btile = jnp.repeat(jnp.transpose(sgu_b, (0, 2, 1)), HEAD_DIM, axis=2)
    w_out_b, ple_w_b, gate_w_b = (w.astype(BF16) for w in (w_out, ple_w, ple_gate_w))
    p_rows = p.reshape(depth, N, D_PLE)
    h = x
    for i in range(depth):
        q, k_blk, vt_blk, gsa, ybc = _proj_call(
            i, rel_bias, h, norm_g[:, None, :], w_in_b, pool_bd,
            pool_scale[:, None, :], sgu_wcat, sgu_btile)
        ya = _attn_call(q, k_blk, vt_blk, bias_tiles, gsa)
        h = _out_call(
            i, h.reshape(N, D), ya.reshape(N, A_WIDTH), ybc.reshape(N, B_WIDTH + C_WIDTH),
            p_rows, w_out_b, ple_w_b, gate_w_b, final_g[None, :],
            final=(i == depth - 1)).reshape(B, S, D)
    return h
```

```python
import functools
import math

import numpy as np
import jax
import jax.numpy as jnp
from jax import lax
from jax.experimental import pallas as pl
from jax.experimental.pallas import tpu as pltpu

D_MODEL = 1024
HEAD_DIM = 64
A_WIDTH = 512
A_HEADS = 8
MOBA_BLOCK = 256
MOBA_TOPK = 3
REL_BUCKETS = 32
REL_MAX_DIST = 128
B_WIDTH = 256
POOL_WINDOWS = (2, 4, 8, 16)
B_GROUP = 64
C_WIDTH = 256
C_HEADS = 4
SGU_CHUNK = 128
D_PLE = 256
D_IN = 3328
EPS = 1e-6
NEG = -1e30
LOG2E = math.log2(math.e)
Q_SCALE = HEAD_DIM ** -0.5 * LOG2E

MAX_WINDOW = max(POOL_WINDOWS)
ROW_BLOCKS = 2
ROW_TILE = ROW_BLOCKS * MOBA_BLOCK
OUT_TILE = 1024
LANES = 128
BF16_SUBLANES = 16
MAX_BLOCKS = (LANES - HEAD_DIM) // 2
MASK_HI = HEAD_DIM
MASK_LO = MASK_HI + MAX_BLOCKS
V_ROWS = HEAD_DIM + BF16_SUBLANES
ATTN_HEADS = 4
PAIR_UNROLLS = (8, 4, 2, 1)
VMEM_LIMIT = 48 * 1024 * 1024

_OFF = np.cumsum((0,) + (A_WIDTH,) * 4 + (B_WIDTH,) * 2 + (C_WIDTH,) * 3)
Q0, K0, V0, GA0, XB0, GB0, UC0, VC0, GC0, _ = (int(o) for o in _OFF)

BF16 = jnp.bfloat16
F32 = jnp.float32


def _bucket_thresholds():
    max_exact = REL_BUCKETS // 2
    n = np.arange(0, 4 * MOBA_BLOCK)
    nf = np.maximum(n, 1).astype(np.float64)
    large = max_exact + (np.log(nf / max_exact) / math.log(REL_MAX_DIST / max_exact)
                         * (REL_BUCKETS - max_exact)).astype(np.int64)
    large = np.minimum(large, REL_BUCKETS - 1)
    bucket = np.where(n < max_exact, n, large)
    return [int(np.argmax(bucket >= b)) for b in range(1, REL_BUCKETS)]


_THRESHOLDS = _bucket_thresholds()


def _sigmoid(x):
    return 0.5 * jnp.tanh(0.5 * x) + 0.5


def _silu(x):
    return x * _sigmoid(x)


def _split_bf16(x):
    hi = x.astype(BF16)
    lo = (x - hi.astype(F32)).astype(BF16)
    return hi, lo


def _dot(a, b):
    return jnp.dot(a, b, preferred_element_type=F32)


def _dot_tn(a, b):
    return lax.dot_general(a, b, (((0,), (1,)), ((), ())), preferred_element_type=F32)


def _dot_nt(a, b):
    return lax.dot_general(a, b, (((1,), (1,)), ((), ())), preferred_element_type=F32)


def _bias_kernel(rb_ref, out_ref):
    h = pl.program_id(0)
    key = lax.broadcasted_iota(jnp.int32, (MOBA_BLOCK, MOBA_BLOCK), 0)
    qry = lax.broadcasted_iota(jnp.int32, (MOBA_BLOCK, MOBA_BLOCK), 1)
    for kind, shift in ((0, 0), (1, MOBA_BLOCK)):
        dist = qry - key + shift
        val = jnp.full((MOBA_BLOCK, MOBA_BLOCK), rb_ref[0, h], F32)
        for b in range(1, REL_BUCKETS):
            val = jnp.where(dist >= _THRESHOLDS[b - 1], rb_ref[b, h], val)
        val = val * LOG2E
        if kind == 0:
            val = jnp.where(dist >= 0, val, NEG)
        out_ref[0, kind] = val


def _bias_tiles(rel_bias):
    return pl.pallas_call(
        _bias_kernel,
        grid=(A_HEADS,),
        in_specs=[pl.BlockSpec(memory_space=pltpu.SMEM)],
        out_specs=pl.BlockSpec((1, 2, MOBA_BLOCK, MOBA_BLOCK), lambda h: (h, 0, 0, 0)),
        out_shape=jax.ShapeDtypeStruct((A_HEADS, 2, MOBA_BLOCK, MOBA_BLOCK), F32),
        name="bias_tiles",
    )(rel_bias)


def _proj_kernel(rb_ref, h_ref, ng_ref, win_ref, poolw_ref, pscale_ref,
                 sguw_ref, sgub_ref,
                 q_ref, k_ref, vt_ref, gsa_ref, ybc_ref,
                 kmt_scr, ext_scr, *, n_blocks):
    T = ROW_TILE
    W = MOBA_BLOCK
    s = pl.program_id(1)
    first_block = s * ROW_BLOCKS

    @pl.when(s == 0)
    def _():
        kmt_scr[...] = jnp.zeros_like(kmt_scr)
        ext_scr[0:MAX_WINDOW, :] = jnp.zeros((MAX_WINDOW, B_WIDTH), F32)

    hbs, zks = [], []
    for blk in range(ROW_BLOCKS):
        h = h_ref[0, blk * W:(blk + 1) * W, :]
        hn = h * lax.rsqrt(jnp.mean(h * h, axis=-1, keepdims=True) + EPS) * ng_ref[...]
        hbs.append(hn.astype(BF16))
        zks.append(_dot(hbs[-1], win_ref[:, K0:K0 + A_WIDTH]))
    hb = jnp.concatenate(hbs, axis=0)

    def proj(c0, width):
        return _dot(hb, win_ref[:, c0:c0 + width])

    zqt = _dot_tn(win_ref[:, Q0:Q0 + A_WIDTH], hb)

    lane = lax.broadcasted_iota(jnp.int32, (1, LANES), 1)
    for blk, zk_blk in enumerate(zks):
        j = first_block + blk
        indicator = jnp.where((lane == MASK_HI + j) | (lane == MASK_LO + j), 1.0, 0.0)
        k_mean = jnp.mean(zk_blk, axis=0, keepdims=True)
        for pair in range(A_HEADS // 2):
            zk_pair = zk_blk[:, pair * LANES:(pair + 1) * LANES]
            zk_odd = pltpu.roll(zk_pair, HEAD_DIM, 1)
            k_ref[0, 2 * pair, blk] = jnp.where(lane < HEAD_DIM, zk_pair, indicator).astype(BF16)
            k_ref[0, 2 * pair + 1, blk] = jnp.where(lane < HEAD_DIM, zk_odd, indicator).astype(BF16)
            mean_pair = k_mean[:, pair * LANES:(pair + 1) * LANES]
            kmt_scr[2 * pair, pl.ds(j, 1), :] = mean_pair
            kmt_scr[2 * pair + 1, pl.ds(j, 1), :] = pltpu.roll(mean_pair, HEAD_DIM, 1)

    q_hi, q_lo = _split_bf16(zqt)
    gates = []
    for hd in range(A_HEADS):
        rows = slice(hd * HEAD_DIM, (hd + 1) * HEAD_DIM)
        km_hi, km_lo = _split_bf16(kmt_scr[hd][:, 0:HEAD_DIM])
        gates.append(_dot(km_hi, q_hi[rows]) + _dot(km_hi, q_lo[rows]) + _dot(km_lo, q_hi[rows]))

    vc = proj(VC0, C_WIDTH)
    zuc = proj(UC0, C_WIDTH)
    zgc = proj(GC0, C_WIDTH)
    xb = proj(XB0, B_WIDTH)
    zgb = proj(GB0, B_WIDTH)
    zgat = _dot_tn(win_ref[:, GA0:GA0 + A_WIDTH], hb)
    zvt = _dot_tn(win_ref[:, V0:V0 + A_WIDTH], hb)

    jrow = lax.broadcasted_iota(jnp.int32, (n_blocks, T), 0)
    own = first_block + lax.broadcasted_iota(jnp.int32, (1, T), 1) // W
    for hd in range(A_HEADS):
        g = jnp.where(jrow < own, gates[hd], NEG)
        sel = jnp.zeros((n_blocks, T), jnp.bool_)
        for _ in range(MOBA_TOPK):
            m = jnp.max(g, axis=0, keepdims=True)
            idx = jnp.min(jnp.where(g == m, jrow, n_blocks), axis=0, keepdims=True)
            pick = jrow == idx
            sel = sel | (pick & (m > NEG * 0.5))
            g = jnp.where(pick, -jnp.inf, g)
        far_bias = rb_ref[REL_BUCKETS - 1, hd] * LOG2E
        term = jnp.where(sel, jnp.where(jrow <= own - 2, far_bias, 0.0), NEG)
        term = jnp.where(jrow == own, 0.0, term)
        term_hi = term.astype(BF16).astype(F32)
        q_aug_t = jnp.concatenate(
            [zqt[hd * HEAD_DIM:(hd + 1) * HEAD_DIM], term_hi, term - term_hi], axis=0)
        q_ref[0, hd] = q_aug_t.T.astype(BF16)

    tail_row = lax.broadcasted_iota(jnp.int32, (V_ROWS - HEAD_DIM, W), 0)
    tail = jnp.where(tail_row == 0, 1.0, 0.0)
    for hd in range(A_HEADS):
        for blk in range(ROW_BLOCKS):
            vt_ref[0, hd, blk] = jnp.concatenate(
                [zvt[hd * HEAD_DIM:(hd + 1) * HEAD_DIM, blk * W:(blk + 1) * W], tail],
                axis=0).astype(BF16)

    gsa_ref[0] = _silu(zgat)

    ext_scr[MAX_WINDOW:MAX_WINDOW + T, :] = xb
    lane_b = lax.broadcasted_iota(jnp.int32, (1, B_WIDTH), 1)
    win = jnp.zeros((1, B_WIDTH), F32)
    for gi, w in enumerate(POOL_WINDOWS):
        win = jnp.where(lane_b // B_GROUP == gi, float(w), win)
    halves = []
    for half, (w_small, w_big) in enumerate(zip(POOL_WINDOWS[0::2], POOL_WINDOWS[1::2])):
        cols = slice(half * LANES, (half + 1) * LANES)
        run = xb[:, cols]
        for lag in range(1, w_small):
            run = run + ext_scr[MAX_WINDOW - lag:MAX_WINDOW - lag + T, cols]
        small = run
        for lag in range(w_small, w_big):
            run = run + ext_scr[MAX_WINDOW - lag:MAX_WINDOW - lag + T, cols]
        halves.append(jnp.where(lane < B_GROUP, small, run))
    wsum = jnp.concatenate(halves, axis=1)
    pos = (s * T + lax.broadcasted_iota(jnp.int32, (T, 1), 0) + 1).astype(F32)
    pooled = wsum / jnp.minimum(pos, win) - xb
    ext_scr[0:MAX_WINDOW, :] = xb[T - MAX_WINDOW:T, :]
    mixed_b = _dot(pooled.astype(BF16), poolw_ref[...]) * pscale_ref[...]
    ybc_ref[0, :, 0:B_WIDTH] = (mixed_b * _silu(zgb)).astype(BF16)

    mu = jnp.mean(vc, axis=-1, keepdims=True)
    cen = vc - mu
    var = jnp.mean(cen * cen, axis=-1, keepdims=True)
    vn = cen * lax.rsqrt(var + EPS)
    rows = lax.broadcasted_iota(jnp.int32, (C_HEADS * SGU_CHUNK, C_WIDTH), 0)
    cols = lax.broadcasted_iota(jnp.int32, (C_HEADS * SGU_CHUNK, C_WIDTH), 1)
    head_sel = (rows // SGU_CHUNK) == (cols // HEAD_DIM)
    ug = zuc * _silu(zgc)
    for c in range(T // SGU_CHUNK):
        vn_c = vn[c * SGU_CHUNK:(c + 1) * SGU_CHUNK]
        stack = jnp.where(head_sel, jnp.concatenate([vn_c] * C_HEADS, axis=0), 0.0)
        mixed_c = _dot(sguw_ref[...], stack.astype(BF16)) + sgub_ref[...]
        ybc_ref[0, c * SGU_CHUNK:(c + 1) * SGU_CHUNK, B_WIDTH:B_WIDTH + C_WIDTH] = (
            ug[c * SGU_CHUNK:(c + 1) * SGU_CHUNK] * mixed_c).astype(BF16)


def _proj_call(layer, rel_bias, h, norm_g, w_in, pool_bd, pool_scale, sgu_wcat, sgu_btile):
    B, S, _ = h.shape
    T = ROW_TILE
    nb = S // MOBA_BLOCK
    assert nb <= MAX_BLOCKS and S % T == 0
    full = lambda shape: pl.BlockSpec((None,) + shape, lambda b, s: (layer,) + (0,) * len(shape))
    return pl.pallas_call(
        functools.partial(_proj_kernel, n_blocks=nb),
        grid=(B, S // T),
        in_specs=[
            pl.BlockSpec(memory_space=pltpu.SMEM),
            pl.BlockSpec((1, T, D_MODEL), lambda b, s: (b, s, 0)),
            full((1, D_MODEL)),
            full((D_MODEL, D_IN)),
            full((B_WIDTH, B_WIDTH)),
            full((1, B_WIDTH)),
            full((SGU_CHUNK, C_HEADS * SGU_CHUNK)),
            full((SGU_CHUNK, C_WIDTH)),
        ],
        out_specs=[
            pl.BlockSpec((1, A_HEADS, T, LANES), lambda b, s: (b, 0, s, 0)),
            pl.BlockSpec((1, A_HEADS, ROW_BLOCKS, MOBA_BLOCK, LANES), lambda b, s: (b, 0, s, 0, 0)),
            pl.BlockSpec((1, A_HEADS, ROW_BLOCKS, V_ROWS, MOBA_BLOCK), lambda b, s: (b, 0, s, 0, 0)),
            pl.BlockSpec((1, A_WIDTH, T), lambda b, s: (b, 0, s)),
            pl.BlockSpec((1, T, B_WIDTH + C_WIDTH), lambda b, s: (b, s, 0)),
        ],
        out_shape=[
            jax.ShapeDtypeStruct((B, A_HEADS, S, LANES), BF16),
            jax.ShapeDtypeStruct((B, A_HEADS, nb, MOBA_BLOCK, LANES), BF16),
            jax.ShapeDtypeStruct((B, A_HEADS, nb, V_ROWS, MOBA_BLOCK), BF16),
            jax.ShapeDtypeStruct((B, A_WIDTH, S), F32),
            jax.ShapeDtypeStruct((B, S, B_WIDTH + C_WIDTH), BF16),
        ],
        scratch_shapes=[
            pltpu.VMEM((A_HEADS, nb, LANES), F32),
            pltpu.VMEM((MAX_WINDOW + T, B_WIDTH), F32),
        ],
        compiler_params=pltpu.CompilerParams(
            dimension_semantics=("arbitrary", "arbitrary"), vmem_limit_bytes=VMEM_LIMIT),
        name="proj_mix",
    )(rel_bias, h, norm_g, w_in, pool_bd, pool_scale, sgu_wcat, sgu_btile)


def _attn_kernel(q_ref, qnext_ref, k_ref, vt_ref, bias_ref, gsa_ref, o_ref,
                 sa_scr, sb_scr, m_scr, acc_scr):
    i = pl.program_id(2)
    W = MOBA_BLOCK
    n_pairs = jnp.maximum(i - 1, 0)
    parked = 2 * n_pairs
    before = jnp.maximum(2 * i - 1, 0)
    pen_parked = jnp.where(i >= 1, 0.0, NEG)
    pen_before = jnp.where(i >= 1, 0.0, NEG)
    heads = range(ATTN_HEADS)
    OWN, PREV = 0, 1

    BOTH = (0, 1)

    def park(hd, scr, j, bias_kinds=(None, None), halves=BOTH, queries=q_ref):
        rows = slice(halves[0] * W, (halves[-1] + 1) * W)
        st = _dot_nt(k_ref[0, hd, j], queries[0, hd, rows, :])
        for n, half in enumerate(halves):
            kind = bias_kinds[half]
            part = st[:, n * W:(n + 1) * W]
            cols = slice(half * W, (half + 1) * W)
            scr[hd, :, cols] = part if kind is None else part + bias_ref[hd, kind]

    def consume(hd, scr, j, pen=None, halves=BOTH):
        for half in halves:
            cols = slice(half * W, (half + 1) * W)
            ps, ms = [], []
            for quarter in range(2):
                qcols = slice(half * W + quarter * LANES, half * W + (quarter + 1) * LANES)
                st = scr[hd, :, qcols]
                cm = jnp.max(st, axis=0, keepdims=True)
                m_new = jnp.maximum(m_scr[hd, :, qcols], cm if pen is None else cm + pen)
                shift = m_new if pen is None else m_new - pen
                ps.append(jnp.exp2(st - shift).astype(BF16))
                ms.append(m_new)
            pv = _dot(vt_ref[0, hd, j], jnp.concatenate(ps, axis=1))
            m = m_scr[hd, :, cols]
            m_new = jnp.concatenate(ms, axis=1)
            m_scr[hd, :, cols] = m_new
            acc_scr[hd, :, cols] = jnp.exp2(m - m_new) * acc_scr[hd, :, cols] + pv

    def stage(produce=None, take=None):
        for hd in heads:
            if produce is not None:
                park(hd, *produce)
            if take is not None:
                consume(hd, *take)

    m_scr[...] = jnp.full(m_scr.shape, -jnp.inf, F32)
    acc_scr[...] = jnp.zeros(acc_scr.shape, F32)

    @pl.when(i == 0)
    def _():
        stage(produce=(sa_scr, 0))

    def pair_step(t, carry):
        stage(produce=(sb_scr, 2 * t + 1), take=(sa_scr, 2 * t))
        stage(produce=(sa_scr, 2 * t + 2), take=(sb_scr, 2 * t + 1))
        return carry

    done = 0
    for unroll in PAIR_UNROLLS:
        trips = (n_pairs - done) // unroll

        def body(u, carry, first=done, unroll=unroll):
            for r in range(unroll):
                pair_step(first + u * unroll + r, carry)
            return carry

        lax.fori_loop(0, trips, body, 0)
        done = done + trips * unroll

    second = (1,)
    stage(produce=(sb_scr, before, (PREV, None)), take=(sa_scr, parked, pen_parked))
    stage(produce=(sa_scr, 2 * i, (OWN, PREV)), take=(sb_scr, before, pen_before))
    stage(produce=(sb_scr, 2 * i + 1, (None, OWN), second), take=(sa_scr, 2 * i))
    stage(produce=(sa_scr, 0, (None, None), BOTH, qnext_ref), take=(sb_scr, 2 * i + 1, None, second))
    outs = [acc_scr[hd, 0:HEAD_DIM, :] / acc_scr[hd, HEAD_DIM:HEAD_DIM + 1, :] for hd in heads]
    o_ref[0] = (jnp.concatenate(outs, axis=0) * gsa_ref[0]).astype(BF16)


def _attn_call(q, k_blk, vt_blk, bias_tiles, gsa):
    B, H, S, _ = q.shape
    nb = S // MOBA_BLOCK
    Tq = ROW_TILE
    G = ATTN_HEADS
    last = S // Tq - 1
    return pl.pallas_call(
        _attn_kernel,
        grid=(B, H // G, S // Tq),
        in_specs=[
            pl.BlockSpec((1, G, Tq, LANES), lambda b, hg, i: (b, hg, i, 0)),
            pl.BlockSpec((1, G, Tq, LANES), lambda b, hg, i: (b, hg, jnp.minimum(i + 1, last), 0)),
            pl.BlockSpec((1, G, nb, MOBA_BLOCK, LANES), lambda b, hg, i: (b, hg, 0, 0, 0)),
            pl.BlockSpec((1, G, nb, V_ROWS, MOBA_BLOCK), lambda b, hg, i: (b, hg, 0, 0, 0)),
            pl.BlockSpec((G, 2, MOBA_BLOCK, MOBA_BLOCK), lambda b, hg, i: (hg, 0, 0, 0)),
            pl.BlockSpec((1, G * HEAD_DIM, Tq), lambda b, hg, i: (b, hg, i)),
        ],
        out_specs=pl.BlockSpec((1, G * HEAD_DIM, Tq), lambda b, hg, i: (b, hg, i)),
        out_shape=jax.ShapeDtypeStruct((B, H * HEAD_DIM, S), BF16),
        scratch_shapes=[
            pltpu.VMEM((G, MOBA_BLOCK, Tq), F32),
            pltpu.VMEM((G, MOBA_BLOCK, Tq), F32),
            pltpu.VMEM((G, 1, Tq), F32),
            pltpu.VMEM((G, V_ROWS, Tq), F32),
        ],
        compiler_params=pltpu.CompilerParams(
            dimension_semantics=("arbitrary", "arbitrary", "arbitrary"),
            vmem_limit_bytes=VMEM_LIMIT),
        name="moba_attn",
    )(q, q, k_blk, vt_blk, bias_tiles, gsa)


def _out_kernel(h_ref, ya_ref, ybc_ref, p_ref, wout_ref, plew_ref, gatew_ref,
                fg_ref, o_ref, *, final):
    y = (lax.dot_general(ya_ref[0], wout_ref[0:A_WIDTH, :], (((0,), (0,)), ((), ())),
                         preferred_element_type=F32)
         + _dot(ybc_ref[...], wout_ref[A_WIDTH:, :]))
    h1 = h_ref[...] + y
    emb = _dot(p_ref[...].astype(BF16), plew_ref[...])
    gate = _sigmoid(_dot(h1.astype(BF16), gatew_ref[...]))
    h2 = h1 + emb * gate
    if final:
        h2 = h2 * lax.rsqrt(jnp.mean(h2 * h2, axis=-1, keepdims=True) + EPS) * fg_ref[...]
    o_ref[...] = h2


def _out_call(layer, h, ya, ybc, p, w_out, ple_w, gate_w, final_g, final):
    N = h.shape[0]
    T = OUT_TILE
    tiles = ya.shape[2] // T
    row = lambda width: pl.BlockSpec((T, width), lambda i: (i, 0))
    full = lambda shape: pl.BlockSpec((None,) + shape, lambda i: (layer,) + (0,) * len(shape))
    return pl.pallas_call(
        functools.partial(_out_kernel, final=final),
        grid=(N // T,),
        in_specs=[
            row(D_MODEL),
            pl.BlockSpec((1, A_WIDTH, T), lambda i: (i // tiles, 0, i % tiles)),
            row(B_WIDTH + C_WIDTH),
            pl.BlockSpec((None, T, D_PLE), lambda i: (layer, i, 0)),
            full((D_MODEL, D_MODEL)), full((D_PLE, D_MODEL)), full((D_MODEL, D_MODEL)),
            pl.BlockSpec((1, D_MODEL), lambda i: (0, 0)),
        ],
        out_specs=row(D_MODEL),
        out_shape=jax.ShapeDtypeStruct((N, D_MODEL), F32),
        compiler_params=pltpu.CompilerParams(
            dimension_semantics=("arbitrary",), vmem_limit_bytes=VMEM_LIMIT),
        name="out_ple",
    )(h, ya, ybc, p, w_out, ple_w, gate_w, final_g)


def _block_diag(w):
    G, c, d = w.shape
    eye = jnp.eye(G, dtype=w.dtype)
    return (eye[:, None, :, None] * w[:, :, None, :]).reshape(G * c, G * d)


def kernel(x, p, norm_g, w_in, w_out, rel_bias, pool_w, pool_scale, sgu_w, sgu_b,
           ple_w, ple_gate_w, final_g):
    B, S, D = x.shape
    depth = w_in.shape[0]
    N = B * S
    bias_tiles = _bias_tiles(rel_bias)
    col = jnp.arange(D_IN)
    q_cols = (col >= Q0) & (col < Q0 + A_WIDTH)
    w_in_b = (w_in * jnp.where(q_cols, Q_SCALE, 1.0)).astype(BF16)
    pool_bd = jax.vmap(_block_diag)(pool_w).astype(BF16)
    sgu_wcat = jnp.transpose(jnp.tril(sgu_w), (0, 2, 1, 3)).reshape(
        depth, SGU_CHUNK, C_HEADS * SGU_CHUNK).astype(BF16)
    sgu_btile = jnp.repeat(jnp.transpose(sgu_b, (0, 2, 1)), HEAD_DIM, axis=2)
    w_out_b, ple_w_b, gate_w_b = (w.astype(BF16) for w in (w_out, ple_w, ple_gate_w))
    p_rows = p.reshape(depth, N, D_PLE)
    h = x
    for i in range(depth):
        q, k_blk, vt_blk, gsa, ybc = _proj_call(
            i, rel_bias, h, norm_g[:, None, :], w_in_b, pool_bd,
            pool_scale[:, None, :], sgu_wcat, sgu_btile)
        ya = _attn_call(q, k_blk, vt_blk, bias_tiles, gsa)
        h = _out_call(
            i, h.reshape(N, D), ya, ybc.reshape(N, B_WIDTH + C_WIDTH),
            p_rows, w_out_b, ple_w_b, gate_w_b, final_g[None, :],
            final=(i == depth - 1)).reshape(B, S, D)
    return h
```

```python
import functools
import math

import numpy as np
import jax
import jax.numpy as jnp
from jax import lax
from jax.experimental import pallas as pl
from jax.experimental.pallas import tpu as pltpu

D_MODEL = 1024
HEAD_DIM = 64
A_WIDTH = 512
A_HEADS = 8
MOBA_BLOCK = 256
MOBA_TOPK = 3
REL_BUCKETS = 32
REL_MAX_DIST = 128
B_WIDTH = 256
POOL_WINDOWS = (2, 4, 8, 16)
B_GROUP = 64
C_WIDTH = 256
C_HEADS = 4
SGU_CHUNK = 128
D_PLE = 256
D_IN = 3328
EPS = 1e-6
NEG = -1e30
LOG2E = math.log2(math.e)
Q_SCALE = HEAD_DIM ** -0.5 * LOG2E

MAX_WINDOW = max(POOL_WINDOWS)
ROW_BLOCKS = 2
ROW_TILE = ROW_BLOCKS * MOBA_BLOCK
OUT_TILE = 1024
LANES = 128
BF16_SUBLANES = 16
MAX_BLOCKS = (LANES - HEAD_DIM) // 2
MASK_HI = HEAD_DIM
MASK_LO = MASK_HI + MAX_BLOCKS
V_ROWS = HEAD_DIM + BF16_SUBLANES
ATTN_HEADS = 4
PAIR_UNROLLS = (8, 4, 2, 1)
VMEM_LIMIT = 48 * 1024 * 1024

_OFF = np.cumsum((0,) + (A_WIDTH,) * 4 + (B_WIDTH,) * 2 + (C_WIDTH,) * 3)
Q0, K0, V0, GA0, XB0, GB0, UC0, VC0, GC0, _ = (int(o) for o in _OFF)

BF16 = jnp.bfloat16
F32 = jnp.float32


def _bucket_thresholds():
    max_exact = REL_BUCKETS // 2
    n = np.arange(0, 4 * MOBA_BLOCK)
    nf = np.maximum(n, 1).astype(np.float64)
    large = max_exact + (np.log(nf / max_exact) / math.log(REL_MAX_DIST / max_exact)
                         * (REL_BUCKETS - max_exact)).astype(np.int64)
    large = np.minimum(large, REL_BUCKETS - 1)
    bucket = np.where(n < max_exact, n, large)
    return [int(np.argmax(bucket >= b)) for b in range(1, REL_BUCKETS)]


_THRESHOLDS = _bucket_thresholds()


def _sigmoid(x):
    return 0.5 * jnp.tanh(0.5 * x) + 0.5


def _silu(x):
    return x * _sigmoid(x)


def _split_bf16(x):
    hi = x.astype(BF16)
    lo = (x - hi.astype(F32)).astype(BF16)
    return hi, lo


def _dot(a, b):
    return jnp.dot(a, b, preferred_element_type=F32)


def _dot_tn(a, b):
    return lax.dot_general(a, b, (((0,), (1,)), ((), ())), preferred_element_type=F32)


def _bias_kernel(rb_ref, out_ref):
    h = pl.program_id(0)
    key = lax.broadcasted_iota(jnp.int32, (MOBA_BLOCK, MOBA_BLOCK), 0)
    qry = lax.broadcasted_iota(jnp.int32, (MOBA_BLOCK, MOBA_BLOCK), 1)
    for kind, shift in ((0, 0), (1, MOBA_BLOCK)):
        dist = qry - key + shift
        val = jnp.full((MOBA_BLOCK, MOBA_BLOCK), rb_ref[0, h], F32)
        for b in range(1, REL_BUCKETS):
            val = jnp.where(dist >= _THRESHOLDS[b - 1], rb_ref[b, h], val)
        val = val * LOG2E
        if kind == 0:
            val = jnp.where(dist >= 0, val, NEG)
        out_ref[0, kind] = val


def _bias_tiles(rel_bias):
    return pl.pallas_call(
        _bias_kernel,
        grid=(A_HEADS,),
        in_specs=[pl.BlockSpec(memory_space=pltpu.SMEM)],
        out_specs=pl.BlockSpec((1, 2, MOBA_BLOCK, MOBA_BLOCK), lambda h: (h, 0, 0, 0)),
        out_shape=jax.ShapeDtypeStruct((A_HEADS, 2, MOBA_BLOCK, MOBA_BLOCK), F32),
        name="bias_tiles",
    )(rel_bias)


def _proj_kernel(rb_ref, h_ref, ng_ref, win_ref, poolw_ref, pscale_ref,
                 sguw_ref, sgub_ref,
                 q_ref, k_ref, vt_ref, gsa_ref, ybc_ref,
                 kmt_scr, ext_scr, *, n_blocks):
    T = ROW_TILE
    W = MOBA_BLOCK
    s = pl.program_id(1)
    first_block = s * ROW_BLOCKS

    @pl.when(s == 0)
    def _():
        kmt_scr[...] = jnp.zeros_like(kmt_scr)
        ext_scr[0:MAX_WINDOW, :] = jnp.zeros((MAX_WINDOW, B_WIDTH), F32)

    hbs, zks = [], []
    for blk in range(ROW_BLOCKS):
        h = h_ref[0, blk * W:(blk + 1) * W, :]
        hn = h * lax.rsqrt(jnp.mean(h * h, axis=-1, keepdims=True) + EPS) * ng_ref[...]
        hbs.append(hn.astype(BF16))
        zks.append(_dot(hbs[-1], win_ref[:, K0:K0 + A_WIDTH]))
    hb = jnp.concatenate(hbs, axis=0)

    def proj(c0, width):
        return _dot(hb, win_ref[:, c0:c0 + width])

    zqt = _dot_tn(win_ref[:, Q0:Q0 + A_WIDTH], hb)

    lane = lax.broadcasted_iota(jnp.int32, (1, LANES), 1)
    for blk, zk_blk in enumerate(zks):
        j = first_block + blk
        indicator = jnp.where((lane == MASK_HI + j) | (lane == MASK_LO + j), 1.0, 0.0)
        k_mean = jnp.mean(zk_blk, axis=0, keepdims=True)
        for pair in range(A_HEADS // 2):
            zk_pair = zk_blk[:, pair * LANES:(pair + 1) * LANES]
            zk_odd = pltpu.roll(zk_pair, HEAD_DIM, 1)
            k_ref[0, 2 * pair, blk] = jnp.where(lane < HEAD_DIM, zk_pair, indicator).astype(BF16)
            k_ref[0, 2 * pair + 1, blk] = jnp.where(lane < HEAD_DIM, zk_odd, indicator).astype(BF16)
            mean_pair = k_mean[:, pair * LANES:(pair + 1) * LANES]
            kmt_scr[2 * pair, pl.ds(j, 1), :] = mean_pair
            kmt_scr[2 * pair + 1, pl.ds(j, 1), :] = pltpu.roll(mean_pair, HEAD_DIM, 1)

    q_hi, q_lo = _split_bf16(zqt)
    gates = []
    for hd in range(A_HEADS):
        rows = slice(hd * HEAD_DIM, (hd + 1) * HEAD_DIM)
        km_hi, km_lo = _split_bf16(kmt_scr[hd][:, 0:HEAD_DIM])
        gates.append(_dot(km_hi, q_hi[rows]) + _dot(km_hi, q_lo[rows]) + _dot(km_lo, q_hi[rows]))

    vc = proj(VC0, C_WIDTH)
    zuc = proj(UC0, C_WIDTH)
    zgc = proj(GC0, C_WIDTH)
    xb = proj(XB0, B_WIDTH)
    zgb = proj(GB0, B_WIDTH)
    zgat = _dot_tn(win_ref[:, GA0:GA0 + A_WIDTH], hb)
    zvt = _dot_tn(win_ref[:, V0:V0 + A_WIDTH], hb)

    jrow = lax.broadcasted_iota(jnp.int32, (n_blocks, T), 0)
    own = first_block + lax.broadcasted_iota(jnp.int32, (1, T), 1) // W
    for hd in range(A_HEADS):
        g = jnp.where(jrow < own, gates[hd], NEG)
        sel = jnp.zeros((n_blocks, T), jnp.bool_)
        for _ in range(MOBA_TOPK):
            m = jnp.max(g, axis=0, keepdims=True)
            idx = jnp.min(jnp.where(g == m, jrow, n_blocks), axis=0, keepdims=True)
            pick = jrow == idx
            sel = sel | (pick & (m > NEG * 0.5))
            g = jnp.where(pick, -jnp.inf, g)
        far_bias = rb_ref[REL_BUCKETS - 1, hd] * LOG2E
        term = jnp.where(sel, jnp.where(jrow <= own - 2, far_bias, 0.0), NEG)
        term = jnp.where(jrow == own, 0.0, term)
        term_hi = term.astype(BF16).astype(F32)
        q_aug_t = jnp.concatenate(
            [zqt[hd * HEAD_DIM:(hd + 1) * HEAD_DIM], term_hi, term - term_hi], axis=0)
        q_ref[0, hd] = q_aug_t.astype(BF16)

    tail_row = lax.broadcasted_iota(jnp.int32, (V_ROWS - HEAD_DIM, W), 0)
    tail = jnp.where(tail_row == 0, 1.0, 0.0)
    for hd in range(A_HEADS):
        for blk in range(ROW_BLOCKS):
            vt_ref[0, hd, blk] = jnp.concatenate(
                [zvt[hd * HEAD_DIM:(hd + 1) * HEAD_DIM, blk * W:(blk + 1) * W], tail],
                axis=0).astype(BF16)

    gsa_ref[0] = _silu(zgat)

    ext_scr[MAX_WINDOW:MAX_WINDOW + T, :] = xb
    lane_b = lax.broadcasted_iota(jnp.int32, (1, B_WIDTH), 1)
    win = jnp.zeros((1, B_WIDTH), F32)
    for gi, w in enumerate(POOL_WINDOWS):
        win = jnp.where(lane_b // B_GROUP == gi, float(w), win)
    halves = []
    for half, (w_small, w_big) in enumerate(zip(POOL_WINDOWS[0::2], POOL_WINDOWS[1::2])):
        cols = slice(half * LANES, (half + 1) * LANES)
        run = xb[:, cols]
        for lag in range(1, w_small):
            run = run + ext_scr[MAX_WINDOW - lag:MAX_WINDOW - lag + T, cols]
        small = run
        for lag in range(w_small, w_big):
            run = run + ext_scr[MAX_WINDOW - lag:MAX_WINDOW - lag + T, cols]
        halves.append(jnp.where(lane < B_GROUP, small, run))
    wsum = jnp.concatenate(halves, axis=1)
    pos = (s * T + lax.broadcasted_iota(jnp.int32, (T, 1), 0) + 1).astype(F32)
    pooled = wsum / jnp.minimum(pos, win) - xb
    ext_scr[0:MAX_WINDOW, :] = xb[T - MAX_WINDOW:T, :]
    mixed_b = _dot(pooled.astype(BF16), poolw_ref[...]) * pscale_ref[...]
    ybc_ref[0, :, 0:B_WIDTH] = (mixed_b * _silu(zgb)).astype(BF16)

    mu = jnp.mean(vc, axis=-1, keepdims=True)
    cen = vc - mu
    var = jnp.mean(cen * cen, axis=-1, keepdims=True)
    vn = cen * lax.rsqrt(var + EPS)
    rows = lax.broadcasted_iota(jnp.int32, (C_HEADS * SGU_CHUNK, C_WIDTH), 0)
    cols = lax.broadcasted_iota(jnp.int32, (C_HEADS * SGU_CHUNK, C_WIDTH), 1)
    head_sel = (rows // SGU_CHUNK) == (cols // HEAD_DIM)
    ug = zuc * _silu(zgc)
    for c in range(T // SGU_CHUNK):
        vn_c = vn[c * SGU_CHUNK:(c + 1) * SGU_CHUNK]
        stack = jnp.where(head_sel, jnp.concatenate([vn_c] * C_HEADS, axis=0), 0.0)
        mixed_c = _dot(sguw_ref[...], stack.astype(BF16)) + sgub_ref[...]
        ybc_ref[0, c * SGU_CHUNK:(c + 1) * SGU_CHUNK, B_WIDTH:B_WIDTH + C_WIDTH] = (
            ug[c * SGU_CHUNK:(c + 1) * SGU_CHUNK] * mixed_c).astype(BF16)


def _proj_call(layer, rel_bias, h, norm_g, w_in, pool_bd, pool_scale, sgu_wcat, sgu_btile):
    B, S, _ = h.shape
    T = ROW_TILE
    nb = S // MOBA_BLOCK
    assert nb <= MAX_BLOCKS and S % T == 0
    full = lambda shape: pl.BlockSpec((None,) + shape, lambda b, s: (layer,) + (0,) * len(shape))
    return pl.pallas_call(
        functools.partial(_proj_kernel, n_blocks=nb),
        grid=(B, S // T),
        in_specs=[
            pl.BlockSpec(memory_space=pltpu.SMEM),
            pl.BlockSpec((1, T, D_MODEL), lambda b, s: (b, s, 0)),
            full((1, D_MODEL)),
            full((D_MODEL, D_IN)),
            full((B_WIDTH, B_WIDTH)),
            full((1, B_WIDTH)),
            full((SGU_CHUNK, C_HEADS * SGU_CHUNK)),
            full((SGU_CHUNK, C_WIDTH)),
        ],
        out_specs=[
            pl.BlockSpec((1, A_HEADS, LANES, T), lambda b, s: (b, 0, 0, s)),
            pl.BlockSpec((1, A_HEADS, ROW_BLOCKS, MOBA_BLOCK, LANES), lambda b, s: (b, 0, s, 0, 0)),
            pl.BlockSpec((1, A_HEADS, ROW_BLOCKS, V_ROWS, MOBA_BLOCK), lambda b, s: (b, 0, s, 0, 0)),
            pl.BlockSpec((1, A_WIDTH, T), lambda b, s: (b, 0, s)),
            pl.BlockSpec((1, T, B_WIDTH + C_WIDTH), lambda b, s: (b, s, 0)),
        ],
        out_shape=[
            jax.ShapeDtypeStruct((B, A_HEADS, LANES, S), BF16),
            jax.ShapeDtypeStruct((B, A_HEADS, nb, MOBA_BLOCK, LANES), BF16),
            jax.ShapeDtypeStruct((B, A_HEADS, nb, V_ROWS, MOBA_BLOCK), BF16),
            jax.ShapeDtypeStruct((B, A_WIDTH, S), F32),
            jax.ShapeDtypeStruct((B, S, B_WIDTH + C_WIDTH), BF16),
        ],
        scratch_shapes=[
            pltpu.VMEM((A_HEADS, nb, LANES), F32),
            pltpu.VMEM((MAX_WINDOW + T, B_WIDTH), F32),
        ],
        compiler_params=pltpu.CompilerParams(
            dimension_semantics=("arbitrary", "arbitrary"), vmem_limit_bytes=VMEM_LIMIT),
        name="proj_mix",
    )(rel_bias, h, norm_g, w_in, pool_bd, pool_scale, sgu_wcat, sgu_btile)


def _attn_kernel(q_ref, qnext_ref, k_ref, vt_ref, bias_ref, gsa_ref, o_ref,
                 sa_scr, sb_scr, m_scr, acc_scr):
    i = pl.program_id(2)
    W = MOBA_BLOCK
    n_pairs = jnp.maximum(i - 1, 0)
    parked = 2 * n_pairs
    before = jnp.maximum(2 * i - 1, 0)
    pen_parked = jnp.where(i >= 1, 0.0, NEG)
    pen_before = jnp.where(i >= 1, 0.0, NEG)
    heads = range(ATTN_HEADS)
    OWN, PREV = 0, 1

    BOTH = (0, 1)

    def park(hd, scr, j, bias_kinds=(None, None), halves=BOTH, queries=q_ref):
        rows = slice(halves[0] * W, (halves[-1] + 1) * W)
        st = _dot(k_ref[0, hd, j], queries[0, hd, :, rows])
        for n, half in enumerate(halves):
            kind = bias_kinds[half]
            part = st[:, n * W:(n + 1) * W]
            cols = slice(half * W, (half + 1) * W)
            scr[hd, :, cols] = part if kind is None else part + bias_ref[hd, kind]

    def consume(hd, scr, j, pen=None, halves=BOTH):
        for half in halves:
            cols = slice(half * W, (half + 1) * W)
            ps, ms = [], []
            for quarter in range(2):
                qcols = slice(half * W + quarter * LANES, half * W + (quarter + 1) * LANES)
                st = scr[hd, :, qcols]
                cm = jnp.max(st, axis=0, keepdims=True)
                m_new = jnp.maximum(m_scr[hd, :, qcols], cm if pen is None else cm + pen)
                shift = m_new if pen is None else m_new - pen
                ps.append(jnp.exp2(st - shift).astype(BF16))
                ms.append(m_new)
            pv = _dot(vt_ref[0, hd, j], jnp.concatenate(ps, axis=1))
            m = m_scr[hd, :, cols]
            m_new = jnp.concatenate(ms, axis=1)
            m_scr[hd, :, cols] = m_new
            acc_scr[hd, :, cols] = jnp.exp2(m - m_new) * acc_scr[hd, :, cols] + pv

    def stage(produce=None, take=None):
        for hd in heads:
            if produce is not None:
                park(hd, *produce)
            if take is not None:
                consume(hd, *take)

    m_scr[...] = jnp.full(m_scr.shape, -jnp.inf, F32)
    acc_scr[...] = jnp.zeros(acc_scr.shape, F32)

    @pl.when(i == 0)
    def _():
        stage(produce=(sa_scr, 0))

    def pair_step(t, carry):
        stage(produce=(sb_scr, 2 * t + 1), take=(sa_scr, 2 * t))
        stage(produce=(sa_scr, 2 * t + 2), take=(sb_scr, 2 * t + 1))
        return carry

    done = 0
    for unroll in PAIR_UNROLLS:
        trips = (n_pairs - done) // unroll

        def body(u, carry, first=done, unroll=unroll):
            for r in range(unroll):
                pair_step(first + u * unroll + r, carry)
            return carry

        lax.fori_loop(0, trips, body, 0)
        done = done + trips * unroll

    second = (1,)
    stage(produce=(sb_scr, before, (PREV, None)), take=(sa_scr, parked, pen_parked))
    stage(produce=(sa_scr, 2 * i, (OWN, PREV)), take=(sb_scr, before, pen_before))
    stage(produce=(sb_scr, 2 * i + 1, (None, OWN), second), take=(sa_scr, 2 * i))
    stage(produce=(sa_scr, 0, (None, None), BOTH, qnext_ref), take=(sb_scr, 2 * i + 1, None, second))
    outs = [acc_scr[hd, 0:HEAD_DIM, :] / acc_scr[hd, HEAD_DIM:HEAD_DIM + 1, :] for hd in heads]
    o_ref[0] = (jnp.concatenate(outs, axis=0) * gsa_ref[0]).astype(BF16)


def _attn_call(q, k_blk, vt_blk, bias_tiles, gsa):
    B, H, _, S = q.shape
    nb = S // MOBA_BLOCK
    Tq = ROW_TILE
    G = ATTN_HEADS
    last = S // Tq - 1
    return pl.pallas_call(
        _attn_kernel,
        grid=(B, H // G, S // Tq),
        in_specs=[
            pl.BlockSpec((1, G, LANES, Tq), lambda b, hg, i: (b, hg, 0, i)),
            pl.BlockSpec((1, G, LANES, Tq), lambda b, hg, i: (b, hg, 0, jnp.minimum(i + 1, last))),
            pl.BlockSpec((1, G, nb, MOBA_BLOCK, LANES), lambda b, hg, i: (b, hg, 0, 0, 0)),
            pl.BlockSpec((1, G, nb, V_ROWS, MOBA_BLOCK), lambda b, hg, i: (b, hg, 0, 0, 0)),
            pl.BlockSpec((G, 2, MOBA_BLOCK, MOBA_BLOCK), lambda b, hg, i: (hg, 0, 0, 0)),
            pl.BlockSpec((1, G * HEAD_DIM, Tq), lambda b, hg, i: (b, hg, i)),
        ],
        out_specs=pl.BlockSpec((1, G * HEAD_DIM, Tq), lambda b, hg, i: (b, hg, i)),
        out_shape=jax.ShapeDtypeStruct((B, H * HEAD_DIM, S), BF16),
        scratch_shapes=[
            pltpu.VMEM((G, MOBA_BLOCK, Tq), F32),
            pltpu.VMEM((G, MOBA_BLOCK, Tq), F32),
            pltpu.VMEM((G, 1, Tq), F32),
            pltpu.VMEM((G, V_ROWS, Tq), F32),
        ],
        compiler_params=pltpu.CompilerParams(
            dimension_semantics=("arbitrary", "arbitrary", "arbitrary"),
            vmem_limit_bytes=VMEM_LIMIT),
        name="moba_attn",
    )(q, q, k_blk, vt_blk, bias_tiles, gsa)


def _out_kernel(h_ref, ya_ref, ybc_ref, p_ref, wout_ref, plew_ref, gatew_ref,
                fg_ref, o_ref, *, final):
    y = (lax.dot_general(ya_ref[0], wout_ref[0:A_WIDTH, :], (((0,), (0,)), ((), ())),
                         preferred_element_type=F32)
         + _dot(ybc_ref[...], wout_ref[A_WIDTH:, :]))
    h1 = h_ref[...] + y
    emb = _dot(p_ref[...].astype(BF16), plew_ref[...])
    gate = _sigmoid(_dot(h1.astype(BF16), gatew_ref[...]))
    h2 = h1 + emb * gate
    if final:
        h2 = h2 * lax.rsqrt(jnp.mean(h2 * h2, axis=-1, keepdims=True) + EPS) * fg_ref[...]
    o_ref[...] = h2


def _out_call(layer, h, ya, ybc, p, w_out, ple_w, gate_w, final_g, final):
    N = h.shape[0]
    T = OUT_TILE
    tiles = ya.shape[2] // T
    row = lambda width: pl.BlockSpec((T, width), lambda i: (i, 0))
    full = lambda shape: pl.BlockSpec((None,) + shape, lambda i: (layer,) + (0,) * len(shape))
    return pl.pallas_call(
        functools.partial(_out_kernel, final=final),
        grid=(N // T,),
        in_specs=[
            row(D_MODEL),
            pl.BlockSpec((1, A_WIDTH, T), lambda i: (i // tiles, 0, i % tiles)),
            row(B_WIDTH + C_WIDTH),
            pl.BlockSpec((None, T, D_PLE), lambda i: (layer, i, 0)),
            full((D_MODEL, D_MODEL)), full((D_PLE, D_MODEL)), full((D_MODEL, D_MODEL)),
            pl.BlockSpec((1, D_MODEL), lambda i: (0, 0)),
        ],
        out_specs=row(D_MODEL),
        out_shape=jax.ShapeDtypeStruct((N, D_MODEL), F32),
        compiler_params=pltpu.CompilerParams(
            dimension_semantics=("arbitrary",), vmem_limit_bytes=VMEM_LIMIT),
        name="out_ple",
    )(h, ya, ybc, p, w_out, ple_w, gate_w, final_g)


def _block_diag(w):
    G, c, d = w.shape
    eye = jnp.eye(G, dtype=w.dtype)
    return (eye[:, None, :, None] * w[:, :, None, :]).reshape(G * c, G * d)


def kernel(x, p, norm_g, w_in, w_out, rel_bias, pool_w, pool_scale, sgu_w, sgu_b,
           ple_w, ple_gate_w, final_g):
    B, S, D = x.shape
    depth = w_in.shape[0]
    N = B * S
    bias_tiles = _bias_tiles(rel_bias)
    col = jnp.arange(D_IN)
    q_cols = (col >= Q0) & (col < Q0 + A_WIDTH)
    w_in_b = (w_in * jnp.where(q_cols, Q_SCALE, 1.0)).astype(BF16)
    pool_bd = jax.vmap(_block_diag)(pool_w).astype(BF16)
    sgu_wcat = jnp.transpose(jnp.tril(sgu_w), (0, 2, 1, 3)).reshape(
        depth, SGU_CHUNK, C_HEADS * SGU_CHUNK).astype(BF16)
    sgu_btile = jnp.repeat(jnp.transpose(sgu_b, (0, 2, 1)), HEAD_DIM, axis=2)
    w_out_b, ple_w_b, gate_w_b = (w.astype(BF16) for w in (w_out, ple_w, ple_gate_w))
    p_rows = p.reshape(depth, N, D_PLE)
    h = x
    for i in range(depth):
        q, k_blk, vt_blk, gsa, ybc = _proj_call(
            i, rel_bias, h, norm_g[:, None, :], w_in_b, pool_bd,
            pool_scale[:, None, :], sgu_wcat, sgu_btile)
        ya = _attn_call(q, k_blk, vt_blk, bias_tiles, gsa)
        h = _out_call(
            i, h.reshape(N, D), ya, ybc.reshape(N, B_WIDTH + C_WIDTH),
            p_rows, w_out_b, ple_w_b, gate_w_b, final_g[None, :],
            final=(i == depth - 1)).reshape(B, S, D)
    return h
```

```python
import functools
import math

import numpy as np
import jax
import jax.numpy as jnp
from jax import lax
from jax.experimental import pallas as pl
from jax.experimental.pallas import tpu as pltpu

D_MODEL = 1024
HEAD_DIM = 64
A_WIDTH = 512
A_HEADS = 8
MOBA_BLOCK = 256
MOBA_TOPK = 3
REL_BUCKETS = 32
REL_MAX_DIST = 128
B_WIDTH = 256
POOL_WINDOWS = (2, 4, 8, 16)
B_GROUP = 64
C_WIDTH = 256
C_HEADS = 4
SGU_CHUNK = 128
D_PLE = 256
D_IN = 3328
EPS = 1e-6
NEG = -1e30
LOG2E = math.log2(math.e)
Q_SCALE = HEAD_DIM ** -0.5 * LOG2E

MAX_WINDOW = max(POOL_WINDOWS)
ROW_BLOCKS = 4
ROW_TILE = ROW_BLOCKS * MOBA_BLOCK
ATTN_TILE = 2 * MOBA_BLOCK
OUT_TILE = 1024
LANES = 128
BF16_SUBLANES = 16
MAX_BLOCKS = (LANES - HEAD_DIM) // 2
MASK_HI = HEAD_DIM
MASK_LO = MASK_HI + MAX_BLOCKS
V_ROWS = HEAD_DIM + BF16_SUBLANES
ATTN_HEADS = 4
PAIR_UNROLLS = (8, 4, 2, 1)
VMEM_LIMIT = 48 * 1024 * 1024

_OFF = np.cumsum((0,) + (A_WIDTH,) * 4 + (B_WIDTH,) * 2 + (C_WIDTH,) * 3)
Q0, K0, V0, GA0, XB0, GB0, UC0, VC0, GC0, _ = (int(o) for o in _OFF)

BF16 = jnp.bfloat16
F32 = jnp.float32


def _bucket_thresholds():
    max_exact = REL_BUCKETS // 2
    n = np.arange(0, 4 * MOBA_BLOCK)
    nf = np.maximum(n, 1).astype(np.float64)
    large = max_exact + (np.log(nf / max_exact) / math.log(REL_MAX_DIST / max_exact)
                         * (REL_BUCKETS - max_exact)).astype(np.int64)
    large = np.minimum(large, REL_BUCKETS - 1)
    bucket = np.where(n < max_exact, n, large)
    return [int(np.argmax(bucket >= b)) for b in range(1, REL_BUCKETS)]


_THRESHOLDS = _bucket_thresholds()


def _sigmoid(x):
    return 0.5 * jnp.tanh(0.5 * x) + 0.5


def _silu(x):
    return x * _sigmoid(x)


def _split_bf16(x):
    hi = x.astype(BF16)
    lo = (x - hi.astype(F32)).astype(BF16)
    return hi, lo


def _dot(a, b):
    return jnp.dot(a, b, preferred_element_type=F32)


def _dot_tn(a, b):
    return lax.dot_general(a, b, (((0,), (1,)), ((), ())), preferred_element_type=F32)


def _bias_kernel(rb_ref, out_ref):
    h = pl.program_id(0)
    key = lax.broadcasted_iota(jnp.int32, (MOBA_BLOCK, MOBA_BLOCK), 0)
    qry = lax.broadcasted_iota(jnp.int32, (MOBA_BLOCK, MOBA_BLOCK), 1)
    for kind, shift in ((0, 0), (1, MOBA_BLOCK)):
        dist = qry - key + shift
        val = jnp.full((MOBA_BLOCK, MOBA_BLOCK), rb_ref[0, h], F32)
        for b in range(1, REL_BUCKETS):
            val = jnp.where(dist >= _THRESHOLDS[b - 1], rb_ref[b, h], val)
        val = val * LOG2E
        if kind == 0:
            val = jnp.where(dist >= 0, val, NEG)
        out_ref[0, kind] = val


def _bias_tiles(rel_bias):
    return pl.pallas_call(
        _bias_kernel,
        grid=(A_HEADS,),
        in_specs=[pl.BlockSpec(memory_space=pltpu.SMEM)],
        out_specs=pl.BlockSpec((1, 2, MOBA_BLOCK, MOBA_BLOCK), lambda h: (h, 0, 0, 0)),
        out_shape=jax.ShapeDtypeStruct((A_HEADS, 2, MOBA_BLOCK, MOBA_BLOCK), F32),
        name="bias_tiles",
    )(rel_bias)


def _proj_kernel(rb_ref, h_ref, ng_ref, win_ref, poolw_ref, pscale_ref,
                 sguw_ref, sgub_ref,
                 q_ref, k_ref, vt_ref, gsa_ref, ybc_ref,
                 kmt_scr, ext_scr, *, n_blocks):
    T = ROW_TILE
    W = MOBA_BLOCK
    s = pl.program_id(1)
    first_block = s * ROW_BLOCKS

    @pl.when(s == 0)
    def _():
        kmt_scr[...] = jnp.zeros_like(kmt_scr)
        ext_scr[0:MAX_WINDOW, :] = jnp.zeros((MAX_WINDOW, B_WIDTH), F32)

    hbs, zks = [], []
    for blk in range(ROW_BLOCKS):
        h = h_ref[0, blk * W:(blk + 1) * W, :]
        hn = h * lax.rsqrt(jnp.mean(h * h, axis=-1, keepdims=True) + EPS) * ng_ref[...]
        hbs.append(hn.astype(BF16))
        zks.append(_dot(hbs[-1], win_ref[:, K0:K0 + A_WIDTH]))
    hb = jnp.concatenate(hbs, axis=0)

    def proj(c0, width):
        return _dot(hb, win_ref[:, c0:c0 + width])

    zqt = _dot_tn(win_ref[:, Q0:Q0 + A_WIDTH], hb)

    lane = lax.broadcasted_iota(jnp.int32, (1, LANES), 1)
    for blk, zk_blk in enumerate(zks):
        j = first_block + blk
        indicator = jnp.where((lane == MASK_HI + j) | (lane == MASK_LO + j), 1.0, 0.0)
        k_mean = jnp.mean(zk_blk, axis=0, keepdims=True)
        for pair in range(A_HEADS // 2):
            zk_pair = zk_blk[:, pair * LANES:(pair + 1) * LANES]
            zk_odd = pltpu.roll(zk_pair, HEAD_DIM, 1)
            k_ref[0, 2 * pair, blk] = jnp.where(lane < HEAD_DIM, zk_pair, indicator).astype(BF16)
            k_ref[0, 2 * pair + 1, blk] = jnp.where(lane < HEAD_DIM, zk_odd, indicator).astype(BF16)
            mean_pair = k_mean[:, pair * LANES:(pair + 1) * LANES]
            kmt_scr[2 * pair, pl.ds(j, 1), :] = mean_pair
            kmt_scr[2 * pair + 1, pl.ds(j, 1), :] = pltpu.roll(mean_pair, HEAD_DIM, 1)

    q_hi, q_lo = _split_bf16(zqt)
    gates = []
    for hd in range(A_HEADS):
        rows = slice(hd * HEAD_DIM, (hd + 1) * HEAD_DIM)
        km_hi, km_lo = _split_bf16(kmt_scr[hd][:, 0:HEAD_DIM])
        gates.append(_dot(km_hi, q_hi[rows]) + _dot(km_hi, q_lo[rows]) + _dot(km_lo, q_hi[rows]))

    vc = proj(VC0, C_WIDTH)
    zuc = proj(UC0, C_WIDTH)
    zgc = proj(GC0, C_WIDTH)
    xb = proj(XB0, B_WIDTH)
    zgb = proj(GB0, B_WIDTH)
    zgat = _dot_tn(win_ref[:, GA0:GA0 + A_WIDTH], hb)
    zvt = _dot_tn(win_ref[:, V0:V0 + A_WIDTH], hb)

    jrow = lax.broadcasted_iota(jnp.int32, (n_blocks, T), 0)
    own = first_block + lax.broadcasted_iota(jnp.int32, (1, T), 1) // W
    for hd in range(A_HEADS):
        g = jnp.where(jrow < own, gates[hd], NEG)
        sel = jnp.zeros((n_blocks, T), jnp.bool_)
        for _ in range(MOBA_TOPK):
            m = jnp.max(g, axis=0, keepdims=True)
            idx = jnp.min(jnp.where(g == m, jrow, n_blocks), axis=0, keepdims=True)
            pick = jrow == idx
            sel = sel | (pick & (m > NEG * 0.5))
            g = jnp.where(pick, -jnp.inf, g)
        far_bias = rb_ref[REL_BUCKETS - 1, hd] * LOG2E
        term = jnp.where(sel, jnp.where(jrow <= own - 2, far_bias, 0.0), NEG)
        term = jnp.where(jrow == own, 0.0, term)
        term_hi = term.astype(BF16).astype(F32)
        q_aug_t = jnp.concatenate(
            [zqt[hd * HEAD_DIM:(hd + 1) * HEAD_DIM], term_hi, term - term_hi], axis=0)
        q_ref[0, hd] = q_aug_t.astype(BF16)

    tail_row = lax.broadcasted_iota(jnp.int32, (V_ROWS - HEAD_DIM, W), 0)
    tail = jnp.where(tail_row == 0, 1.0, 0.0)
    for hd in range(A_HEADS):
        for blk in range(ROW_BLOCKS):
            vt_ref[0, hd, blk] = jnp.concatenate(
                [zvt[hd * HEAD_DIM:(hd + 1) * HEAD_DIM, blk * W:(blk + 1) * W], tail],
                axis=0).astype(BF16)

    gsa_ref[0] = _silu(zgat)

    ext_scr[MAX_WINDOW:MAX_WINDOW + T, :] = xb
    lane_b = lax.broadcasted_iota(jnp.int32, (1, B_WIDTH), 1)
    win = jnp.zeros((1, B_WIDTH), F32)
    for gi, w in enumerate(POOL_WINDOWS):
        win = jnp.where(lane_b // B_GROUP == gi, float(w), win)
    halves = []
    for half, (w_small, w_big) in enumerate(zip(POOL_WINDOWS[0::2], POOL_WINDOWS[1::2])):
        cols = slice(half * LANES, (half + 1) * LANES)
        run = xb[:, cols]
        for lag in range(1, w_small):
            run = run + ext_scr[MAX_WINDOW - lag:MAX_WINDOW - lag + T, cols]
        small = run
        for lag in range(w_small, w_big):
            run = run + ext_scr[MAX_WINDOW - lag:MAX_WINDOW - lag + T, cols]
        halves.append(jnp.where(lane < B_GROUP, small, run))
    wsum = jnp.concatenate(halves, axis=1)
    pos = (s * T + lax.broadcasted_iota(jnp.int32, (T, 1), 0) + 1).astype(F32)
    pooled = wsum / jnp.minimum(pos, win) - xb
    ext_scr[0:MAX_WINDOW, :] = xb[T - MAX_WINDOW:T, :]
    mixed_b = _dot(pooled.astype(BF16), poolw_ref[...]) * pscale_ref[...]
    ybc_ref[0, :, 0:B_WIDTH] = (mixed_b * _silu(zgb)).astype(BF16)

    mu = jnp.mean(vc, axis=-1, keepdims=True)
    cen = vc - mu
    var = jnp.mean(cen * cen, axis=-1, keepdims=True)
    vn = cen * lax.rsqrt(var + EPS)
    rows = lax.broadcasted_iota(jnp.int32, (C_HEADS * SGU_CHUNK, C_WIDTH), 0)
    cols = lax.broadcasted_iota(jnp.int32, (C_HEADS * SGU_CHUNK, C_WIDTH), 1)
    head_sel = (rows // SGU_CHUNK) == (cols // HEAD_DIM)
    ug = zuc * _silu(zgc)
    for c in range(T // SGU_CHUNK):
        vn_c = vn[c * SGU_CHUNK:(c + 1) * SGU_CHUNK]
        stack = jnp.where(head_sel, jnp.concatenate([vn_c] * C_HEADS, axis=0), 0.0)
        mixed_c = _dot(sguw_ref[...], stack.astype(BF16)) + sgub_ref[...]
        ybc_ref[0, c * SGU_CHUNK:(c + 1) * SGU_CHUNK, B_WIDTH:B_WIDTH + C_WIDTH] = (
            ug[c * SGU_CHUNK:(c + 1) * SGU_CHUNK] * mixed_c).astype(BF16)


def _proj_call(layer, rel_bias, h, norm_g, w_in, pool_bd, pool_scale, sgu_wcat, sgu_btile):
    B, S, _ = h.shape
    T = ROW_TILE
    nb = S // MOBA_BLOCK
    assert nb <= MAX_BLOCKS and S % T == 0
    full = lambda shape: pl.BlockSpec((None,) + shape, lambda b, s: (layer,) + (0,) * len(shape))
    return pl.pallas_call(
        functools.partial(_proj_kernel, n_blocks=nb),
        grid=(B, S // T),
        in_specs=[
            pl.BlockSpec(memory_space=pltpu.SMEM),
            pl.BlockSpec((1, T, D_MODEL), lambda b, s: (b, s, 0)),
            full((1, D_MODEL)),
            full((D_MODEL, D_IN)),
            full((B_WIDTH, B_WIDTH)),
            full((1, B_WIDTH)),
            full((SGU_CHUNK, C_HEADS * SGU_CHUNK)),
            full((SGU_CHUNK, C_WIDTH)),
        ],
        out_specs=[
            pl.BlockSpec((1, A_HEADS, LANES, T), lambda b, s: (b, 0, 0, s)),
            pl.BlockSpec((1, A_HEADS, ROW_BLOCKS, MOBA_BLOCK, LANES), lambda b, s: (b, 0, s, 0, 0)),
            pl.BlockSpec((1, A_HEADS, ROW_BLOCKS, V_ROWS, MOBA_BLOCK), lambda b, s: (b, 0, s, 0, 0)),
            pl.BlockSpec((1, A_WIDTH, T), lambda b, s: (b, 0, s)),
            pl.BlockSpec((1, T, B_WIDTH + C_WIDTH), lambda b, s: (b, s, 0)),
        ],
        out_shape=[
            jax.ShapeDtypeStruct((B, A_HEADS, LANES, S), BF16),
            jax.ShapeDtypeStruct((B, A_HEADS, nb, MOBA_BLOCK, LANES), BF16),
            jax.ShapeDtypeStruct((B, A_HEADS, nb, V_ROWS, MOBA_BLOCK), BF16),
            jax.ShapeDtypeStruct((B, A_WIDTH, S), F32),
            jax.ShapeDtypeStruct((B, S, B_WIDTH + C_WIDTH), BF16),
        ],
        scratch_shapes=[
            pltpu.VMEM((A_HEADS, nb, LANES), F32),
            pltpu.VMEM((MAX_WINDOW + T, B_WIDTH), F32),
        ],
        compiler_params=pltpu.CompilerParams(
            dimension_semantics=("arbitrary", "arbitrary"), vmem_limit_bytes=VMEM_LIMIT),
        name="proj_mix",
    )(rel_bias, h, norm_g, w_in, pool_bd, pool_scale, sgu_wcat, sgu_btile)


def _attn_kernel(q_ref, qnext_ref, k_ref, vt_ref, bias_ref, gsa_ref, o_ref,
                 sa_scr, sb_scr, m_scr, acc_scr):
    i = pl.program_id(2)
    W = MOBA_BLOCK
    n_pairs = jnp.maximum(i - 1, 0)
    parked = 2 * n_pairs
    before = jnp.maximum(2 * i - 1, 0)
    pen_parked = jnp.where(i >= 1, 0.0, NEG)
    pen_before = jnp.where(i >= 1, 0.0, NEG)
    heads = range(ATTN_HEADS)
    OWN, PREV = 0, 1

    BOTH = (0, 1)

    def park(hd, scr, j, bias_kinds=(None, None), halves=BOTH, queries=q_ref):
        rows = slice(halves[0] * W, (halves[-1] + 1) * W)
        st = _dot(k_ref[0, hd, j], queries[0, hd, :, rows])
        for n, half in enumerate(halves):
            kind = bias_kinds[half]
            part = st[:, n * W:(n + 1) * W]
            cols = slice(half * W, (half + 1) * W)
            scr[hd, :, cols] = part if kind is None else part + bias_ref[hd, kind]

    def consume(hd, scr, j, pen=None, halves=BOTH):
        for half in halves:
            cols = slice(half * W, (half + 1) * W)
            ps, ms = [], []
            for quarter in range(2):
                qcols = slice(half * W + quarter * LANES, half * W + (quarter + 1) * LANES)
                st = scr[hd, :, qcols]
                cm = jnp.max(st, axis=0, keepdims=True)
                m_new = jnp.maximum(m_scr[hd, :, qcols], cm if pen is None else cm + pen)
                shift = m_new if pen is None else m_new - pen
                ps.append(jnp.exp2(st - shift).astype(BF16))
                ms.append(m_new)
            pv = _dot(vt_ref[0, hd, j], jnp.concatenate(ps, axis=1))
            m = m_scr[hd, :, cols]
            m_new = jnp.concatenate(ms, axis=1)
            m_scr[hd, :, cols] = m_new
            acc_scr[hd, :, cols] = jnp.exp2(m - m_new) * acc_scr[hd, :, cols] + pv

    def stage(produce=None, take=None):
        for hd in heads:
            if produce is not None:
                park(hd, *produce)
            if take is not None:
                consume(hd, *take)

    m_scr[...] = jnp.full(m_scr.shape, -jnp.inf, F32)
    acc_scr[...] = jnp.zeros(acc_scr.shape, F32)

    @pl.when(i == 0)
    def _():
        stage(produce=(sa_scr, 0))

    def pair_step(t, carry):
        stage(produce=(sb_scr, 2 * t + 1), take=(sa_scr, 2 * t))
        stage(produce=(sa_scr, 2 * t + 2), take=(sb_scr, 2 * t + 1))
        return carry

    done = 0
    for unroll in PAIR_UNROLLS:
        trips = (n_pairs - done) // unroll

        def body(u, carry, first=done, unroll=unroll):
            for r in range(unroll):
                pair_step(first + u * unroll + r, carry)
            return carry

        lax.fori_loop(0, trips, body, 0)
        done = done + trips * unroll

    second = (1,)
    stage(produce=(sb_scr, before, (PREV, None)), take=(sa_scr, parked, pen_parked))
    stage(produce=(sa_scr, 2 * i, (OWN, PREV)), take=(sb_scr, before, pen_before))
    stage(produce=(sb_scr, 2 * i + 1, (None, OWN), second), take=(sa_scr, 2 * i))
    stage(produce=(sa_scr, 0, (None, None), BOTH, qnext_ref), take=(sb_scr, 2 * i + 1, None, second))
    outs = [acc_scr[hd, 0:HEAD_DIM, :] / acc_scr[hd, HEAD_DIM:HEAD_DIM + 1, :] for hd in heads]
    o_ref[0] = (jnp.concatenate(outs, axis=0) * gsa_ref[0]).astype(BF16)


def _attn_call(q, k_blk, vt_blk, bias_tiles, gsa):
    B, H, _, S = q.shape
    nb = S // MOBA_BLOCK
    Tq = ATTN_TILE
    G = ATTN_HEADS
    last = S // Tq - 1
    return pl.pallas_call(
        _attn_kernel,
        grid=(B, H // G, S // Tq),
        in_specs=[
            pl.BlockSpec((1, G, LANES, Tq), lambda b, hg, i: (b, hg, 0, i)),
            pl.BlockSpec((1, G, LANES, Tq), lambda b, hg, i: (b, hg, 0, jnp.minimum(i + 1, last))),
            pl.BlockSpec((1, G, nb, MOBA_BLOCK, LANES), lambda b, hg, i: (b, hg, 0, 0, 0)),
            pl.BlockSpec((1, G, nb, V_ROWS, MOBA_BLOCK), lambda b, hg, i: (b, hg, 0, 0, 0)),
            pl.BlockSpec((G, 2, MOBA_BLOCK, MOBA_BLOCK), lambda b, hg, i: (hg, 0, 0, 0)),
            pl.BlockSpec((1, G * HEAD_DIM, Tq), lambda b, hg, i: (b, hg, i)),
        ],
        out_specs=pl.BlockSpec((1, G * HEAD_DIM, Tq), lambda b, hg, i: (b, hg, i)),
        out_shape=jax.ShapeDtypeStruct((B, H * HEAD_DIM, S), BF16),
        scratch_shapes=[
            pltpu.VMEM((G, MOBA_BLOCK, Tq), F32),
            pltpu.VMEM((G, MOBA_BLOCK, Tq), F32),
            pltpu.VMEM((G, 1, Tq), F32),
            pltpu.VMEM((G, V_ROWS, Tq), F32),
        ],
        compiler_params=pltpu.CompilerParams(
            dimension_semantics=("arbitrary", "arbitrary", "arbitrary"),
            vmem_limit_bytes=VMEM_LIMIT),
        name="moba_attn",
    )(q, q, k_blk, vt_blk, bias_tiles, gsa)


def _out_kernel(h_ref, ya_ref, ybc_ref, p_ref, wout_ref, plew_ref, gatew_ref,
                fg_ref, o_ref, *, final):
    y = (lax.dot_general(ya_ref[0], wout_ref[0:A_WIDTH, :], (((0,), (0,)), ((), ())),
                         preferred_element_type=F32)
         + _dot(ybc_ref[...], wout_ref[A_WIDTH:, :]))
    h1 = h_ref[...] + y
    emb = _dot(p_ref[...].astype(BF16), plew_ref[...])
    gate = _sigmoid(_dot(h1.astype(BF16), gatew_ref[...]))
    h2 = h1 + emb * gate
    if final:
        h2 = h2 * lax.rsqrt(jnp.mean(h2 * h2, axis=-1, keepdims=True) + EPS) * fg_ref[...]
    o_ref[...] = h2


def _out_call(layer, h, ya, ybc, p, w_out, ple_w, gate_w, final_g, final):
    N = h.shape[0]
    T = OUT_TILE
    tiles = ya.shape[2] // T
    row = lambda width: pl.BlockSpec((T, width), lambda i: (i, 0))
    full = lambda shape: pl.BlockSpec((None,) + shape, lambda i: (layer,) + (0,) * len(shape))
    return pl.pallas_call(
        functools.partial(_out_kernel, final=final),
        grid=(N // T,),
        in_specs=[
            row(D_MODEL),
            pl.BlockSpec((1, A_WIDTH, T), lambda i: (i // tiles, 0, i % tiles)),
            row(B_WIDTH + C_WIDTH),
            pl.BlockSpec((None, T, D_PLE), lambda i: (layer, i, 0)),
            full((D_MODEL, D_MODEL)), full((D_PLE, D_MODEL)), full((D_MODEL, D_MODEL)),
            pl.BlockSpec((1, D_MODEL), lambda i: (0, 0)),
        ],
        out_specs=row(D_MODEL),
        out_shape=jax.ShapeDtypeStruct((N, D_MODEL), F32),
        compiler_params=pltpu.CompilerParams(
            dimension_semantics=("arbitrary",), vmem_limit_bytes=VMEM_LIMIT),
        name="out_ple",
    )(h, ya, ybc, p, w_out, ple_w, gate_w, final_g)


def _block_diag(w):
    G, c, d = w.shape
    eye = jnp.eye(G, dtype=w.dtype)
    return (eye[:, None, :, None] * w[:, :, None, :]).reshape(G * c, G * d)


def kernel(x, p, norm_g, w_in, w_out, rel_bias, pool_w, pool_scale, sgu_w, sgu_b,
           ple_w, ple_gate_w, final_g):
    B, S, D = x.shape
    depth = w_in.shape[0]
    N = B * S
    bias_tiles = _bias_tiles(rel_bias)
    col = jnp.arange(D_IN)
    q_cols = (col >= Q0) & (col < Q0 + A_WIDTH)
    w_in_b = (w_in * jnp.where(q_cols, Q_SCALE, 1.0)).astype(BF16)
    pool_bd = jax.vmap(_block_diag)(pool_w).astype(BF16)
    sgu_wcat = jnp.transpose(jnp.tril(sgu_w), (0, 2, 1, 3)).reshape(
        depth, SGU_CHUNK, C_HEADS * SGU_CHUNK).astype(BF16)
    sgu_btile = jnp.repeat(jnp.transpose(sgu_b, (0, 2, 1)), HEAD_DIM, axis=2)
    w_out_b, ple_w_b, gate_w_b = (w.astype(BF16) for w in (w_out, ple_w, ple_gate_w))
    p_rows = p.reshape(depth, N, D_PLE)
    h = x
    for i in range(depth):
        q, k_blk, vt_blk, gsa, ybc = _proj_call(
            i, rel_bias, h, norm_g[:, None, :], w_in_b, pool_bd,
            pool_scale[:, None, :], sgu_wcat, sgu_btile)
        ya = _attn_call(q, k_blk, vt_blk, bias_tiles, gsa)
        h = _out_call(
            i, h.reshape(N, D), ya, ybc.reshape(N, B_WIDTH + C_WIDTH),
            p_rows, w_out_b, ple_w_b, gate_w_b, final_g[None, :],
            final=(i == depth - 1)).reshape(B, S, D)
    return h
```

```python
import functools
import math

import numpy as np
import jax
import jax.numpy as jnp
from jax import lax
from jax.experimental import pallas as pl
from jax.experimental.pallas import tpu as pltpu

D_MODEL = 1024
HEAD_DIM = 64
A_WIDTH = 512
A_HEADS = 8
MOBA_BLOCK = 256
MOBA_TOPK = 3
REL_BUCKETS = 32
REL_MAX_DIST = 128
B_WIDTH = 256
POOL_WINDOWS = (2, 4, 8, 16)
B_GROUP = 64
C_WIDTH = 256
C_HEADS = 4
SGU_CHUNK = 128
D_PLE = 256
D_IN = 3328
EPS = 1e-6
NEG = -1e30
LOG2E = math.log2(math.e)
Q_SCALE = HEAD_DIM ** -0.5 * LOG2E

MAX_WINDOW = max(POOL_WINDOWS)
ROW_BLOCKS = 2
ROW_TILE = ROW_BLOCKS * MOBA_BLOCK
OUT_TILE = 1024
LANES = 128
BF16_SUBLANES = 16
MAX_BLOCKS = (LANES - HEAD_DIM) // 2
MASK_HI = HEAD_DIM
MASK_LO = MASK_HI + MAX_BLOCKS
V_ROWS = HEAD_DIM + BF16_SUBLANES
ATTN_HEADS = 4
PAIR_UNROLLS = (8, 4, 2, 1)
VMEM_LIMIT = 48 * 1024 * 1024

_OFF = np.cumsum((0,) + (A_WIDTH,) * 4 + (B_WIDTH,) * 2 + (C_WIDTH,) * 3)
Q0, K0, V0, GA0, XB0, GB0, UC0, VC0, GC0, _ = (int(o) for o in _OFF)

BF16 = jnp.bfloat16
F32 = jnp.float32


def _bucket_thresholds():
    max_exact = REL_BUCKETS // 2
    n = np.arange(0, 4 * MOBA_BLOCK)
    nf = np.maximum(n, 1).astype(np.float64)
    large = max_exact + (np.log(nf / max_exact) / math.log(REL_MAX_DIST / max_exact)
                         * (REL_BUCKETS - max_exact)).astype(np.int64)
    large = np.minimum(large, REL_BUCKETS - 1)
    bucket = np.where(n < max_exact, n, large)
    return [int(np.argmax(bucket >= b)) for b in range(1, REL_BUCKETS)]


_THRESHOLDS = _bucket_thresholds()


def _sigmoid(x):
    return 0.5 * jnp.tanh(0.5 * x) + 0.5


def _silu(x):
    return x * _sigmoid(x)


def _split_bf16(x):
    hi = x.astype(BF16)
    lo = (x - hi.astype(F32)).astype(BF16)
    return hi, lo


def _dot(a, b):
    return jnp.dot(a, b, preferred_element_type=F32)


def _dot_tn(a, b):
    return lax.dot_general(a, b, (((0,), (1,)), ((), ())), preferred_element_type=F32)


def _bias_kernel(rb_ref, out_ref):
    h = pl.program_id(0)
    key = lax.broadcasted_iota(jnp.int32, (MOBA_BLOCK, MOBA_BLOCK), 0)
    qry = lax.broadcasted_iota(jnp.int32, (MOBA_BLOCK, MOBA_BLOCK), 1)
    for kind, shift in ((0, 0), (1, MOBA_BLOCK)):
        dist = qry - key + shift
        val = jnp.full((MOBA_BLOCK, MOBA_BLOCK), rb_ref[0, h], F32)
        for b in range(1, REL_BUCKETS):
            val = jnp.where(dist >= _THRESHOLDS[b - 1], rb_ref[b, h], val)
        val = val * LOG2E
        if kind == 0:
            val = jnp.where(dist >= 0, val, NEG)
        out_ref[0, kind] = val


def _bias_tiles(rel_bias):
    return pl.pallas_call(
        _bias_kernel,
        grid=(A_HEADS,),
        in_specs=[pl.BlockSpec(memory_space=pltpu.SMEM)],
        out_specs=pl.BlockSpec((1, 2, MOBA_BLOCK, MOBA_BLOCK), lambda h: (h, 0, 0, 0)),
        out_shape=jax.ShapeDtypeStruct((A_HEADS, 2, MOBA_BLOCK, MOBA_BLOCK), F32),
        name="bias_tiles",
    )(rel_bias)


def _proj_kernel(rb_ref, h_ref, ng_ref, win_ref, poolw_ref, pscale_ref,
                 sguw_ref, sgub_ref,
                 q_ref, k_ref, vt_ref, gsa_ref, ybc_ref,
                 kmt_scr, ext_scr, *, n_blocks):
    T = ROW_TILE
    W = MOBA_BLOCK
    s = pl.program_id(1)
    first_block = s * ROW_BLOCKS

    @pl.when(s == 0)
    def _():
        kmt_scr[...] = jnp.zeros_like(kmt_scr)
        ext_scr[0:MAX_WINDOW, :] = jnp.zeros((MAX_WINDOW, B_WIDTH), F32)

    hbs, zks = [], []
    for blk in range(ROW_BLOCKS):
        h = h_ref[0, blk * W:(blk + 1) * W, :]
        hn = h * lax.rsqrt(jnp.mean(h * h, axis=-1, keepdims=True) + EPS) * ng_ref[...]
        hbs.append(hn.astype(BF16))
        zks.append(_dot(hbs[-1], win_ref[:, K0:K0 + A_WIDTH]))
    hb = jnp.concatenate(hbs, axis=0)

    def proj(c0, width):
        return _dot(hb, win_ref[:, c0:c0 + width])

    zqt = _dot_tn(win_ref[:, Q0:Q0 + A_WIDTH], hb)

    lane = lax.broadcasted_iota(jnp.int32, (1, LANES), 1)
    for blk, zk_blk in enumerate(zks):
        j = first_block + blk
        indicator = jnp.where((lane == MASK_HI + j) | (lane == MASK_LO + j), 1.0, 0.0)
        k_mean = jnp.mean(zk_blk, axis=0, keepdims=True)
        for pair in range(A_HEADS // 2):
            zk_pair = zk_blk[:, pair * LANES:(pair + 1) * LANES]
            zk_odd = pltpu.roll(zk_pair, HEAD_DIM, 1)
            k_ref[0, 2 * pair, blk] = jnp.where(lane < HEAD_DIM, zk_pair, indicator).astype(BF16)
            k_ref[0, 2 * pair + 1, blk] = jnp.where(lane < HEAD_DIM, zk_odd, indicator).astype(BF16)
            mean_pair = k_mean[:, pair * LANES:(pair + 1) * LANES]
            kmt_scr[2 * pair, pl.ds(j, 1), :] = mean_pair
            kmt_scr[2 * pair + 1, pl.ds(j, 1), :] = pltpu.roll(mean_pair, HEAD_DIM, 1)

    q_hi, q_lo = _split_bf16(zqt)
    gates = []
    for hd in range(A_HEADS):
        rows = slice(hd * HEAD_DIM, (hd + 1) * HEAD_DIM)
        km_hi, km_lo = _split_bf16(kmt_scr[hd][:, 0:HEAD_DIM])
        gates.append(_dot(km_hi, q_hi[rows]) + _dot(km_hi, q_lo[rows]) + _dot(km_lo, q_hi[rows]))

    vc = proj(VC0, C_WIDTH)
    zuc = proj(UC0, C_WIDTH)
    zgc = proj(GC0, C_WIDTH)
    xb = proj(XB0, B_WIDTH)
    zgb = proj(GB0, B_WIDTH)
    zgat = _dot_tn(win_ref[:, GA0:GA0 + A_WIDTH], hb)
    zvt = _dot_tn(win_ref[:, V0:V0 + A_WIDTH], hb)

    jrow = lax.broadcasted_iota(jnp.int32, (n_blocks, T), 0)
    own = first_block + lax.broadcasted_iota(jnp.int32, (1, T), 1) // W
    for hd in range(A_HEADS):
        g = jnp.where(jrow < own, gates[hd], NEG)
        sel = jnp.zeros((n_blocks, T), jnp.bool_)
        for _ in range(MOBA_TOPK):
            m = jnp.max(g, axis=0, keepdims=True)
            idx = jnp.min(jnp.where(g == m, jrow, n_blocks), axis=0, keepdims=True)
            pick = jrow == idx
            sel = sel | (pick & (m > NEG * 0.5))
            g = jnp.where(pick, -jnp.inf, g)
        far_bias = rb_ref[REL_BUCKETS - 1, hd] * LOG2E
        term = jnp.where(sel, jnp.where(jrow <= own - 2, far_bias, 0.0), NEG)
        term = jnp.where(jrow == own, 0.0, term)
        term_hi = term.astype(BF16).astype(F32)
        q_aug_t = jnp.concatenate(
            [zqt[hd * HEAD_DIM:(hd + 1) * HEAD_DIM], term_hi, term - term_hi], axis=0)
        q_ref[0, hd] = q_aug_t.astype(BF16)

    tail_row = lax.broadcasted_iota(jnp.int32, (V_ROWS - HEAD_DIM, W), 0)
    tail = jnp.where(tail_row == 0, 1.0, 0.0)
    for hd in range(A_HEADS):
        for blk in range(ROW_BLOCKS):
            vt_ref[0, hd, blk] = jnp.concatenate(
                [zvt[hd * HEAD_DIM:(hd + 1) * HEAD_DIM, blk * W:(blk + 1) * W], tail],
                axis=0).astype(BF16)

    gsa_ref[0] = _silu(zgat)

    ext_scr[MAX_WINDOW:MAX_WINDOW + T, :] = xb
    lane_b = lax.broadcasted_iota(jnp.int32, (1, B_WIDTH), 1)
    win = jnp.zeros((1, B_WIDTH), F32)
    for gi, w in enumerate(POOL_WINDOWS):
        win = jnp.where(lane_b // B_GROUP == gi, float(w), win)
    halves = []
    for half, (w_small, w_big) in enumerate(zip(POOL_WINDOWS[0::2], POOL_WINDOWS[1::2])):
        cols = slice(half * LANES, (half + 1) * LANES)
        run = xb[:, cols]
        for lag in range(1, w_small):
            run = run + ext_scr[MAX_WINDOW - lag:MAX_WINDOW - lag + T, cols]
        small = run
        for lag in range(w_small, w_big):
            run = run + ext_scr[MAX_WINDOW - lag:MAX_WINDOW - lag + T, cols]
        halves.append(jnp.where(lane < B_GROUP, small, run))
    wsum = jnp.concatenate(halves, axis=1)
    pos = (s * T + lax.broadcasted_iota(jnp.int32, (T, 1), 0) + 1).astype(F32)
    pooled = wsum / jnp.minimum(pos, win) - xb
    ext_scr[0:MAX_WINDOW, :] = xb[T - MAX_WINDOW:T, :]
    mixed_b = _dot(pooled.astype(BF16), poolw_ref[...]) * pscale_ref[...]
    ybc_ref[0, :, 0:B_WIDTH] = (mixed_b * _silu(zgb)).astype(BF16)

    mu = jnp.mean(vc, axis=-1, keepdims=True)
    cen = vc - mu
    var = jnp.mean(cen * cen, axis=-1, keepdims=True)
    vn = cen * lax.rsqrt(var + EPS)
    rows = lax.broadcasted_iota(jnp.int32, (C_HEADS * SGU_CHUNK, C_WIDTH), 0)
    cols = lax.broadcasted_iota(jnp.int32, (C_HEADS * SGU_CHUNK, C_WIDTH), 1)
    head_sel = (rows // SGU_CHUNK) == (cols // HEAD_DIM)
    ug = zuc * _silu(zgc)
    for c in range(T // SGU_CHUNK):
        vn_c = vn[c * SGU_CHUNK:(c + 1) * SGU_CHUNK]
        stack = jnp.where(head_sel, jnp.concatenate([vn_c] * C_HEADS, axis=0), 0.0)
        mixed_c = _dot(sguw_ref[...], stack.astype(BF16)) + sgub_ref[...]
        ybc_ref[0, c * SGU_CHUNK:(c + 1) * SGU_CHUNK, B_WIDTH:B_WIDTH + C_WIDTH] = (
            ug[c * SGU_CHUNK:(c + 1) * SGU_CHUNK] * mixed_c).astype(BF16)


def _proj_call(layer, rel_bias, h, norm_g, w_in, pool_bd, pool_scale, sgu_wcat, sgu_btile):
    B, S, _ = h.shape
    T = ROW_TILE
    nb = S // MOBA_BLOCK
    assert nb <= MAX_BLOCKS and S % T == 0
    full = lambda shape: pl.BlockSpec((None,) + shape, lambda b, s: (layer,) + (0,) * len(shape))
    return pl.pallas_call(
        functools.partial(_proj_kernel, n_blocks=nb),
        grid=(B, S // T),
        in_specs=[
            pl.BlockSpec(memory_space=pltpu.SMEM),
            pl.BlockSpec((1, T, D_MODEL), lambda b, s: (b, s, 0)),
            full((1, D_MODEL)),
            full((D_MODEL, D_IN)),
            full((B_WIDTH, B_WIDTH)),
            full((1, B_WIDTH)),
            full((SGU_CHUNK, C_HEADS * SGU_CHUNK)),
            full((SGU_CHUNK, C_WIDTH)),
        ],
        out_specs=[
            pl.BlockSpec((1, A_HEADS, LANES, T), lambda b, s: (b, 0, 0, s)),
            pl.BlockSpec((1, A_HEADS, ROW_BLOCKS, MOBA_BLOCK, LANES), lambda b, s: (b, 0, s, 0, 0)),
            pl.BlockSpec((1, A_HEADS, ROW_BLOCKS, V_ROWS, MOBA_BLOCK), lambda b, s: (b, 0, s, 0, 0)),
            pl.BlockSpec((1, A_WIDTH, T), lambda b, s: (b, 0, s)),
            pl.BlockSpec((1, T, B_WIDTH + C_WIDTH), lambda b, s: (b, s, 0)),
        ],
        out_shape=[
            jax.ShapeDtypeStruct((B, A_HEADS, LANES, S), BF16),
            jax.ShapeDtypeStruct((B, A_HEADS, nb, MOBA_BLOCK, LANES), BF16),
            jax.ShapeDtypeStruct((B, A_HEADS, nb, V_ROWS, MOBA_BLOCK), BF16),
            jax.ShapeDtypeStruct((B, A_WIDTH, S), F32),
            jax.ShapeDtypeStruct((B, S, B_WIDTH + C_WIDTH), BF16),
        ],
        scratch_shapes=[
            pltpu.VMEM((A_HEADS, nb, LANES), F32),
            pltpu.VMEM((MAX_WINDOW + T, B_WIDTH), F32),
        ],
        compiler_params=pltpu.CompilerParams(
            dimension_semantics=("arbitrary", "arbitrary"), vmem_limit_bytes=VMEM_LIMIT,
            allow_input_fusion=[False, False, False, True, False, False, False, False]),
        name="proj_mix",
    )(rel_bias, h, norm_g, w_in, pool_bd, pool_scale, sgu_wcat, sgu_btile)


def _attn_kernel(q_ref, qnext_ref, k_ref, vt_ref, bias_ref, gsa_ref, o_ref,
                 sa_scr, sb_scr, m_scr, acc_scr):
    i = pl.program_id(2)
    W = MOBA_BLOCK
    n_pairs = jnp.maximum(i - 1, 0)
    parked = 2 * n_pairs
    before = jnp.maximum(2 * i - 1, 0)
    pen_parked = jnp.where(i >= 1, 0.0, NEG)
    pen_before = jnp.where(i >= 1, 0.0, NEG)
    heads = range(ATTN_HEADS)
    OWN, PREV = 0, 1

    BOTH = (0, 1)

    def park(hd, scr, j, bias_kinds=(None, None), halves=BOTH, queries=q_ref):
        rows = slice(halves[0] * W, (halves[-1] + 1) * W)
        st = _dot(k_ref[0, hd, j], queries[0, hd, :, rows])
        for n, half in enumerate(halves):
            kind = bias_kinds[half]
            part = st[:, n * W:(n + 1) * W]
            cols = slice(half * W, (half + 1) * W)
            scr[hd, :, cols] = part if kind is None else part + bias_ref[hd, kind]

    def consume(hd, scr, j, pen=None, halves=BOTH):
        for half in halves:
            cols = slice(half * W, (half + 1) * W)
            ps, ms = [], []
            for quarter in range(2):
                qcols = slice(half * W + quarter * LANES, half * W + (quarter + 1) * LANES)
                st = scr[hd, :, qcols]
                cm = jnp.max(st, axis=0, keepdims=True)
                m_new = jnp.maximum(m_scr[hd, :, qcols], cm if pen is None else cm + pen)
                shift = m_new if pen is None else m_new - pen
                ps.append(jnp.exp2(st - shift).astype(BF16))
                ms.append(m_new)
            pv = _dot(vt_ref[0, hd, j], jnp.concatenate(ps, axis=1))
            m = m_scr[hd, :, cols]
            m_new = jnp.concatenate(ms, axis=1)
            m_scr[hd, :, cols] = m_new
            acc_scr[hd, :, cols] = jnp.exp2(m - m_new) * acc_scr[hd, :, cols] + pv

    def stage(produce=None, take=None):
        for hd in heads:
            if produce is not None:
                park(hd, *produce)
            if take is not None:
                consume(hd, *take)

    m_scr[...] = jnp.full(m_scr.shape, -jnp.inf, F32)
    acc_scr[...] = jnp.zeros(acc_scr.shape, F32)

    @pl.when(i == 0)
    def _():
        stage(produce=(sa_scr, 0))

    def pair_step(t, carry):
        stage(produce=(sb_scr, 2 * t + 1), take=(sa_scr, 2 * t))
        stage(produce=(sa_scr, 2 * t + 2), take=(sb_scr, 2 * t + 1))
        return carry

    done = 0
    for unroll in PAIR_UNROLLS:
        trips = (n_pairs - done) // unroll

        def body(u, carry, first=done, unroll=unroll):
            for r in range(unroll):
                pair_step(first + u * unroll + r, carry)
            return carry

        lax.fori_loop(0, trips, body, 0)
        done = done + trips * unroll

    second = (1,)
    stage(produce=(sb_scr, before, (PREV, None)), take=(sa_scr, parked, pen_parked))
    stage(produce=(sa_scr, 2 * i, (OWN, PREV)), take=(sb_scr, before, pen_before))
    stage(produce=(sb_scr, 2 * i + 1, (None, OWN), second), take=(sa_scr, 2 * i))
    stage(produce=(sa_scr, 0, (None, None), BOTH, qnext_ref), take=(sb_scr, 2 * i + 1, None, second))
    outs = [acc_scr[hd, 0:HEAD_DIM, :] / acc_scr[hd, HEAD_DIM:HEAD_DIM + 1, :] for hd in heads]
    o_ref[0] = (jnp.concatenate(outs, axis=0) * gsa_ref[0]).astype(BF16)


def _attn_call(q, k_blk, vt_blk, bias_tiles, gsa):
    B, H, _, S = q.shape
    nb = S // MOBA_BLOCK
    Tq = ROW_TILE
    G = ATTN_HEADS
    last = S // Tq - 1
    return pl.pallas_call(
        _attn_kernel,
        grid=(B, H // G, S // Tq),
        in_specs=[
            pl.BlockSpec((1, G, LANES, Tq), lambda b, hg, i: (b, hg, 0, i)),
            pl.BlockSpec((1, G, LANES, Tq), lambda b, hg, i: (b, hg, 0, jnp.minimum(i + 1, last))),
            pl.BlockSpec((1, G, nb, MOBA_BLOCK, LANES), lambda b, hg, i: (b, hg, 0, 0, 0)),
            pl.BlockSpec((1, G, nb, V_ROWS, MOBA_BLOCK), lambda b, hg, i: (b, hg, 0, 0, 0)),
            pl.BlockSpec((G, 2, MOBA_BLOCK, MOBA_BLOCK), lambda b, hg, i: (hg, 0, 0, 0)),
            pl.BlockSpec((1, G * HEAD_DIM, Tq), lambda b, hg, i: (b, hg, i)),
        ],
        out_specs=pl.BlockSpec((1, G * HEAD_DIM, Tq), lambda b, hg, i: (b, hg, i)),
        out_shape=jax.ShapeDtypeStruct((B, H * HEAD_DIM, S), BF16),
        scratch_shapes=[
            pltpu.VMEM((G, MOBA_BLOCK, Tq), F32),
            pltpu.VMEM((G, MOBA_BLOCK, Tq), F32),
            pltpu.VMEM((G, 1, Tq), F32),
            pltpu.VMEM((G, V_ROWS, Tq), F32),
        ],
        compiler_params=pltpu.CompilerParams(
            dimension_semantics=("arbitrary", "arbitrary", "arbitrary"),
            vmem_limit_bytes=VMEM_LIMIT),
        name="moba_attn",
    )(q, q, k_blk, vt_blk, bias_tiles, gsa)


def _out_kernel(h_ref, ya_ref, ybc_ref, p_ref, wout_ref, plew_ref, gatew_ref,
                fg_ref, o_ref, *, final):
    y = (lax.dot_general(ya_ref[0], wout_ref[0:A_WIDTH, :], (((0,), (0,)), ((), ())),
                         preferred_element_type=F32)
         + _dot(ybc_ref[...], wout_ref[A_WIDTH:, :]))
    h1 = h_ref[...] + y
    emb = _dot(p_ref[...].astype(BF16), plew_ref[...])
    gate = _sigmoid(_dot(h1.astype(BF16), gatew_ref[...]))
    h2 = h1 + emb * gate
    if final:
        h2 = h2 * lax.rsqrt(jnp.mean(h2 * h2, axis=-1, keepdims=True) + EPS) * fg_ref[...]
    o_ref[...] = h2


def _out_call(layer, h, ya, ybc, p, w_out, ple_w, gate_w, final_g, final):
    N = h.shape[0]
    T = OUT_TILE
    tiles = ya.shape[2] // T
    row = lambda width: pl.BlockSpec((T, width), lambda i: (i, 0))
    full = lambda shape: pl.BlockSpec((None,) + shape, lambda i: (layer,) + (0,) * len(shape))
    return pl.pallas_call(
        functools.partial(_out_kernel, final=final),
        grid=(N // T,),
        in_specs=[
            row(D_MODEL),
            pl.BlockSpec((1, A_WIDTH, T), lambda i: (i // tiles, 0, i % tiles)),
            row(B_WIDTH + C_WIDTH),
            pl.BlockSpec((None, T, D_PLE), lambda i: (layer, i, 0)),
            full((D_MODEL, D_MODEL)), full((D_PLE, D_MODEL)), full((D_MODEL, D_MODEL)),
            pl.BlockSpec((1, D_MODEL), lambda i: (0, 0)),
        ],
        out_specs=row(D_MODEL),
        out_shape=jax.ShapeDtypeStruct((N, D_MODEL), F32),
        compiler_params=pltpu.CompilerParams(
            dimension_semantics=("arbitrary",), vmem_limit_bytes=VMEM_LIMIT,
            allow_input_fusion=[False, False, False, False, True, True, True, False]),
        name="out_ple",
    )(h, ya, ybc, p, w_out, ple_w, gate_w, final_g)


def _block_diag(w):
    G, c, d = w.shape
    eye = jnp.eye(G, dtype=w.dtype)
    return (eye[:, None, :, None] * w[:, :, None, :]).reshape(G * c, G * d)


def kernel(x, p, norm_g, w_in, w_out, rel_bias, pool_w, pool_scale, sgu_w, sgu_b,
           ple_w, ple_gate_w, final_g):
    B, S, D = x.shape
    depth = w_in.shape[0]
    N = B * S
    bias_tiles = _bias_tiles(rel_bias)
    col = jnp.arange(D_IN)
    q_cols = (col >= Q0) & (col < Q0 + A_WIDTH)
    w_in_b = (w_in * jnp.where(q_cols, Q_SCALE, 1.0)).astype(BF16)
    pool_bd = jax.vmap(_block_diag)(pool_w).astype(BF16)
    sgu_wcat = jnp.transpose(jnp.tril(sgu_w), (0, 2, 1, 3)).reshape(
        depth, SGU_CHUNK, C_HEADS * SGU_CHUNK).astype(BF16)
    sgu_btile = jnp.repeat(jnp.transpose(sgu_b, (0, 2, 1)), HEAD_DIM, axis=2)
    w_out_b, ple_w_b, gate_w_b = (w.astype(BF16) for w in (w_out, ple_w, ple_gate_w))
    p_rows = p.reshape(depth, N, D_PLE)
    h = x
    for i in range(depth):
        q, k_blk, vt_blk, gsa, ybc = _proj_call(
            i, rel_bias, h, norm_g[:, None, :], w_in_b, pool_bd,
            pool_scale[:, None, :], sgu_wcat, sgu_btile)
        ya = _attn_call(q, k_blk, vt_blk, bias_tiles, gsa)
        h = _out_call(
            i, h.reshape(N, D), ya, ybc.reshape(N, B_WIDTH + C_WIDTH),
            p_rows, w_out_b, ple_w_b, gate_w_b, final_g[None, :],
            final=(i == depth - 1)).reshape(B, S, D)
    return h
```
